```python
import jax, jax.numpy as jnp
from jax import lax
import numpy as np

D_MODEL = 1024
BATCH = 1
SEQ = 16384
DEPTH = 4

HEAD_DIM = 64
N_MIXERS = 3
BAND_BLOCK = 128
A_HEADS = 16
A_KV_HEADS = 2
A_WINDOW = 128
B_HEADS = 16
B_PATTERNS = ((128, 1), (512, 4), (2048, 16))
B_GROUPS = len(B_PATTERNS)
C_HEADS = 16
C_BLOCK = 256
C_TOPK = 3
C_QCHUNK = 64
PEER_HEADS = 8
PEER_NKEYS = 128
PEER_EXPERTS = PEER_NKEYS * PEER_NKEYS
PEER_DKEY = 256
PEER_TOPK = 16
PEER_CHUNK = 128
PLE_DIM = 256
LN_EPS = 1e-5
DEEPNORM_ALPHA = (2 * DEPTH) ** 0.25
DEEPNORM_BETA = (8 * DEPTH) ** -0.25
N_LAYERS_A = (DEPTH + 2) // 3
N_LAYERS_B = (DEPTH + 1) // 3
N_LAYERS_C = DEPTH // 3
A_QKV = (A_HEADS + 2 * A_KV_HEADS) * HEAD_DIM
B_QKV = (B_GROUPS * B_HEADS + 2 * B_HEADS) * HEAD_DIM
C_QKV = 3 * C_HEADS * HEAD_DIM

kernel_name = "hybrid_swa_dilated_moba_peer_deepnorm"


def alibi_slopes(n):
    return jnp.exp2(-8.0 * jnp.arange(1, n + 1, dtype=jnp.float32) / n)


def layer_norm(x, g, b):
    xf = x.astype(jnp.float32)
    mu = xf.mean(-1, keepdims=True)
    var = jnp.mean(jnp.square(xf - mu), -1, keepdims=True)
    return ((xf - mu) * lax.rsqrt(var + LN_EPS) * g.astype(jnp.float32) + b.astype(jnp.float32)).astype(x.dtype)


def banded_attention(q, k, v, slopes, max_dist, dist_scale, sinks=None):
    n, s, hk, g, dh = q.shape
    nb = s // BAND_BLOCK
    qb = q.reshape(n, nb, BAND_BLOCK, hk, g, dh)

    def with_prev(t):
        tb = t.reshape(n, nb, BAND_BLOCK, hk, dh)
        prev = jnp.pad(tb, ((0, 0), (1, 0), (0, 0), (0, 0), (0, 0)))[:, :-1]
        return jnp.concatenate([prev, tb], axis=2)

    kb, vb = with_prev(k), with_prev(v)
    scores = jnp.einsum('nbqhgd,nbkhd->nbhgqk', qb, kb).astype(jnp.float32) * (dh ** -0.5)
    dist = (jnp.arange(BAND_BLOCK)[:, None] + BAND_BLOCK) - jnp.arange(2 * BAND_BLOCK)[None, :]
    valid = (dist >= 0) & (dist <= max_dist)
    first = (jnp.arange(nb) == 0)[:, None, None] & (jnp.arange(2 * BAND_BLOCK) < BAND_BLOCK)[None, None, :]
    valid = valid[None] & ~first
    bias = -(slopes.reshape(hk, g)[:, :, None, None] * (dist_scale * dist.astype(jnp.float32)))
    scores = jnp.where(valid[None, :, None, None], scores + bias[None, None], -jnp.inf)
    m = scores.max(-1)
    if sinks is not None:
        sink = sinks.reshape(hk, g).astype(jnp.float32)[None, None, :, :, None]
        m = jnp.maximum(m, sink)
    e = jnp.exp(scores - m[..., None])
    denom = e.sum(-1)
    if sinks is not None:
        denom = denom + jnp.exp(sink - m)
    probs = (e / denom[..., None]).astype(v.dtype)
    o = jnp.einsum('nbhgqk,nbkhd->nbqhgd', probs, vb).reshape(n, s, hk, g, dh)
    lse = (m + jnp.log(denom)).transpose(0, 1, 4, 2, 3).reshape(n, s, hk, g)
    return o, lse


def mixer_a(x, w_qkv, sinks, w_o):
    b, s, _ = x.shape
    grp = A_HEADS // A_KV_HEADS
    qkv = x @ w_qkv
    q, k, v = jnp.split(qkv, [A_HEADS * HEAD_DIM, (A_HEADS + A_KV_HEADS) * HEAD_DIM], axis=-1)
    q = q.reshape(b, s, A_KV_HEADS, grp, HEAD_DIM)
    k = k.reshape(b, s, A_KV_HEADS, HEAD_DIM)
    v = v.reshape(b, s, A_KV_HEADS, HEAD_DIM)
    o, _ = banded_attention(q, k, v, alibi_slopes(A_HEADS), A_WINDOW - 1, 1.0, sinks)
    return o.reshape(b, s, A_HEADS * HEAD_DIM) @ w_o


def dilate(t, r):
    b, sp = t.shape[:2]
    t = jnp.moveaxis(t.reshape(b, sp // r, r, *t.shape[2:]), 2, 1)
    return t.reshape(b * r, sp // r, *t.shape[3:])


def undilate(t, b, r):
    sr = t.shape[1]
    t = jnp.moveaxis(t.reshape(b, r, sr, *t.shape[2:]), 1, 2)
    return t.reshape(b, sr * r, *t.shape[3:])


def mixer_b(x, w_qkv, w_o):
    b, s, _ = x.shape
    hd = B_HEADS * HEAD_DIM
    qkv = x @ w_qkv
    qs = qkv[..., :B_GROUPS * hd].reshape(b, s, B_GROUPS, B_HEADS, HEAD_DIM)
    k = qkv[..., B_GROUPS * hd:(B_GROUPS + 1) * hd].reshape(b, s, B_HEADS, HEAD_DIM)
    v = qkv[..., (B_GROUPS + 1) * hd:].reshape(b, s, B_HEADS, HEAD_DIM)
    slopes = alibi_slopes(B_HEADS)
    outs, lses = [], []
    for gi, (w, r) in enumerate(B_PATTERNS):
        span = r * BAND_BLOCK
        sp = -(-s // span) * span
        pad = ((0, 0), (0, sp - s), (0, 0), (0, 0))
        qg = dilate(jnp.pad(qs[:, :, gi], pad), r)[:, :, :, None]
        kg = dilate(jnp.pad(k, pad), r)
        vg = dilate(jnp.pad(v, pad), r)
        o, lse = banded_attention(qg, kg, vg, slopes, w // r, float(r))
        outs.append(undilate(o[:, :, :, 0], b, r)[:, :s])
        lses.append(undilate(lse[..., 0], b, r)[:, :s])
    wts = jax.nn.softmax(jnp.stack(lses, 0), axis=0)
    o = jnp.einsum('gbsh,gbshd->bshd', wts.astype(x.dtype), jnp.stack(outs, 0))
    return o.reshape(b, s, hd) @ w_o


def mixer_c(x, w_qkv, w_o):
    b, s, _ = x.shape
    hd = C_HEADS * HEAD_DIM
    scale = HEAD_DIM ** -0.5
    sp = -(-s // C_BLOCK) * C_BLOCK
    nblk = sp // C_BLOCK
    qkv = jnp.pad(x @ w_qkv, ((0, 0), (0, sp - s), (0, 0)))
    q, k, v = jnp.split(qkv, 3, axis=-1)
    q = q.reshape(b, sp, C_HEADS, HEAD_DIM)
    kb = k.reshape(b, nblk, C_BLOCK, C_HEADS, HEAD_DIM)
    vb = v.reshape(b, nblk, C_BLOCK, C_HEADS, HEAD_DIM)
    kmean = kb.mean(axis=2)
    gate = jnp.einsum('bshd,bnhd->bhsn', q, kmean).astype(jnp.float32)
    qblk = jnp.arange(sp) // C_BLOCK
    past = jnp.arange(nblk)[None, :] < qblk[:, None]
    gate = jnp.where(past[None, None], gate, -jnp.inf)
    topk = min(C_TOPK, nblk)
    _, sel = lax.top_k(gate, topk)
    sel_valid = jnp.arange(topk)[None, :] < qblk[:, None]
    kbh = kb.transpose(0, 3, 1, 2, 4)
    vbh = vb.transpose(0, 3, 1, 2, 4)
    slopes = alibi_slopes(C_HEADS)
    bi = jnp.arange(b)[:, None, None, None]
    hi = jnp.arange(C_HEADS)[None, :, None, None]

    def attend_chunk(c):
        t0 = c * C_QCHUNK
        qc = lax.dynamic_slice_in_dim(q, t0, C_QCHUNK, axis=1)
        selc = lax.dynamic_slice_in_dim(sel, t0, C_QCHUNK, axis=2)
        validc = lax.dynamic_slice_in_dim(sel_valid, t0, C_QCHUNK, axis=0)
        ob = t0 // C_BLOCK
        k_own = lax.dynamic_index_in_dim(kb, ob, axis=1, keepdims=False)
        v_own = lax.dynamic_index_in_dim(vb, ob, axis=1, keepdims=False)
        kg = kbh[bi, hi, selc]
        vg = vbh[bi, hi, selc]
        tq = t0 + jnp.arange(C_QCHUNK)
        s_sel = jnp.einsum('bqhd,bhqkjd->bhqkj', qc, kg).astype(jnp.float32) * scale
        pos_sel = selc[..., None] * C_BLOCK + jnp.arange(C_BLOCK)
        d_sel = (tq[:, None, None] - pos_sel).astype(jnp.float32)
        s_sel = jnp.where(validc[None, None, :, :, None], s_sel - slopes[:, None, None, None] * d_sel, -jnp.inf)
        s_own = jnp.einsum('bqhd,bjhd->bhqj', qc, k_own).astype(jnp.float32) * scale
        d_own = tq[:, None] - (ob * C_BLOCK + jnp.arange(C_BLOCK))[None, :]
        s_own = jnp.where((d_own >= 0)[None, None], s_own - slopes[:, None, None] * d_own.astype(jnp.float32), -jnp.inf)
        scores = jnp.concatenate([s_sel.reshape(b, C_HEADS, C_QCHUNK, topk * C_BLOCK), s_own], axis=-1)
        probs = jax.nn.softmax(scores, axis=-1).astype(v.dtype)
        p_sel = probs[..., :topk * C_BLOCK].reshape(b, C_HEADS, C_QCHUNK, topk, C_BLOCK)
        p_own = probs[..., topk * C_BLOCK:]
        return (jnp.einsum('bhqkj,bhqkjd->bqhd', p_sel, vg)
                + jnp.einsum('bhqj,bjhd->bqhd', p_own, v_own))

    o = lax.map(attend_chunk, jnp.arange(sp // C_QCHUNK))
    o = jnp.moveaxis(o, 0, 1).reshape(b, sp, hd)[:, :s]
    return o @ w_o


def peer(x, w_q, subkeys, u, v):
    b, s, d = x.shape
    n_tok = b * s
    t = x.reshape(n_tok, d)
    q = (t @ w_q).astype(jnp.float32).reshape(n_tok, PEER_HEADS, 2, PEER_DKEY // 2)
    sc = jnp.einsum('thcd,ckd->thck', q, subkeys.astype(jnp.float32))
    v1, i1 = lax.top_k(sc[:, :, 0], PEER_TOPK)
    v2, i2 = lax.top_k(sc[:, :, 1], PEER_TOPK)
    cand = (v1[..., :, None] + v2[..., None, :]).reshape(n_tok, PEER_HEADS, PEER_TOPK * PEER_TOPK)
    cidx = (i1[..., :, None] * PEER_NKEYS + i2[..., None, :]).reshape(n_tok, PEER_HEADS, PEER_TOPK * PEER_TOPK)
    best, pos = lax.top_k(cand, PEER_TOPK)
    experts = jnp.take_along_axis(cidx, pos, axis=-1).reshape(n_tok, PEER_HEADS * PEER_TOPK)
    gates = jax.nn.softmax(best, axis=-1).reshape(n_tok, PEER_HEADS * PEER_TOPK)

    def chunk(c):
        t0 = c * PEER_CHUNK
        xc = lax.dynamic_slice_in_dim(t, t0, PEER_CHUNK, axis=0)
        ec = lax.dynamic_slice_in_dim(experts, t0, PEER_CHUNK, axis=0)
        gc = lax.dynamic_slice_in_dim(gates, t0, PEER_CHUNK, axis=0)
        h = jnp.einsum('cd,ced->ce', xc, u[ec])
        a = (jax.nn.gelu(h.astype(jnp.float32), approximate=False) * gc).astype(x.dtype)
        return jnp.einsum('ce,ced->cd', a, v[ec])

    return lax.map(chunk, jnp.arange(n_tok // PEER_CHUNK)).reshape(b, s, d)


def per_layer_embedding(x, p_i, w_p, w_g, b_g):
    gate = jax.nn.sigmoid((x @ w_g + b_g).astype(jnp.float32)).astype(x.dtype)
    return x + gate * (p_i @ w_p)


def setup_inputs(seed: int = 0) -> dict:
    key = jax.random.key(seed)
    ks = jax.random.split(key, 20)
    f32 = jnp.float32
    d = D_MODEL

    def nrm(k, shape, scale):
        return jax.random.normal(k, shape, f32) * scale

    def value_scaled(w, v_start):
        col = jnp.where(jnp.arange(w.shape[-1]) >= v_start, DEEPNORM_BETA, 1.0).astype(f32)
        return w * col

    hd = B_HEADS * HEAD_DIM
    return {
        "x": nrm(ks[0], (BATCH, SEQ, d), 1.0),
        "p": nrm(ks[1], (DEPTH, BATCH, SEQ, PLE_DIM), 1.0),
        "a_w_qkv": value_scaled(nrm(ks[2], (N_LAYERS_A, d, A_QKV), d ** -0.5), (A_HEADS + A_KV_HEADS) * HEAD_DIM),
        "a_sinks": nrm(ks[3], (N_LAYERS_A, A_HEADS), 0.5),
        "a_w_o": nrm(ks[4], (N_LAYERS_A, A_HEADS * HEAD_DIM, d), DEEPNORM_BETA * (A_HEADS * HEAD_DIM) ** -0.5),
        "b_w_qkv": value_scaled(nrm(ks[5], (N_LAYERS_B, d, B_QKV), d ** -0.5), (B_GROUPS + 1) * hd),
        "b_w_o": nrm(ks[6], (N_LAYERS_B, hd, d), DEEPNORM_BETA * hd ** -0.5),
        "c_w_qkv": value_scaled(nrm(ks[7], (N_LAYERS_C, d, C_QKV), d ** -0.5), 2 * C_HEADS * HEAD_DIM),
        "c_w_o": nrm(ks[8], (N_LAYERS_C, C_HEADS * HEAD_DIM, d), DEEPNORM_BETA * (C_HEADS * HEAD_DIM) ** -0.5),
        "ln1_g": 1.0 + nrm(ks[9], (DEPTH, d), 0.02),
        "ln1_b": nrm(ks[10], (DEPTH, d), 0.02),
        "ln2_g": 1.0 + nrm(ks[11], (DEPTH, d), 0.02),
        "ln2_b": nrm(ks[12], (DEPTH, d), 0.02),
        "peer_w_q": nrm(ks[13], (DEPTH, d, PEER_HEADS * PEER_DKEY), d ** -0.5),
        "peer_subkeys": nrm(ks[14], (DEPTH, 2, PEER_NKEYS, PEER_DKEY // 2), (PEER_DKEY // 2) ** -0.5),
        "peer_u": nrm(ks[15], (DEPTH, PEER_EXPERTS, d), d ** -0.5),
        "peer_v": nrm(ks[16], (DEPTH, PEER_EXPERTS, d), DEEPNORM_BETA * PEER_HEADS ** -0.5),
        "ple_w": nrm(ks[17], (DEPTH, PLE_DIM, d), PLE_DIM ** -0.5),
        "ple_gate_w": nrm(ks[18], (DEPTH, d, d), d ** -0.5),
        "ple_gate_b": nrm(ks[19], (DEPTH, d), 0.02),
    }


def reference(x, p, a_w_qkv, a_sinks, a_w_o, b_w_qkv, b_w_o, c_w_qkv, c_w_o,
              ln1_g, ln1_b, ln2_g, ln2_b, peer_w_q, peer_subkeys, peer_u, peer_v,
              ple_w, ple_gate_w, ple_gate_b):
    for i in range(DEPTH):
        kind, j = i % N_MIXERS, i // N_MIXERS
        if kind == 0:
            y = mixer_a(x, a_w_qkv[j], a_sinks[j], a_w_o[j])
        elif kind == 1:
            y = mixer_b(x, b_w_qkv[j], b_w_o[j])
        else:
            y = mixer_c(x, c_w_qkv[j], c_w_o[j])
        x = layer_norm(DEEPNORM_ALPHA * x + y, ln1_g[i], ln1_b[i])
        y = peer(x, peer_w_q[i], peer_subkeys[i], peer_u[i], peer_v[i])
        x = layer_norm(DEEPNORM_ALPHA * x + y, ln2_g[i], ln2_b[i])
        x = per_layer_embedding(x, p[i], ple_w[i], ple_gate_w[i], ple_gate_b[i])
    return x
```

```python
import functools

import jax
import jax.numpy as jnp
from jax import lax
from jax.experimental import pallas as pl
from jax.experimental.pallas import tpu as pltpu

F32 = jnp.float32
BF16 = jnp.bfloat16
NEG_INF = float("-inf")

D_MODEL = 1024
HEAD_DIM = 64
N_HEADS = 16
BAND = 128
A_KV_HEADS = 2
A_WINDOW = 128
B_PATTERNS = ((128, 1), (512, 4), (2048, 16))
C_BLOCK = 256
C_TOPK = 3
PEER_HEADS = 8
PEER_NKEYS = 128
PEER_TOPK = 16
LN_EPS = 1e-5
VMEM_LIMIT = 56 * 1024 * 1024


def _params(*sem):
    return pltpu.CompilerParams(dimension_semantics=sem, vmem_limit_bytes=VMEM_LIMIT)


def _dot(a, b):
    return jnp.dot(a, b, preferred_element_type=F32)


def _dot_nt(a, b):
    return lax.dot_general(a, b, (((1,), (1,)), ((), ())), preferred_element_type=F32)


def _split(a):
    hi = a.astype(BF16)
    lo = (a - hi.astype(F32)).astype(BF16)
    return hi, lo


def _alibi_slope(h, n):
    return 2.0 ** (-8.0 * (h + 1) / n)


def _resident(shape):
    zeros = (0,) * len(shape)
    return pl.BlockSpec(shape, lambda *_: zeros)


def _proj_kernel(x_ref, w_ref, o_ref):
    o_ref[...] = _dot(x_ref[...].astype(BF16), w_ref[...]).astype(o_ref.dtype)


def _proj(x, w, dil=1, tm=512):
    t, k = x.shape
    n = w.shape[1]
    l = t // dil
    tm = min(tm, l)
    return pl.pallas_call(
        _proj_kernel,
        grid=(dil, l // tm),
        in_specs=[pl.BlockSpec((tm, k), lambda c, i: (i, c)), _resident((k, n))],
        out_specs=pl.BlockSpec((None, tm, n), lambda c, i: (c, i, 0)),
        out_shape=jax.ShapeDtypeStruct((dil, l, n), BF16),
        compiler_params=_params("parallel", "parallel"),
        name="proj",
    )(x.reshape(l, dil * k), w)


def _proj_c_kernel(x_ref, wqt_ref, wk_ref, wvt_ref, qt_ref, k_ref, vt_ref, km_ref, *, nblk):
    xb = x_ref[...].astype(BF16)
    qt_ref[...] = _dot_nt(wqt_ref[...], xb)
    kf = _dot(xb, wk_ref[...])
    k_ref[...] = kf.astype(BF16)
    vt_ref[...] = _dot_nt(wvt_ref[...], xb).astype(BF16)
    for r in range(nblk):
        km_ref[r] = jnp.mean(kf[r * C_BLOCK:(r + 1) * C_BLOCK], axis=0, keepdims=True)


def _proj_c(x, wqt, wk, wvt, tm=512):
    t, k = x.shape
    d = wk.shape[1]
    nblk = tm // C_BLOCK
    return pl.pallas_call(
        functools.partial(_proj_c_kernel, nblk=nblk),
        grid=(t // tm,),
        in_specs=[pl.BlockSpec((tm, k), lambda i: (i, 0)), _resident((d, k)), _resident((k, d)),
                  _resident((d, k))],
        out_specs=[pl.BlockSpec((d, tm), lambda i: (0, i)), pl.BlockSpec((tm, d), lambda i: (i, 0)),
                   pl.BlockSpec((d, tm), lambda i: (0, i)), pl.BlockSpec((nblk, 1, d), lambda i: (i, 0, 0))],
        out_shape=[jax.ShapeDtypeStruct((d, t), F32), jax.ShapeDtypeStruct((t, d), BF16),
                   jax.ShapeDtypeStruct((d, t), BF16), jax.ShapeDtypeStruct((t // C_BLOCK, 1, d), F32)],
        compiler_params=_params("parallel"),
        name="proj_moba",
    )(x, wqt, wk, wvt)


def _band_kernel(*refs, kv_group, dist_scale, max_dist, use_sinks, want_lse):
    q_ref, kp_ref, ko_ref, vp_ref, vo_ref = refs[:5]
    pos = 5
    sink_ref = None
    if use_sinks:
        sink_ref = refs[pos]
        pos += 1
    o_ref = refs[pos]
    lse_ref = refs[pos + 1] if want_lse else None

    b = pl.program_id(1)
    qi = lax.broadcasted_iota(jnp.int32, (BAND, 2 * BAND), 0)
    ki = lax.broadcasted_iota(jnp.int32, (BAND, 2 * BAND), 1)
    dist = qi + BAND - ki
    valid = (dist >= 0) & (dist <= max_dist) & (ki >= jnp.where(b > 0, 0, BAND))
    distf = dist.astype(F32)
    k_all = jnp.concatenate([kp_ref[...], ko_ref[...]], axis=0)
    v_all = jnp.concatenate([vp_ref[...], vo_ref[...]], axis=0)
    outs, lses = [], []
    for h in range(N_HEADS):
        g = h // kv_group
        qh = q_ref[:, h * HEAD_DIM:(h + 1) * HEAD_DIM]
        kh = k_all[:, g * HEAD_DIM:(g + 1) * HEAD_DIM]
        vh = v_all[:, g * HEAD_DIM:(g + 1) * HEAD_DIM]
        s = _dot_nt(qh, kh) * (HEAD_DIM ** -0.5)
        s = jnp.where(valid, s - (_alibi_slope(h, N_HEADS) * dist_scale) * distf, NEG_INF)
        m = jnp.max(s, axis=-1, keepdims=True)
        if use_sinks:
            sk = sink_ref[:, h:h + 1]
            m = jnp.maximum(m, sk)
        e = jnp.exp(s - m)
        den = jnp.sum(e, axis=-1, keepdims=True)
        if use_sinks:
            den = den + jnp.exp(sk - m)
        p = (e / den).astype(BF16)
        outs.append(_dot(p, vh))
        if want_lse:
            lses.append(jnp.broadcast_to(m + jnp.log(den), (BAND, HEAD_DIM)))
    o_ref[...] = jnp.concatenate(outs, axis=1).astype(o_ref.dtype)
    if want_lse:
        lse_ref[...] = jnp.concatenate(lses, axis=1)


def _band_attention(qkv, *, q_blk, k_blk, v_blk, kv_width, kv_group, dist_scale, max_dist,
                    sinks=None, want_lse=False):
    r, l, _ = qkv.shape
    d = N_HEADS * HEAD_DIM
    prev = lambda blk: (lambda c, b: (c, jnp.maximum(b - 1, 0), blk))
    own = lambda blk: (lambda c, b: (c, b, blk))
    in_specs = [pl.BlockSpec((None, BAND, d), own(q_blk)),
                pl.BlockSpec((None, BAND, kv_width), prev(k_blk)),
                pl.BlockSpec((None, BAND, kv_width), own(k_blk)),
                pl.BlockSpec((None, BAND, kv_width), prev(v_blk)),
                pl.BlockSpec((None, BAND, kv_width), own(v_blk))]
    args = [qkv] * 5
    if sinks is not None:
        in_specs.append(_resident((1, N_HEADS)))
        args.append(sinks.reshape(1, N_HEADS).astype(F32))
    o_spec = pl.BlockSpec((BAND, d), lambda c, b: (b, c))
    out_specs = [o_spec]
    out_shape = [jax.ShapeDtypeStruct((l, r * d), BF16)]
    if want_lse:
        out_specs.append(o_spec)
        out_shape.append(jax.ShapeDtypeStruct((l, r * d), F32))
    outs = pl.pallas_call(
        functools.partial(_band_kernel, kv_group=kv_group, dist_scale=dist_scale, max_dist=max_dist,
                          use_sinks=sinks is not None, want_lse=want_lse),
        grid=(r, l // BAND),
        in_specs=in_specs, out_specs=out_specs, out_shape=out_shape,
        compiler_params=_params("parallel", "parallel"),
        name="band_attention",
    )(*args)
    return [o.reshape(l * r, d) for o in outs]


def _dot3(a, b):
    ah, al = _split(a)
    bh, bl = _split(b)
    return _dot(ah, bh) + (_dot(ah, bl) + _dot(al, bh))


def _dot3_nt(a, b):
    ah, al = _split(a)
    bh, bl = _split(b)
    return _dot_nt(ah, bh) + (_dot_nt(ah, bl) + _dot_nt(al, bh))


def _moba_kernel(slope_ref, qt_ref, k_ref, vt_ref, km_ref, o_ref, sel_ref, *, nblk):
    hp = pl.program_id(0)
    qi = pl.program_id(1)
    tq = C_BLOCK
    qt = qt_ref[...]
    rows = lax.broadcasted_iota(jnp.int32, (2 * HEAD_DIM, tq), 0)
    krow = lax.broadcasted_iota(jnp.int32, (C_BLOCK, tq), 0)
    qcol = lax.broadcasted_iota(jnp.int32, (C_BLOCK, tq), 1)
    rel = (qcol - krow).astype(F32)
    blk = lax.broadcasted_iota(jnp.int32, (nblk, tq), 0)
    past = blk < qi
    outs = []
    for hh in range(2):
        slope = slope_ref[2 * hp + hh]
        qm = jnp.where((rows >= hh * HEAD_DIM) & (rows < (hh + 1) * HEAD_DIM), qt, 0.0)
        gate = jnp.where(past, _dot3(km_ref[...], qm), NEG_INF)
        g = gate
        for r in range(C_TOPK):
            thr = jnp.max(g, axis=0, keepdims=True)
            if r < C_TOPK - 1:
                g = jnp.where(g >= thr, NEG_INF, g)
        sel_ref[...] = jnp.where(past & (gate >= thr), 1.0, 0.0)
        qb = (qm * (HEAD_DIM ** -0.5)).astype(BF16)
        vrows = pl.ds(hh * HEAD_DIM, HEAD_DIM)

        k0 = pl.multiple_of(qi * C_BLOCK, C_BLOCK)
        s = _dot(k_ref[pl.ds(k0, C_BLOCK), :], qb)
        s = jnp.where(rel >= 0, s - slope * rel, NEG_INF)
        m0 = jnp.max(s, axis=0, keepdims=True)
        p = jnp.exp(s - m0)
        l0 = jnp.sum(p, axis=0, keepdims=True)
        acc0 = _dot(vt_ref[vrows, pl.ds(k0, C_BLOCK)], p.astype(BF16))

        def body(n, carry):
            m_i, l_i, acc = carry
            kn = pl.multiple_of(n * C_BLOCK, C_BLOCK)
            sn = _dot(k_ref[pl.ds(kn, C_BLOCK), :], qb)
            off = ((qi - n) * C_BLOCK).astype(F32)
            sn = sn - slope * (rel + off)
            sn = jnp.where(sel_ref[pl.ds(n, 1), :] > 0.5, sn, NEG_INF)
            m_new = jnp.maximum(m_i, jnp.max(sn, axis=0, keepdims=True))
            pn = jnp.exp(sn - m_new)
            a = jnp.exp(m_i - m_new)
            l_new = a * l_i + jnp.sum(pn, axis=0, keepdims=True)
            acc_new = a * acc + _dot(vt_ref[vrows, pl.ds(kn, C_BLOCK)], pn.astype(BF16))
            return m_new, l_new, acc_new

        _, l_f, acc_f = lax.fori_loop(0, qi, body, (m0, l0, acc0))
        outs.append(acc_f / l_f)
    o_ref[...] = jnp.concatenate(outs, axis=0).T.astype(o_ref.dtype)


def _moba_attention(qt, k, vt, kmean):
    d, t = qt.shape
    nblk = t // C_BLOCK
    slopes = jnp.asarray([_alibi_slope(h, N_HEADS) for h in range(N_HEADS)], F32)
    pw = 2 * HEAD_DIM
    return pl.pallas_call(
        functools.partial(_moba_kernel, nblk=nblk),
        grid=(d // pw, nblk),
        in_specs=[pl.BlockSpec(memory_space=pltpu.SMEM),
                  pl.BlockSpec((pw, C_BLOCK), lambda hp, i: (hp, i)),
                  pl.BlockSpec((t, pw), lambda hp, i: (0, hp)),
                  pl.BlockSpec((pw, t), lambda hp, i: (hp, 0)),
                  pl.BlockSpec((nblk, pw), lambda hp, i: (0, hp))],
        out_specs=pl.BlockSpec((C_BLOCK, pw), lambda hp, i: (i, hp)),
        out_shape=jax.ShapeDtypeStruct((t, d), BF16),
        scratch_shapes=[pltpu.VMEM((nblk, C_BLOCK), F32)],
        compiler_params=_params("parallel", "arbitrary"),
        name="moba_attention",
    )(slopes, qt, k, vt, kmean)


def _layer_norm(z, g, b):
    mu = jnp.mean(z, axis=-1, keepdims=True)
    zc = z - mu
    var = jnp.mean(zc * zc, axis=-1, keepdims=True)
    return zc * lax.rsqrt(var + LN_EPS) * g + b


def _outproj_ln_kernel(*refs, n_groups, alpha):
    o_refs = refs[:n_groups]
    lse_refs = refs[n_groups:2 * n_groups] if n_groups > 1 else ()
    pos = n_groups + len(lse_refs)
    wo_ref, x_ref, g_ref, b_ref, xn_ref, xb_ref = refs[pos:pos + 6]
    if n_groups == 1:
        o = o_refs[0][...]
    else:
        lses = [r[...] for r in lse_refs]
        m = functools.reduce(jnp.maximum, lses)
        es = [jnp.exp(l - m) for l in lses]
        den = functools.reduce(jnp.add, es)
        o = functools.reduce(jnp.add, [(e / den) * r[...].astype(F32) for e, r in zip(es, o_refs)])
        o = o.astype(BF16)
    y = _dot(o, wo_ref[...])
    xn = _layer_norm(alpha * x_ref[...] + y, g_ref[...], b_ref[...])
    xn_ref[...] = xn
    xb_ref[...] = xn.astype(BF16)


def _outproj_ln(os_, lses, wo, x, g, b, alpha, tm=512):
    t, d = x.shape
    n = len(os_)
    tile = pl.BlockSpec((tm, d), lambda i: (i, 0))
    return pl.pallas_call(
        functools.partial(_outproj_ln_kernel, n_groups=n, alpha=alpha),
        grid=(t // tm,),
        in_specs=[tile] * (n + len(lses)) + [_resident((d, d)), tile, _resident((1, d)), _resident((1, d))],
        out_specs=[tile, tile],
        out_shape=[jax.ShapeDtypeStruct((t, d), F32), jax.ShapeDtypeStruct((t, d), BF16)],
        compiler_params=_params("parallel"),
        name="outproj_ln",
    )(*os_, *lses, wo, x, g.reshape(1, d), b.reshape(1, d))


def _top_vals(s, k):
    out = []
    for r in range(k):
        m = jnp.max(s, axis=0, keepdims=True)
        out.append(m)
        if r < k - 1:
            s = jnp.where(s >= m, NEG_INF, s)
    return out


_CAND = [(i, j) for i in range(PEER_TOPK + 1) for j in range(PEER_TOPK + 1)
         if (i + 1) * (j + 1) <= PEER_TOPK + 1]
_NCAND = -(-len(_CAND) // 8) * 8


def _router_kernel(x_ref, wh_ref, wl_ref, sk_ref, s2_ref, e2_ref, a_ref, w_ref, q_ref, cand_ref):
    xh, xl = _split(x_ref[...])
    q_ref[...] = _dot(xh, wh_ref[...]) + (_dot(xh, wl_ref[...]) + _dot(xl, wh_ref[...]))
    tt = x_ref.shape[0]
    cand_ref[...] = jnp.full((_NCAND, tt), NEG_INF, F32)
    nk = PEER_NKEYS
    for h in range(PEER_HEADS):
        s1 = _dot3_nt(sk_ref[0], q_ref[:, (2 * h) * nk:(2 * h + 1) * nk])
        s2 = _dot3_nt(sk_ref[1], q_ref[:, (2 * h + 1) * nk:(2 * h + 2) * nk])
        a = _top_vals(s1, PEER_TOPK + 1)
        b = _top_vals(s2, PEER_TOPK + 1)
        for r, (i, j) in enumerate(_CAND):
            cand_ref[r:r + 1, :] = a[i] + b[j]
        v = _top_vals(cand_ref[...], PEER_TOPK + 1)
        thr = 0.5 * (v[PEER_TOPK - 1] + v[PEER_TOPK])
        z = functools.reduce(jnp.add, [jnp.exp(v[r] - v[0]) for r in range(PEER_TOPK)])
        s2_ref[h] = s2
        e2_ref[h] = jnp.exp(s2 - b[0])
        a_ref[h] = thr - s1
        w_ref[h] = jnp.exp(s1 - a[0]) / z


def _router(xn, wq_hi, wq_lo, subkeys, tt=256):
    t, d = xn.shape
    nq = wq_hi.shape[1]
    out = jax.ShapeDtypeStruct((PEER_HEADS, PEER_NKEYS, t), F32)
    ospec = pl.BlockSpec((PEER_HEADS, PEER_NKEYS, tt), lambda i: (0, 0, i))
    return pl.pallas_call(
        _router_kernel,
        grid=(t // tt,),
        in_specs=[pl.BlockSpec((tt, d), lambda i: (i, 0)), _resident((d, nq)), _resident((d, nq)),
                  _resident(subkeys.shape)],
        out_specs=[ospec] * 4,
        out_shape=[out] * 4,
        scratch_shapes=[pltpu.VMEM((tt, nq), F32), pltpu.VMEM((_NCAND, tt), F32)],
        compiler_params=_params("parallel"),
        name="peer_router",
    )(xn, wq_hi, wq_lo, subkeys)


def _gelu(x):
    return 0.5 * x * (1.0 + lax.erf(x * (2.0 ** -0.5)))


def _peer_kernel(x_ref, u_ref, vt_ref, s2_ref, e2_ref, a_ref, w_ref, y_ref, acc_ref, h_ref, act_ref):
    j = pl.program_id(1)
    te, tt = h_ref.shape
    nk = PEER_NKEYS

    @pl.when(j == 0)
    def _():
        acc_ref[...] = jnp.zeros_like(acc_ref)

    h_ref[...] = _dot_nt(u_ref[...], x_ref[...])
    n_i1 = te // nk
    i1_rows = pl.ds(pl.multiple_of(j * n_i1, n_i1), n_i1)
    for ii in range(n_i1):
        for c in range(tt // 128):
            cols = pl.ds(c * 128, 128)
            g = jnp.zeros((nk, 128), F32)
            for h in range(PEER_HEADS):
                a_row = a_ref[h, i1_rows, cols][ii:ii + 1]
                w_row = w_ref[h, i1_rows, cols][ii:ii + 1]
                g = g + jnp.where(s2_ref[h, :, cols] >= a_row, e2_ref[h, :, cols], 0.0) * w_row
            rows = pl.ds(ii * nk, nk)
            act_ref[rows, cols] = (_gelu(h_ref[rows, cols]) * g).astype(BF16)
    acc_ref[...] += _dot(vt_ref[...], act_ref[...])

    @pl.when(j == pl.num_programs(1) - 1)
    def _():
        y_ref[...] = acc_ref[...].T


def _peer_experts(xb, u, vt, s2, e2, a, w, tt=512, te=1024):
    t, d = xb.shape
    ne = u.shape[0]
    rspec = pl.BlockSpec((PEER_HEADS, PEER_NKEYS, tt), lambda i, j: (0, 0, i))
    return pl.pallas_call(
        _peer_kernel,
        grid=(t // tt, ne // te),
        in_specs=[pl.BlockSpec((tt, d), lambda i, j: (i, 0)),
                  pl.BlockSpec((te, d), lambda i, j: (j, 0)),
                  pl.BlockSpec((d, te), lambda i, j: (0, j)),
                  rspec, rspec, rspec, rspec],
        out_specs=pl.BlockSpec((tt, d), lambda i, j: (i, 0)),
        out_shape=jax.ShapeDtypeStruct((t, d), F32),
        scratch_shapes=[pltpu.VMEM((d, tt), F32), pltpu.VMEM((te, tt), F32), pltpu.VMEM((te, tt), BF16)],
        compiler_params=_params("parallel", "arbitrary"),
        name="peer_experts",
    )(xb, u, vt, s2, e2, a, w)


def _ln_ple_kernel(x_ref, y_ref, g_ref, b_ref, p_ref, wp_ref, wg_ref, bg_ref, o_ref, *, alpha):
    xn = _layer_norm(alpha * x_ref[...] + y_ref[...], g_ref[...], b_ref[...])
    gate = jax.nn.sigmoid(_dot(xn.astype(BF16), wg_ref[...]) + bg_ref[...])
    o_ref[...] = xn + gate * _dot(p_ref[...].astype(BF16), wp_ref[...])


def _ln_ple(x, y, g, b, p, wp, wg, bg, alpha, tm=512):
    t, d = x.shape
    dp = p.shape[1]
    tile = pl.BlockSpec((tm, d), lambda i: (i, 0))
    vec = _resident((1, d))
    return pl.pallas_call(
        functools.partial(_ln_ple_kernel, alpha=alpha),
        grid=(t // tm,),
        in_specs=[tile, tile, vec, vec, pl.BlockSpec((tm, dp), lambda i: (i, 0)), _resident((dp, d)),
                  _resident((d, d)), vec],
        out_specs=tile,
        out_shape=jax.ShapeDtypeStruct((t, d), F32),
        compiler_params=_params("parallel"),
        name="ln_ple",
    )(x, y, g.reshape(1, d), b.reshape(1, d), p, wp, wg, bg.reshape(1, d))


def _mixer_a(x, w_qkv, sinks):
    d = N_HEADS * HEAD_DIM
    qkv = _proj(x, w_qkv.astype(BF16))
    kvw = A_KV_HEADS * HEAD_DIM
    o, = _band_attention(qkv, q_blk=0, k_blk=d // kvw, v_blk=d // kvw + 1, kv_width=kvw,
                         kv_group=N_HEADS // A_KV_HEADS, dist_scale=1.0, max_dist=A_WINDOW - 1,
                         sinks=sinks)
    return [o], []


def _mixer_b(x, w_qkv):
    d = N_HEADS * HEAD_DIM
    ng = len(B_PATTERNS)
    wk = w_qkv[:, ng * d:(ng + 1) * d]
    wv = w_qkv[:, (ng + 1) * d:]
    os_, lses = [], []
    for gi, (w, r) in enumerate(B_PATTERNS):
        wg = jnp.concatenate([w_qkv[:, gi * d:(gi + 1) * d], wk, wv], axis=1).astype(BF16)
        qkv = _proj(x, wg, dil=r)
        o, lse = _band_attention(qkv, q_blk=0, k_blk=1, v_blk=2, kv_width=d, kv_group=1,
                                 dist_scale=float(r), max_dist=w // r, want_lse=True)
        os_.append(o)
        lses.append(lse)
    return os_, lses


def _mixer_c(x, w_qkv):
    d = N_HEADS * HEAD_DIM
    wq, wk, wv = w_qkv[:, :d], w_qkv[:, d:2 * d], w_qkv[:, 2 * d:]
    qt, k, vt, kmean = _proj_c(x, wq.T.astype(BF16), wk.astype(BF16), wv.T.astype(BF16))
    return [_moba_attention(qt, k, vt, kmean.reshape(-1, d))], []


def kernel(x, p, a_w_qkv, a_sinks, a_w_o, b_w_qkv, b_w_o, c_w_qkv, c_w_o, ln1_g, ln1_b, ln2_g, ln2_b,
           peer_w_q, peer_subkeys, peer_u, peer_v, ple_w, ple_gate_w, ple_gate_b):
    depth = p.shape[0]
    alpha = (2 * depth) ** 0.25
    bsz, seq, d = x.shape
    assert bsz == 1 and seq % (B_PATTERNS[-1][1] * BAND) == 0
    xt = x.reshape(seq, d)
    for i in range(depth):
        kind, j = i % 3, i // 3
        if kind == 0:
            os_, lses = _mixer_a(xt, a_w_qkv[j], a_sinks[j])
            wo = a_w_o[j]
        elif kind == 1:
            os_, lses = _mixer_b(xt, b_w_qkv[j])
            wo = b_w_o[j]
        else:
            os_, lses = _mixer_c(xt, c_w_qkv[j])
            wo = c_w_o[j]
        x1, x1b = _outproj_ln(os_, lses, wo.astype(BF16), xt, ln1_g[i], ln1_b[i], alpha)
        wq_hi, wq_lo = _split(peer_w_q[i])
        s2, e2, a, w = _router(x1, wq_hi, wq_lo, peer_subkeys[i])
        y = _peer_experts(x1b, peer_u[i].astype(BF16), peer_v[i].T.astype(BF16), s2, e2, a, w)
        xt = _ln_ple(x1, y, ln2_g[i], ln2_b[i], p[i].reshape(seq, -1), ple_w[i].astype(BF16),
                     ple_gate_w[i].astype(BF16), ple_gate_b[i], alpha)
    return xt.reshape(bsz, seq, d)
```

```python
import functools

import jax
import jax.numpy as jnp
from jax import lax
from jax.experimental import pallas as pl
from jax.experimental.pallas import tpu as pltpu

F32 = jnp.float32
BF16 = jnp.bfloat16
NEG_INF = float("-inf")
BIG = 1e30
LOG2E = 1.4426950408889634

D_MODEL = 1024
HEAD_DIM = 64
N_HEADS = 16
BAND = 128
A_KV_HEADS = 2
A_WINDOW = 128
B_PATTERNS = ((128, 1), (512, 4), (2048, 16))
C_BLOCK = 256
C_TOPK = 3
PEER_HEADS = 8
PEER_NKEYS = 128
PEER_TOPK = 16
LN_EPS = 1e-5
VMEM_LIMIT = 56 * 1024 * 1024


def _params(*sem):
    return pltpu.CompilerParams(dimension_semantics=sem, vmem_limit_bytes=VMEM_LIMIT)


def _dot(a, b):
    return jnp.dot(a, b, preferred_element_type=F32)


def _dot_nt(a, b):
    return lax.dot_general(a, b, (((1,), (1,)), ((), ())), preferred_element_type=F32)


def _split(a):
    hi = a.astype(BF16)
    lo = (a - hi.astype(F32)).astype(BF16)
    return hi, lo


def _alibi_slope(h, n):
    return 2.0 ** (-8.0 * (h + 1) / n)


def _resident(shape):
    zeros = (0,) * len(shape)
    return pl.BlockSpec(shape, lambda *_: zeros)


def _proj_kernel(x_ref, w_ref, o_ref):
    o_ref[...] = _dot(x_ref[...].astype(BF16), w_ref[...]).astype(o_ref.dtype)


def _proj(x, w, dil=1, tm=512):
    t, k = x.shape
    n = w.shape[1]
    l = t // dil
    tm = min(tm, l)
    return pl.pallas_call(
        _proj_kernel,
        grid=(dil, l // tm),
        in_specs=[pl.BlockSpec((tm, k), lambda c, i: (i, c)), _resident((k, n))],
        out_specs=pl.BlockSpec((None, tm, n), lambda c, i: (c, i, 0)),
        out_shape=jax.ShapeDtypeStruct((dil, l, n), BF16),
        compiler_params=_params("parallel", "parallel"),
        name="proj",
    )(x.reshape(l, dil * k), w)


def _proj_c_kernel(x_ref, wqt_ref, wk_ref, wvt_ref, qt_ref, k_ref, vt_ref, km_ref, *, nblk):
    xb = x_ref[...].astype(BF16)
    qt_ref[...] = _dot_nt(wqt_ref[...], xb)
    kf = _dot(xb, wk_ref[...])
    k_ref[...] = kf.astype(BF16)
    vt_ref[...] = _dot_nt(wvt_ref[...], xb).astype(BF16)
    for r in range(nblk):
        km_ref[r] = jnp.mean(kf[r * C_BLOCK:(r + 1) * C_BLOCK], axis=0, keepdims=True)


def _proj_c(x, wqt, wk, wvt, tm=512):
    t, k = x.shape
    d = wk.shape[1]
    nblk = tm // C_BLOCK
    return pl.pallas_call(
        functools.partial(_proj_c_kernel, nblk=nblk),
        grid=(t // tm,),
        in_specs=[pl.BlockSpec((tm, k), lambda i: (i, 0)), _resident((d, k)), _resident((k, d)),
                  _resident((d, k))],
        out_specs=[pl.BlockSpec((d, tm), lambda i: (0, i)), pl.BlockSpec((tm, d), lambda i: (i, 0)),
                   pl.BlockSpec((d, tm), lambda i: (0, i)), pl.BlockSpec((nblk, 1, d), lambda i: (i, 0, 0))],
        out_shape=[jax.ShapeDtypeStruct((d, t), F32), jax.ShapeDtypeStruct((t, d), BF16),
                   jax.ShapeDtypeStruct((d, t), BF16), jax.ShapeDtypeStruct((t // C_BLOCK, 1, d), F32)],
        compiler_params=_params("parallel"),
        name="proj_moba",
    )(x, wqt, wk, wvt)


def _band_kernel(*refs, kv_group, dist_scale, max_dist, use_sinks, want_lse):
    q_ref, kp_ref, ko_ref, vp_ref, vo_ref = refs[:5]
    pos = 5
    sink_ref = None
    if use_sinks:
        sink_ref = refs[pos]
        pos += 1
    o_ref = refs[pos]
    lse_ref = refs[pos + 1] if want_lse else None

    b = pl.program_id(1)
    qi = lax.broadcasted_iota(jnp.int32, (BAND, 2 * BAND), 0)
    ki = lax.broadcasted_iota(jnp.int32, (BAND, 2 * BAND), 1)
    dist = qi + BAND - ki
    valid = (dist >= 0) & (dist <= max_dist) & (ki >= jnp.where(b > 0, 0, BAND))
    distf = dist.astype(F32)
    k_all = jnp.concatenate([kp_ref[...], ko_ref[...]], axis=0)
    v_all = jnp.concatenate([vp_ref[...], vo_ref[...]], axis=0)
    outs, lses = [], []
    for h in range(N_HEADS):
        g = h // kv_group
        qh = q_ref[:, h * HEAD_DIM:(h + 1) * HEAD_DIM]
        kh = k_all[:, g * HEAD_DIM:(g + 1) * HEAD_DIM]
        vh = v_all[:, g * HEAD_DIM:(g + 1) * HEAD_DIM]
        s = _dot_nt(qh, kh) * (HEAD_DIM ** -0.5)
        s = jnp.where(valid, s - (_alibi_slope(h, N_HEADS) * dist_scale) * distf, NEG_INF)
        m = jnp.max(s, axis=-1, keepdims=True)
        if use_sinks:
            sk = sink_ref[:, h:h + 1]
            m = jnp.maximum(m, sk)
        e = jnp.exp(s - m)
        den = jnp.sum(e, axis=-1, keepdims=True)
        if use_sinks:
            den = den + jnp.exp(sk - m)
        p = (e / den).astype(BF16)
        outs.append(_dot(p, vh))
        if want_lse:
            lses.append(jnp.broadcast_to(m + jnp.log(den), (BAND, HEAD_DIM)))
    o_ref[...] = jnp.concatenate(outs, axis=1).astype(o_ref.dtype)
    if want_lse:
        lse_ref[...] = jnp.concatenate(lses, axis=1)


def _band_attention(qkv, *, q_blk, k_blk, v_blk, kv_width, kv_group, dist_scale, max_dist,
                    sinks=None, want_lse=False):
    r, l, _ = qkv.shape
    d = N_HEADS * HEAD_DIM
    prev = lambda blk: (lambda c, b: (c, jnp.maximum(b - 1, 0), blk))
    own = lambda blk: (lambda c, b: (c, b, blk))
    in_specs = [pl.BlockSpec((None, BAND, d), own(q_blk)),
                pl.BlockSpec((None, BAND, kv_width), prev(k_blk)),
                pl.BlockSpec((None, BAND, kv_width), own(k_blk)),
                pl.BlockSpec((None, BAND, kv_width), prev(v_blk)),
                pl.BlockSpec((None, BAND, kv_width), own(v_blk))]
    args = [qkv] * 5
    if sinks is not None:
        in_specs.append(_resident((1, N_HEADS)))
        args.append(sinks.reshape(1, N_HEADS).astype(F32))
    o_spec = pl.BlockSpec((BAND, d), lambda c, b: (b, c))
    out_specs = [o_spec]
    out_shape = [jax.ShapeDtypeStruct((l, r * d), BF16)]
    if want_lse:
        out_specs.append(o_spec)
        out_shape.append(jax.ShapeDtypeStruct((l, r * d), F32))
    outs = pl.pallas_call(
        functools.partial(_band_kernel, kv_group=kv_group, dist_scale=dist_scale, max_dist=max_dist,
                          use_sinks=sinks is not None, want_lse=want_lse),
        grid=(r, l // BAND),
        in_specs=in_specs, out_specs=out_specs, out_shape=out_shape,
        compiler_params=_params("parallel", "parallel"),
        name="band_attention",
    )(*args)
    return [o.reshape(l * r, d) for o in outs]


def _dot3(a, b):
    ah, al = _split(a)
    bh, bl = _split(b)
    return _dot(ah, bh) + (_dot(ah, bl) + _dot(al, bh))


def _dot3_nt(a, b):
    ah, al = _split(a)
    bh, bl = _split(b)
    return _dot_nt(ah, bh) + (_dot_nt(ah, bl) + _dot_nt(al, bh))


def _split3(a):
    h = a.astype(BF16).astype(F32)
    r = a - h
    m = r.astype(BF16).astype(F32)
    l = (r - m).astype(BF16).astype(F32)
    return h, m, l


_KV_STEP = 2 * C_BLOCK
_N_AUG = 16


def _moba_kernel(slope_ref, qt_ref, k_ref, vt_ref, km_ref, o_ref, sel_ref, s_ref, p_ref, *, nblk):
    hp = pl.program_id(0)
    qi = pl.program_id(1)
    tq = C_BLOCK
    pw = 2 * HEAD_DIM
    qt = qt_ref[...]
    rows = lax.broadcasted_iota(jnp.int32, (pw, tq), 0)
    blk = lax.broadcasted_iota(jnp.int32, (nblk, tq), 0)
    past = blk < qi
    aug_r = lax.broadcasted_iota(jnp.int32, (_N_AUG, tq), 0)
    tqf = lax.broadcasted_iota(jnp.int32, (_N_AUG, tq), 1).astype(F32)
    kcol = lax.broadcasted_iota(jnp.int32, (_KV_STEP, pw), 1)
    tkf = lax.broadcasted_iota(jnp.int32, (_KV_STEP, pw), 0).astype(F32)
    causal = (lax.broadcasted_iota(jnp.int32, (C_BLOCK, tq), 1)
              >= lax.broadcasted_iota(jnp.int32, (C_BLOCK, tq), 0))
    zpad = jnp.zeros((pw - _N_AUG, tq), BF16)
    k0 = pl.multiple_of(qi * C_BLOCK, C_BLOCK)
    k_own = k_ref[pl.ds(k0, C_BLOCK), :]

    qcat, kpos, c2, init = [], [], [], []
    for hh in range(2):
        c = slope_ref[2 * hp + hh] * LOG2E
        qm = jnp.where((rows >= hh * HEAD_DIM) & (rows < (hh + 1) * HEAD_DIM), qt, 0.0)
        gate = jnp.where(past, _dot3(km_ref[...], qm), NEG_INF)
        g = gate
        for r in range(C_TOPK):
            thr = jnp.max(g, axis=0, keepdims=True)
            if r < C_TOPK - 1:
                g = jnp.where(g >= thr, NEG_INF, g)
        sel_ref[hh] = jnp.where(past & (gate >= thr), 1.0, 0.0)

        qh, qmid, ql = _split3(-c * tqf)
        qpos = jnp.where(aug_r == 0, qh, jnp.where(aug_r == 1, qmid, jnp.where(aug_r == 2, ql,
                         jnp.where(aug_r < 6, 1.0, 0.0))))
        qb = (qm * (LOG2E * HEAD_DIM ** -0.5)).astype(BF16)
        qcat.append(jnp.concatenate([qb, qpos.astype(BF16), zpad], axis=0))
        kh, kmid, kl = _split3(c * tkf)
        kpos.append(jnp.where(kcol < 3, 1.0, jnp.where(kcol == 3, kh, jnp.where(kcol == 4, kmid,
                              jnp.where(kcol == 5, kl, 0.0)))).astype(BF16))
        c2.append(c)

        s = _dot(jnp.concatenate([k_own, kpos[hh][:C_BLOCK]], axis=1), qcat[hh])
        s = jnp.where(causal, s, NEG_INF)
        m0 = jnp.max(s, axis=0, keepdims=True)
        p = jnp.exp2(s - m0)
        l0 = jnp.sum(p, axis=0, keepdims=True)
        acc0 = _dot(vt_ref[pl.ds(hh * HEAD_DIM, HEAD_DIM), pl.ds(k0, C_BLOCK)], p.astype(BF16))
        init.append((m0, l0, acc0))

    nsub = _KV_STEP // C_BLOCK

    nsteps = (qi + nsub - 1) // nsub
    last_step = k_ref.shape[0] // _KV_STEP - 1

    def key_start(n):
        return pl.multiple_of(jnp.clip(n, 0, last_step) * _KV_STEP, _KV_STEP)

    def scores(n, slot):
        kb = k_ref[pl.ds(key_start(n), _KV_STEP), :]
        for hh in range(2):
            s_ref[slot, hh] = _dot(jnp.concatenate([kb, kpos[hh]], axis=1), qcat[hh])

    def pv(n, hh, slot):
        return _dot(vt_ref[pl.ds(hh * HEAD_DIM, HEAD_DIM), pl.ds(key_start(n), _KV_STEP)],
                    p_ref[slot, hh])

    def step(n, slot, carry):
        a_prev, state = carry
        off = (qi * C_BLOCK - n * _KV_STEP).astype(F32)
        scores(n + 1, 1 - slot)
        new_a, new_state = [], []
        for hh in range(2):
            m_i, l_i, acc = state[hh]
            acc = a_prev[hh] * acc + pv(n - 1, hh, 1 - slot)
            shift = c2[hh] * off
            sel = [sel_ref[hh, pl.ds(nsub * n + j, 1), :] > 0.5 for j in range(nsub)]
            m_new = m_i
            for j in range(nsub):
                mj = jnp.max(s_ref[slot, hh, j * C_BLOCK:(j + 1) * C_BLOCK, :], axis=0, keepdims=True)
                m_new = jnp.maximum(m_new, jnp.where(sel[j], mj - shift, NEG_INF))
            l_new = jnp.exp2(m_i - m_new) * l_i
            for j in range(nsub):
                rows = slice(j * C_BLOCK, (j + 1) * C_BLOCK)
                p = jnp.exp2(s_ref[slot, hh, rows, :] - jnp.where(sel[j], m_new + shift, BIG))
                p_ref[slot, hh, rows, :] = p.astype(BF16)
                l_new = l_new + jnp.sum(p, axis=0, keepdims=True)
            new_a.append(jnp.exp2(m_i - m_new))
            new_state.append((m_new, l_new, acc))
        return tuple(new_a), tuple(new_state)

    def body(n2, carry):
        return step(2 * n2 + 1, 1, step(2 * n2, 0, carry))

    p_ref[1] = jnp.zeros_like(p_ref[1])
    scores(0, 0)
    one_a = jnp.ones((1, tq), F32)
    npairs = (nsteps + 1) // 2
    a_prev, state = lax.fori_loop(0, npairs, body, ((one_a, one_a), tuple(init)))
    outs = []
    for hh in range(2):
        _, l, acc = state[hh]
        outs.append((a_prev[hh] * acc + pv(2 * npairs - 1, hh, 1)) / l)
    o_ref[...] = jnp.concatenate(outs, axis=0).T.astype(o_ref.dtype)


def _moba_attention(qt, k, vt, kmean):
    d, t = qt.shape
    nblk = t // C_BLOCK
    slopes = jnp.asarray([_alibi_slope(h, N_HEADS) for h in range(N_HEADS)], F32)
    pw = 2 * HEAD_DIM
    assert t % (2 * _KV_STEP) == 0
    return pl.pallas_call(
        functools.partial(_moba_kernel, nblk=nblk),
        grid=(d // pw, nblk),
        in_specs=[pl.BlockSpec(memory_space=pltpu.SMEM),
                  pl.BlockSpec((pw, C_BLOCK), lambda hp, i: (hp, i)),
                  pl.BlockSpec((t, pw), lambda hp, i: (0, hp)),
                  pl.BlockSpec((pw, t), lambda hp, i: (hp, 0)),
                  pl.BlockSpec((nblk, pw), lambda hp, i: (0, hp))],
        out_specs=pl.BlockSpec((C_BLOCK, pw), lambda hp, i: (i, hp)),
        out_shape=jax.ShapeDtypeStruct((t, d), BF16),
        scratch_shapes=[pltpu.VMEM((2, nblk, C_BLOCK), F32), pltpu.VMEM((2, 2, _KV_STEP, C_BLOCK), F32),
                        pltpu.VMEM((2, 2, _KV_STEP, C_BLOCK), BF16)],
        compiler_params=_params("parallel", "arbitrary"),
        name="moba_attention",
    )(slopes, qt, k, vt, kmean)


def _layer_norm(z, g, b):
    mu = jnp.mean(z, axis=-1, keepdims=True)
    zc = z - mu
    var = jnp.mean(zc * zc, axis=-1, keepdims=True)
    return zc * lax.rsqrt(var + LN_EPS) * g + b


def _outproj_ln_kernel(*refs, n_groups, alpha):
    o_refs = refs[:n_groups]
    lse_refs = refs[n_groups:2 * n_groups] if n_groups > 1 else ()
    pos = n_groups + len(lse_refs)
    wo_ref, x_ref, g_ref, b_ref, xn_ref, xb_ref = refs[pos:pos + 6]
    if n_groups == 1:
        o = o_refs[0][...]
    else:
        lses = [r[...] for r in lse_refs]
        m = functools.reduce(jnp.maximum, lses)
        es = [jnp.exp(l - m) for l in lses]
        den = functools.reduce(jnp.add, es)
        o = functools.reduce(jnp.add, [(e / den) * r[...].astype(F32) for e, r in zip(es, o_refs)])
        o = o.astype(BF16)
    y = _dot(o, wo_ref[...])
    xn = _layer_norm(alpha * x_ref[...] + y, g_ref[...], b_ref[...])
    xn_ref[...] = xn
    xb_ref[...] = xn.astype(BF16)


def _outproj_ln(os_, lses, wo, x, g, b, alpha, tm=512):
    t, d = x.shape
    n = len(os_)
    tile = pl.BlockSpec((tm, d), lambda i: (i, 0))
    return pl.pallas_call(
        functools.partial(_outproj_ln_kernel, n_groups=n, alpha=alpha),
        grid=(t // tm,),
        in_specs=[tile] * (n + len(lses)) + [_resident((d, d)), tile, _resident((1, d)), _resident((1, d))],
        out_specs=[tile, tile],
        out_shape=[jax.ShapeDtypeStruct((t, d), F32), jax.ShapeDtypeStruct((t, d), BF16)],
        compiler_params=_params("parallel"),
        name="outproj_ln",
    )(*os_, *lses, wo, x, g.reshape(1, d), b.reshape(1, d))


def _top_vals(s, k):
    out = []
    for r in range(k):
        m = jnp.max(s, axis=0, keepdims=True)
        out.append(m)
        if r < k - 1:
            s = jnp.where(s >= m, NEG_INF, s)
    return out


_CAND = [(i, j) for i in range(PEER_TOPK + 1) for j in range(PEER_TOPK + 1)
         if (i + 1) * (j + 1) <= PEER_TOPK + 1]
_NCAND = -(-len(_CAND) // 8) * 8


def _router_kernel(x_ref, wh_ref, wl_ref, sk_ref, s2_ref, e2_ref, a_ref, w_ref, q_ref, cand_ref):
    xh, xl = _split(x_ref[...])
    q_ref[...] = _dot(xh, wh_ref[...]) + (_dot(xh, wl_ref[...]) + _dot(xl, wh_ref[...]))
    tt = x_ref.shape[0]
    cand_ref[...] = jnp.full((_NCAND, tt), NEG_INF, F32)
    nk = PEER_NKEYS
    for h in range(PEER_HEADS):
        s1 = _dot3_nt(sk_ref[0], q_ref[:, (2 * h) * nk:(2 * h + 1) * nk])
        s2 = _dot3_nt(sk_ref[1], q_ref[:, (2 * h + 1) * nk:(2 * h + 2) * nk])
        a = _top_vals(s1, PEER_TOPK + 1)
        b = _top_vals(s2, PEER_TOPK + 1)
        for r, (i, j) in enumerate(_CAND):
            cand_ref[r:r + 1, :] = a[i] + b[j]
        v = _top_vals(cand_ref[...], PEER_TOPK + 1)
        thr = 0.5 * (v[PEER_TOPK - 1] + v[PEER_TOPK])
        z = functools.reduce(jnp.add, [jnp.exp(v[r] - v[0]) for r in range(PEER_TOPK)])
        s2_ref[h] = s2
        e2_ref[h] = jnp.exp(s2 - b[0])
        a_ref[h] = thr - s1
        w_ref[h] = jnp.exp(s1 - a[0]) / z


def _router(xn, wq_hi, wq_lo, subkeys, tt=256):
    t, d = xn.shape
    nq = wq_hi.shape[1]
    out = jax.ShapeDtypeStruct((PEER_HEADS, PEER_NKEYS, t), F32)
    ospec = pl.BlockSpec((PEER_HEADS, PEER_NKEYS, tt), lambda i: (0, 0, i))
    return pl.pallas_call(
        _router_kernel,
        grid=(t // tt,),
        in_specs=[pl.BlockSpec((tt, d), lambda i: (i, 0)), _resident((d, nq)), _resident((d, nq)),
                  _resident(subkeys.shape)],
        out_specs=[ospec] * 4,
        out_shape=[out] * 4,
        scratch_shapes=[pltpu.VMEM((tt, nq), F32), pltpu.VMEM((_NCAND, tt), F32)],
        compiler_params=_params("parallel"),
        name="peer_router",
    )(xn, wq_hi, wq_lo, subkeys)


def _gelu(x):
    return 0.5 * x * (1.0 + lax.erf(x * (2.0 ** -0.5)))


def _peer_kernel(x_ref, u_ref, vt_ref, s2_ref, e2_ref, a_ref, w_ref, y_ref, acc_ref, h_ref, act_ref):
    j = pl.program_id(1)
    te, tt = h_ref.shape
    nk = PEER_NKEYS

    @pl.when(j == 0)
    def _():
        acc_ref[...] = jnp.zeros_like(acc_ref)

    h_ref[...] = _dot_nt(u_ref[...], x_ref[...])
    n_i1 = te // nk
    i1_rows = pl.ds(pl.multiple_of(j * n_i1, n_i1), n_i1)
    for ii in range(n_i1):
        for c in range(tt // 128):
            cols = pl.ds(c * 128, 128)
            g = jnp.zeros((nk, 128), F32)
            for h in range(PEER_HEADS):
                a_row = a_ref[h, i1_rows, cols][ii:ii + 1]
                w_row = w_ref[h, i1_rows, cols][ii:ii + 1]
                g = g + jnp.where(s2_ref[h, :, cols] >= a_row, e2_ref[h, :, cols], 0.0) * w_row
            rows = pl.ds(ii * nk, nk)
            act_ref[rows, cols] = (_gelu(h_ref[rows, cols]) * g).astype(BF16)
    acc_ref[...] += _dot(vt_ref[...], act_ref[...])

    @pl.when(j == pl.num_programs(1) - 1)
    def _():
        y_ref[...] = acc_ref[...].T


def _peer_experts(xb, u, vt, s2, e2, a, w, tt=512, te=1024):
    t, d = xb.shape
    ne = u.shape[0]
    rspec = pl.BlockSpec((PEER_HEADS, PEER_NKEYS, tt), lambda i, j: (0, 0, i))
    return pl.pallas_call(
        _peer_kernel,
        grid=(t // tt, ne // te),
        in_specs=[pl.BlockSpec((tt, d), lambda i, j: (i, 0)),
                  pl.BlockSpec((te, d), lambda i, j: (j, 0)),
                  pl.BlockSpec((d, te), lambda i, j: (0, j)),
                  rspec, rspec, rspec, rspec],
        out_specs=pl.BlockSpec((tt, d), lambda i, j: (i, 0)),
        out_shape=jax.ShapeDtypeStruct((t, d), F32),
        scratch_shapes=[pltpu.VMEM((d, tt), F32), pltpu.VMEM((te, tt), F32), pltpu.VMEM((te, tt), BF16)],
        compiler_params=_params("parallel", "arbitrary"),
        name="peer_experts",
    )(xb, u, vt, s2, e2, a, w)


def _ln_ple_kernel(x_ref, y_ref, g_ref, b_ref, p_ref, wp_ref, wg_ref, bg_ref, o_ref, *, alpha):
    xn = _layer_norm(alpha * x_ref[...] + y_ref[...], g_ref[...], b_ref[...])
    gate = jax.nn.sigmoid(_dot(xn.astype(BF16), wg_ref[...]) + bg_ref[...])
    o_ref[...] = xn + gate * _dot(p_ref[...].astype(BF16), wp_ref[...])


def _ln_ple(x, y, g, b, p, wp, wg, bg, alpha, tm=512):
    t, d = x.shape
    dp = p.shape[1]
    tile = pl.BlockSpec((tm, d), lambda i: (i, 0))
    vec = _resident((1, d))
    return pl.pallas_call(
        functools.partial(_ln_ple_kernel, alpha=alpha),
        grid=(t // tm,),
        in_specs=[tile, tile, vec, vec, pl.BlockSpec((tm, dp), lambda i: (i, 0)), _resident((dp, d)),
                  _resident((d, d)), vec],
        out_specs=tile,
        out_shape=jax.ShapeDtypeStruct((t, d), F32),
        compiler_params=_params("parallel"),
        name="ln_ple",
    )(x, y, g.reshape(1, d), b.reshape(1, d), p, wp, wg, bg.reshape(1, d))


def _mixer_a(x, w_qkv, sinks):
    d = N_HEADS * HEAD_DIM
    qkv = _proj(x, w_qkv.astype(BF16))
    kvw = A_KV_HEADS * HEAD_DIM
    o, = _band_attention(qkv, q_blk=0, k_blk=d // kvw, v_blk=d // kvw + 1, kv_width=kvw,
                         kv_group=N_HEADS // A_KV_HEADS, dist_scale=1.0, max_dist=A_WINDOW - 1,
                         sinks=sinks)
    return [o], []


def _mixer_b(x, w_qkv):
    d = N_HEADS * HEAD_DIM
    ng = len(B_PATTERNS)
    wk = w_qkv[:, ng * d:(ng + 1) * d]
    wv = w_qkv[:, (ng + 1) * d:]
    os_, lses = [], []
    for gi, (w, r) in enumerate(B_PATTERNS):
        wg = jnp.concatenate([w_qkv[:, gi * d:(gi + 1) * d], wk, wv], axis=1).astype(BF16)
        qkv = _proj(x, wg, dil=r)
        o, lse = _band_attention(qkv, q_blk=0, k_blk=1, v_blk=2, kv_width=d, kv_group=1,
                                 dist_scale=float(r), max_dist=w // r, want_lse=True)
        os_.append(o)
        lses.append(lse)
    return os_, lses


def _mixer_c(x, w_qkv):
    d = N_HEADS * HEAD_DIM
    wq, wk, wv = w_qkv[:, :d], w_qkv[:, d:2 * d], w_qkv[:, 2 * d:]
    qt, k, vt, kmean = _proj_c(x, wq.T.astype(BF16), wk.astype(BF16), wv.T.astype(BF16))
    return [_moba_attention(qt, k, vt, kmean.reshape(-1, d))], []


def kernel(x, p, a_w_qkv, a_sinks, a_w_o, b_w_qkv, b_w_o, c_w_qkv, c_w_o, ln1_g, ln1_b, ln2_g, ln2_b,
           peer_w_q, peer_subkeys, peer_u, peer_v, ple_w, ple_gate_w, ple_gate_b):
    depth = p.shape[0]
    alpha = (2 * depth) ** 0.25
    bsz, seq, d = x.shape
    assert bsz == 1 and seq % (B_PATTERNS[-1][1] * BAND) == 0
    xt = x.reshape(seq, d)
    for i in range(depth):
        kind, j = i % 3, i // 3
        if kind == 0:
            os_, lses = _mixer_a(xt, a_w_qkv[j], a_sinks[j])
            wo = a_w_o[j]
        elif kind == 1:
            os_, lses = _mixer_b(xt, b_w_qkv[j])
            wo = b_w_o[j]
        else:
            os_, lses = _mixer_c(xt, c_w_qkv[j])
            wo = c_w_o[j]
        x1, x1b = _outproj_ln(os_, lses, wo.astype(BF16), xt, ln1_g[i], ln1_b[i], alpha)
        wq_hi, wq_lo = _split(peer_w_q[i])
        s2, e2, a, w = _router(x1, wq_hi, wq_lo, peer_subkeys[i])
        y = _peer_experts(x1b, peer_u[i].astype(BF16), peer_v[i].T.astype(BF16), s2, e2, a, w)
        xt = _ln_ple(x1, y, ln2_g[i], ln2_b[i], p[i].reshape(seq, -1), ple_w[i].astype(BF16),
                     ple_gate_w[i].astype(BF16), ple_gate_b[i], alpha)
    return xt.reshape(bsz, seq, d)
```

```python
import functools

import jax
import jax.numpy as jnp
from jax import lax
from jax.experimental import pallas as pl
from jax.experimental.pallas import tpu as pltpu

F32 = jnp.float32
BF16 = jnp.bfloat16
NEG_INF = float("-inf")
BIG = 1e30
LOG2E = 1.4426950408889634

D_MODEL = 1024
HEAD_DIM = 64
N_HEADS = 16
BAND = 128
A_KV_HEADS = 2
A_WINDOW = 128
B_PATTERNS = ((128, 1), (512, 4), (2048, 16))
C_BLOCK = 256
C_TOPK = 3
PEER_HEADS = 8
PEER_NKEYS = 128
PEER_TOPK = 16
LN_EPS = 1e-5
VMEM_LIMIT = 56 * 1024 * 1024


def _params(*sem):
    return pltpu.CompilerParams(dimension_semantics=sem, vmem_limit_bytes=VMEM_LIMIT)


def _dot(a, b):
    return jnp.dot(a, b, preferred_element_type=F32)


def _dot_nt(a, b):
    return lax.dot_general(a, b, (((1,), (1,)), ((), ())), preferred_element_type=F32)


def _split(a):
    hi = a.astype(BF16)
    lo = (a - hi.astype(F32)).astype(BF16)
    return hi, lo


def _alibi_slope(h, n):
    return 2.0 ** (-8.0 * (h + 1) / n)


def _resident(shape):
    zeros = (0,) * len(shape)
    return pl.BlockSpec(shape, lambda *_: zeros)


def _proj_kernel(x_ref, w_ref, o_ref):
    o_ref[...] = _dot(x_ref[...].astype(BF16), w_ref[...]).astype(o_ref.dtype)


def _proj(x, w, dil=1, tm=512):
    t, k = x.shape
    n = w.shape[1]
    l = t // dil
    tm = min(tm, l)
    return pl.pallas_call(
        _proj_kernel,
        grid=(dil, l // tm),
        in_specs=[pl.BlockSpec((tm, k), lambda c, i: (i, c)), _resident((k, n))],
        out_specs=pl.BlockSpec((None, tm, n), lambda c, i: (c, i, 0)),
        out_shape=jax.ShapeDtypeStruct((dil, l, n), BF16),
        compiler_params=_params("parallel", "parallel"),
        name="proj",
    )(x.reshape(l, dil * k), w)


def _proj_c_kernel(x_ref, wqt_ref, wk_ref, wvt_ref, qt_ref, k_ref, vt_ref, km_ref, *, nblk):
    xb = x_ref[...].astype(BF16)
    qt_ref[...] = _dot_nt(wqt_ref[...], xb)
    kf = _dot(xb, wk_ref[...])
    k_ref[...] = kf.astype(BF16)
    vt_ref[...] = _dot_nt(wvt_ref[...], xb).astype(BF16)
    for r in range(nblk):
        km_ref[r] = jnp.mean(kf[r * C_BLOCK:(r + 1) * C_BLOCK], axis=0, keepdims=True)


def _proj_c(x, wqt, wk, wvt, tm=512):
    t, k = x.shape
    d = wk.shape[1]
    nblk = tm // C_BLOCK
    return pl.pallas_call(
        functools.partial(_proj_c_kernel, nblk=nblk),
        grid=(t // tm,),
        in_specs=[pl.BlockSpec((tm, k), lambda i: (i, 0)), _resident((d, k)), _resident((k, d)),
                  _resident((d, k))],
        out_specs=[pl.BlockSpec((d, tm), lambda i: (0, i)), pl.BlockSpec((tm, d), lambda i: (i, 0)),
                   pl.BlockSpec((d, tm), lambda i: (0, i)), pl.BlockSpec((nblk, 1, d), lambda i: (i, 0, 0))],
        out_shape=[jax.ShapeDtypeStruct((d, t), F32), jax.ShapeDtypeStruct((t, d), BF16),
                   jax.ShapeDtypeStruct((d, t), BF16), jax.ShapeDtypeStruct((t // C_BLOCK, 1, d), F32)],
        compiler_params=_params("parallel"),
        name="proj_moba",
    )(x, wqt, wk, wvt)


def _band_kernel(*refs, kv_group, dist_scale, max_dist, use_sinks, want_lse):
    q_ref, kp_ref, ko_ref, vp_ref, vo_ref = refs[:5]
    pos = 5
    sink_ref = None
    if use_sinks:
        sink_ref = refs[pos]
        pos += 1
    o_ref = refs[pos]
    lse_ref = refs[pos + 1] if want_lse else None

    b = pl.program_id(1)
    qi = lax.broadcasted_iota(jnp.int32, (BAND, 2 * BAND), 0)
    ki = lax.broadcasted_iota(jnp.int32, (BAND, 2 * BAND), 1)
    dist = qi + BAND - ki
    valid = (dist >= 0) & (dist <= max_dist) & (ki >= jnp.where(b > 0, 0, BAND))
    distf = dist.astype(F32)
    k_all = jnp.concatenate([kp_ref[...], ko_ref[...]], axis=0)
    v_all = jnp.concatenate([vp_ref[...], vo_ref[...]], axis=0)
    outs, lses = [], []
    for h in range(N_HEADS):
        g = h // kv_group
        qh = q_ref[:, h * HEAD_DIM:(h + 1) * HEAD_DIM]
        kh = k_all[:, g * HEAD_DIM:(g + 1) * HEAD_DIM]
        vh = v_all[:, g * HEAD_DIM:(g + 1) * HEAD_DIM]
        s = _dot_nt(qh, kh) * (HEAD_DIM ** -0.5)
        s = jnp.where(valid, s - (_alibi_slope(h, N_HEADS) * dist_scale) * distf, NEG_INF)
        m = jnp.max(s, axis=-1, keepdims=True)
        if use_sinks:
            sk = sink_ref[:, h:h + 1]
            m = jnp.maximum(m, sk)
        e = jnp.exp(s - m)
        den = jnp.sum(e, axis=-1, keepdims=True)
        if use_sinks:
            den = den + jnp.exp(sk - m)
        p = (e / den).astype(BF16)
        outs.append(_dot(p, vh))
        if want_lse:
            lses.append(jnp.broadcast_to(m + jnp.log(den), (BAND, HEAD_DIM)))
    o_ref[...] = jnp.concatenate(outs, axis=1).astype(o_ref.dtype)
    if want_lse:
        lse_ref[...] = jnp.concatenate(lses, axis=1)


def _band_attention(qkv, *, q_blk, k_blk, v_blk, kv_width, kv_group, dist_scale, max_dist,
                    sinks=None, want_lse=False):
    r, l, _ = qkv.shape
    d = N_HEADS * HEAD_DIM
    prev = lambda blk: (lambda c, b: (c, jnp.maximum(b - 1, 0), blk))
    own = lambda blk: (lambda c, b: (c, b, blk))
    in_specs = [pl.BlockSpec((None, BAND, d), own(q_blk)),
                pl.BlockSpec((None, BAND, kv_width), prev(k_blk)),
                pl.BlockSpec((None, BAND, kv_width), own(k_blk)),
                pl.BlockSpec((None, BAND, kv_width), prev(v_blk)),
                pl.BlockSpec((None, BAND, kv_width), own(v_blk))]
    args = [qkv] * 5
    if sinks is not None:
        in_specs.append(_resident((1, N_HEADS)))
        args.append(sinks.reshape(1, N_HEADS).astype(F32))
    o_spec = pl.BlockSpec((BAND, d), lambda c, b: (b, c))
    out_specs = [o_spec]
    out_shape = [jax.ShapeDtypeStruct((l, r * d), BF16)]
    if want_lse:
        out_specs.append(o_spec)
        out_shape.append(jax.ShapeDtypeStruct((l, r * d), F32))
    outs = pl.pallas_call(
        functools.partial(_band_kernel, kv_group=kv_group, dist_scale=dist_scale, max_dist=max_dist,
                          use_sinks=sinks is not None, want_lse=want_lse),
        grid=(r, l // BAND),
        in_specs=in_specs, out_specs=out_specs, out_shape=out_shape,
        compiler_params=_params("parallel", "parallel"),
        name="band_attention",
    )(*args)
    return [o.reshape(l * r, d) for o in outs]


def _dot3(a, b):
    ah, al = _split(a)
    bh, bl = _split(b)
    return _dot(ah, bh) + (_dot(ah, bl) + _dot(al, bh))


def _dot3_nt(a, b):
    ah, al = _split(a)
    bh, bl = _split(b)
    return _dot_nt(ah, bh) + (_dot_nt(ah, bl) + _dot_nt(al, bh))


def _split3(a):
    h = a.astype(BF16).astype(F32)
    r = a - h
    m = r.astype(BF16).astype(F32)
    l = (r - m).astype(BF16).astype(F32)
    return h, m, l


_KV_STEP = 2 * C_BLOCK
_N_AUG = 16


def _moba_kernel(slope_ref, qt_ref, k_ref, vt_ref, km_ref, o_ref, sel_ref, s_ref, p_ref, *, nblk):
    hp = pl.program_id(0)
    qi = pl.program_id(1)
    tq = C_BLOCK
    pw = 2 * HEAD_DIM
    qt = qt_ref[...]
    rows = lax.broadcasted_iota(jnp.int32, (pw, tq), 0)
    blk = lax.broadcasted_iota(jnp.int32, (nblk, tq), 0)
    past = blk < qi
    aug_r = lax.broadcasted_iota(jnp.int32, (_N_AUG, tq), 0)
    tqf = lax.broadcasted_iota(jnp.int32, (_N_AUG, tq), 1).astype(F32)
    kcol = lax.broadcasted_iota(jnp.int32, (_KV_STEP, pw), 1)
    tkf = lax.broadcasted_iota(jnp.int32, (_KV_STEP, pw), 0).astype(F32)
    causal = (lax.broadcasted_iota(jnp.int32, (C_BLOCK, tq), 1)
              >= lax.broadcasted_iota(jnp.int32, (C_BLOCK, tq), 0))
    zpad = jnp.zeros((pw - _N_AUG, tq), BF16)
    k0 = pl.multiple_of(qi * C_BLOCK, C_BLOCK)
    k_own = k_ref[pl.ds(k0, C_BLOCK), :]

    qcat, kpos, c2, init = [], [], [], []
    for hh in range(2):
        c = slope_ref[2 * hp + hh] * LOG2E
        qm = jnp.where((rows >= hh * HEAD_DIM) & (rows < (hh + 1) * HEAD_DIM), qt, 0.0)
        gate = jnp.where(past, _dot3(km_ref[...], qm), NEG_INF)
        g = gate
        for r in range(C_TOPK):
            thr = jnp.max(g, axis=0, keepdims=True)
            if r < C_TOPK - 1:
                g = jnp.where(g >= thr, NEG_INF, g)
        sel_ref[hh] = jnp.where(past & (gate >= thr), 1.0, 0.0)

        qh, qmid, ql = _split3(-c * tqf)
        qpos = jnp.where(aug_r == 0, qh, jnp.where(aug_r == 1, qmid, jnp.where(aug_r == 2, ql,
                         jnp.where(aug_r < 6, 1.0, 0.0))))
        qb = (qm * (LOG2E * HEAD_DIM ** -0.5)).astype(BF16)
        qcat.append(jnp.concatenate([qb, qpos.astype(BF16), zpad], axis=0))
        kh, kmid, kl = _split3(c * tkf)
        kpos.append(jnp.where(kcol < 3, 1.0, jnp.where(kcol == 3, kh, jnp.where(kcol == 4, kmid,
                              jnp.where(kcol == 5, kl, 0.0)))).astype(BF16))
        c2.append(c)

        s = _dot(jnp.concatenate([k_own, kpos[hh][:C_BLOCK]], axis=1), qcat[hh])
        s = jnp.where(causal, s, NEG_INF)
        m0 = jnp.max(s, axis=0, keepdims=True)
        p = jnp.exp2(s - m0)
        l0 = jnp.sum(p, axis=0, keepdims=True)
        acc0 = _dot(vt_ref[pl.ds(hh * HEAD_DIM, HEAD_DIM), pl.ds(k0, C_BLOCK)], p.astype(BF16))
        init.append((m0, l0, acc0))

    nsub = _KV_STEP // C_BLOCK

    nsteps = (qi + nsub - 1) // nsub
    last_step = k_ref.shape[0] // _KV_STEP - 1

    def key_start(n):
        return pl.multiple_of(jnp.clip(n, 0, last_step) * _KV_STEP, _KV_STEP)

    def scores(n, slot):
        kb = k_ref[pl.ds(key_start(n), _KV_STEP), :]
        for hh in range(2):
            s_ref[slot, hh] = _dot(jnp.concatenate([kb, kpos[hh]], axis=1), qcat[hh])

    def pv(n, hh, slot):
        return _dot(vt_ref[pl.ds(hh * HEAD_DIM, HEAD_DIM), pl.ds(key_start(n), _KV_STEP)],
                    p_ref[slot, hh])

    def step(n, slot, carry):
        a_prev, state = carry
        off = (qi * C_BLOCK - n * _KV_STEP).astype(F32)
        scores(n + 1, 1 - slot)
        new_a, new_state = [], []
        for hh in range(2):
            m_i, l_i, acc = state[hh]
            acc = a_prev[hh] * acc + pv(n - 1, hh, 1 - slot)
            shift = c2[hh] * off
            sel = [sel_ref[hh, pl.ds(nsub * n + j, 1), :] > 0.5 for j in range(nsub)]
            m_new = m_i
            for j in range(nsub):
                mj = jnp.max(s_ref[slot, hh, j * C_BLOCK:(j + 1) * C_BLOCK, :], axis=0, keepdims=True)
                m_new = jnp.maximum(m_new, jnp.where(sel[j], mj - shift, NEG_INF))
            l_new = jnp.exp2(m_i - m_new) * l_i
            for j in range(nsub):
                rows = slice(j * C_BLOCK, (j + 1) * C_BLOCK)
                p = jnp.exp2(s_ref[slot, hh, rows, :] - jnp.where(sel[j], m_new + shift, BIG))
                p_ref[slot, hh, rows, :] = p.astype(BF16)
                l_new = l_new + jnp.sum(p, axis=0, keepdims=True)
            new_a.append(jnp.exp2(m_i - m_new))
            new_state.append((m_new, l_new, acc))
        return tuple(new_a), tuple(new_state)

    def body(n2, carry):
        return step(2 * n2 + 1, 1, step(2 * n2, 0, carry))

    p_ref[1] = jnp.zeros_like(p_ref[1])
    scores(0, 0)
    one_a = jnp.ones((1, tq), F32)
    npairs = (nsteps + 1) // 2
    a_prev, state = lax.fori_loop(0, npairs, body, ((one_a, one_a), tuple(init)))
    outs = []
    for hh in range(2):
        _, l, acc = state[hh]
        outs.append((a_prev[hh] * acc + pv(2 * npairs - 1, hh, 1)) / l)
    o_ref[...] = jnp.concatenate(outs, axis=0).T.astype(o_ref.dtype)


def _moba_attention(qt, k, vt, kmean):
    d, t = qt.shape
    nblk = t // C_BLOCK
    slopes = jnp.asarray([_alibi_slope(h, N_HEADS) for h in range(N_HEADS)], F32)
    pw = 2 * HEAD_DIM
    assert t % (2 * _KV_STEP) == 0
    return pl.pallas_call(
        functools.partial(_moba_kernel, nblk=nblk),
        grid=(d // pw, nblk),
        in_specs=[pl.BlockSpec(memory_space=pltpu.SMEM),
                  pl.BlockSpec((pw, C_BLOCK), lambda hp, i: (hp, i)),
                  pl.BlockSpec((t, pw), lambda hp, i: (0, hp)),
                  pl.BlockSpec((pw, t), lambda hp, i: (hp, 0)),
                  pl.BlockSpec((nblk, pw), lambda hp, i: (0, hp))],
        out_specs=pl.BlockSpec((C_BLOCK, pw), lambda hp, i: (i, hp)),
        out_shape=jax.ShapeDtypeStruct((t, d), BF16),
        scratch_shapes=[pltpu.VMEM((2, nblk, C_BLOCK), F32), pltpu.VMEM((2, 2, _KV_STEP, C_BLOCK), F32),
                        pltpu.VMEM((2, 2, _KV_STEP, C_BLOCK), BF16)],
        compiler_params=_params("parallel", "arbitrary"),
        name="moba_attention",
    )(slopes, qt, k, vt, kmean)


def _layer_norm(z, g, b):
    mu = jnp.mean(z, axis=-1, keepdims=True)
    zc = z - mu
    var = jnp.mean(zc * zc, axis=-1, keepdims=True)
    return zc * lax.rsqrt(var + LN_EPS) * g + b


def _outproj_ln_kernel(*refs, n_groups, alpha):
    o_refs = refs[:n_groups]
    lse_refs = refs[n_groups:2 * n_groups] if n_groups > 1 else ()
    pos = n_groups + len(lse_refs)
    wo_ref, x_ref, g_ref, b_ref, xn_ref, xb_ref = refs[pos:pos + 6]
    if n_groups == 1:
        o = o_refs[0][...]
    else:
        lses = [r[...] for r in lse_refs]
        m = functools.reduce(jnp.maximum, lses)
        es = [jnp.exp(l - m) for l in lses]
        den = functools.reduce(jnp.add, es)
        o = functools.reduce(jnp.add, [(e / den) * r[...].astype(F32) for e, r in zip(es, o_refs)])
        o = o.astype(BF16)
    y = _dot(o, wo_ref[...])
    xn = _layer_norm(alpha * x_ref[...] + y, g_ref[...], b_ref[...])
    xn_ref[...] = xn
    xb_ref[...] = _pack_rows(xn)


def _outproj_ln(os_, lses, wo, x, g, b, alpha, tm=512):
    t, d = x.shape
    n = len(os_)
    tile = pl.BlockSpec((tm, d), lambda i: (i, 0))
    return pl.pallas_call(
        functools.partial(_outproj_ln_kernel, n_groups=n, alpha=alpha),
        grid=(t // tm,),
        in_specs=[tile] * (n + len(lses)) + [_resident((d, d)), tile, _resident((1, d)), _resident((1, d))],
        out_specs=[tile, pl.BlockSpec((tm // 2, d), lambda i: (i, 0))],
        out_shape=[jax.ShapeDtypeStruct((t, d), F32), jax.ShapeDtypeStruct((t // 2, d), jnp.uint32)],
        compiler_params=_params("parallel"),
        name="outproj_ln",
    )(*os_, *lses, wo, x, g.reshape(1, d), b.reshape(1, d))


def _pack_rows(a):
    return pltpu.bitcast(a.astype(BF16), jnp.uint32)


def _pack_dup(a):
    bits = pltpu.bitcast(a.astype(BF16).astype(F32), jnp.uint32)
    return bits | (bits >> 16)


def _pack_rows_xla(a):
    m2, n = a.shape
    pairs = jnp.swapaxes(a.astype(BF16).reshape(m2 // 2, 2, n), 1, 2)
    return lax.bitcast_convert_type(pairs, jnp.uint32)


def _top_vals(s, k):
    out = []
    for r in range(k):
        m = jnp.max(s, axis=0, keepdims=True)
        out.append(m)
        if r < k - 1:
            s = jnp.where(s >= m, NEG_INF, s)
    return out


_CAND = [(i, j) for i in range(PEER_TOPK + 1) for j in range(PEER_TOPK + 1)
         if (i + 1) * (j + 1) <= PEER_TOPK + 1]
_NCAND = -(-len(_CAND) // 8) * 8


def _router_kernel(x_ref, wh_ref, wl_ref, sk_ref, r2_ref, e2_ref, n1_ref, w_ref, q_ref, cand_ref):
    xh, xl = _split(x_ref[...])
    q_ref[...] = _dot(xh, wh_ref[...]) + (_dot(xh, wl_ref[...]) + _dot(xl, wh_ref[...]))
    tt = x_ref.shape[0]
    cand_ref[...] = jnp.full((_NCAND, tt), NEG_INF, F32)
    nk = PEER_NKEYS
    for h in range(PEER_HEADS):
        s1 = _dot3_nt(sk_ref[0], q_ref[:, (2 * h) * nk:(2 * h + 1) * nk])
        s2 = _dot3_nt(sk_ref[1], q_ref[:, (2 * h + 1) * nk:(2 * h + 2) * nk])
        a = _top_vals(s1, PEER_TOPK + 1)
        b = _top_vals(s2, PEER_TOPK + 1)
        for r, (i, j) in enumerate(_CAND):
            cand_ref[r:r + 1, :] = a[i] + b[j]
        v = _top_vals(cand_ref[...], PEER_TOPK + 1)
        thr = 0.5 * (v[PEER_TOPK - 1] + v[PEER_TOPK])
        z = functools.reduce(jnp.add, [jnp.exp(v[r] - v[0]) for r in range(PEER_TOPK)])
        cut = thr - s1
        rank2 = jnp.zeros_like(s2)
        count1 = jnp.zeros_like(s1)
        for r in range(PEER_TOPK):
            rank2 = rank2 + jnp.where(s2 < b[r], 1.0, 0.0)
            count1 = count1 + jnp.where(b[r] >= cut, 1.0, 0.0)
        r2_ref[h] = _pack_rows(rank2)
        e2_ref[h] = _pack_rows(jnp.exp(s2 - b[0]))
        n1_ref[h] = _pack_dup(count1)
        w_ref[h] = _pack_dup(jnp.exp(s1 - a[0]) / z)


def _router(xn, wq_hi, wq_lo, subkeys, tt=256):
    t, d = xn.shape
    nq = wq_hi.shape[1]
    shape = (PEER_HEADS, PEER_NKEYS, t)
    ospec = pl.BlockSpec((PEER_HEADS, PEER_NKEYS, tt), lambda i: (0, 0, i))
    pspec = pl.BlockSpec((PEER_HEADS, PEER_NKEYS // 2, tt), lambda i: (0, 0, i))
    packed = jax.ShapeDtypeStruct((PEER_HEADS, PEER_NKEYS // 2, t), jnp.uint32)
    return pl.pallas_call(
        _router_kernel,
        grid=(t // tt,),
        in_specs=[pl.BlockSpec((tt, d), lambda i: (i, 0)), _resident((d, nq)), _resident((d, nq)),
                  _resident(subkeys.shape)],
        out_specs=[pspec, pspec, ospec, ospec],
        out_shape=[packed, packed, jax.ShapeDtypeStruct(shape, jnp.uint32),
                   jax.ShapeDtypeStruct(shape, jnp.uint32)],
        scratch_shapes=[pltpu.VMEM((tt, nq), F32), pltpu.VMEM((_NCAND, tt), F32)],
        compiler_params=_params("parallel"),
        name="peer_router",
    )(xn, wq_hi, wq_lo, subkeys)


def _gelu(x):
    return 0.5 * x * (1.0 + lax.erf(x * (2.0 ** -0.5)))


_BF16_ROWS = 16
_PEER_SUB = 256
_SUBLANES = 8


def _peer_kernel(x_ref, u_ref, vt_ref, r2_ref, e2_ref, n1_ref, w_ref, y_ref, acc_ref, h_ref, act_ref):
    j = pl.program_id(1)
    te, tt = h_ref.shape
    nk = PEER_NKEYS

    @pl.when(j == 0)
    def _():
        acc_ref[...] = jnp.zeros_like(acc_ref)

    xb = pltpu.bitcast(x_ref[...], BF16)
    for k in range(te // _PEER_SUB):
        uk = pltpu.bitcast(u_ref[pl.ds(k * _PEER_SUB // 2, _PEER_SUB // 2), :], BF16)
        h_ref[pl.ds(k * _PEER_SUB, _PEER_SUB), :] = _dot_nt(uk, xb)
    half = nk // 2
    hshape = (half // _BF16_ROWS, _BF16_ROWS, 128)
    n_grp = 8
    for grp in range(te // (n_grp * nk)):
        for c in range(tt // 128):
            cols = pl.ds(c * 128, 128)
            for part in range(2):
                krows = pl.ds(part * (half // 2), half // 2)
                g = [jnp.zeros(hshape, BF16) for _ in range(n_grp)]
                for h in range(PEER_HEADS):
                    r2 = pltpu.bitcast(r2_ref[h, krows, cols], BF16).reshape(hshape)
                    e2 = pltpu.bitcast(e2_ref[h, krows, cols], BF16).reshape(hshape)
                    for ii in range(n_grp):
                        row = pl.ds(grp * n_grp + ii, 1)
                        n_row = pltpu.bitcast(jnp.broadcast_to(n1_ref[h, row, cols], (_SUBLANES, 128)), BF16)
                        w_row = pltpu.bitcast(jnp.broadcast_to(w_ref[h, row, cols], (_SUBLANES, 128)), BF16)
                        g[ii] = g[ii] + jnp.where(r2 < n_row[None], e2, 0.0) * w_row[None]
                for ii in range(n_grp):
                    rows = pl.ds((grp * n_grp + ii) * nk + part * half, half)
                    act_ref[rows, cols] = _gelu(h_ref[rows, cols]).astype(BF16) * g[ii].reshape(half, 128)
    acc_ref[...] += _dot(pltpu.bitcast(vt_ref[...], BF16), act_ref[...])

    @pl.when(j == pl.num_programs(1) - 1)
    def _():
        y_ref[...] = acc_ref[...].T


def _peer_experts(xb, u, vt, r2, e2, n1, w, tt=512, te=2048):
    t, d = 2 * xb.shape[0], xb.shape[1]
    ne = 2 * u.shape[0]
    rspec = pl.BlockSpec((PEER_HEADS, te // PEER_NKEYS, tt), lambda i, j: (0, j, i))
    pspec = pl.BlockSpec((PEER_HEADS, PEER_NKEYS // 2, tt), lambda i, j: (0, 0, i))
    return pl.pallas_call(
        _peer_kernel,
        grid=(t // tt, ne // te),
        in_specs=[pl.BlockSpec((tt // 2, d), lambda i, j: (i, 0)),
                  pl.BlockSpec((te // 2, d), lambda i, j: (j, 0)),
                  pl.BlockSpec((d // 2, te), lambda i, j: (0, j)),
                  pspec, pspec, rspec, rspec],
        out_specs=pl.BlockSpec((tt, d), lambda i, j: (i, 0)),
        out_shape=jax.ShapeDtypeStruct((t, d), F32),
        scratch_shapes=[pltpu.VMEM((d, tt), F32), pltpu.VMEM((te, tt), F32), pltpu.VMEM((te, tt), BF16)],
        compiler_params=_params("parallel", "arbitrary"),
        name="peer_experts",
    )(xb, u, vt, r2, e2, n1, w)


def _ln_ple_kernel(x_ref, y_ref, g_ref, b_ref, p_ref, wp_ref, wg_ref, bg_ref, o_ref, *, alpha):
    xn = _layer_norm(alpha * x_ref[...] + y_ref[...], g_ref[...], b_ref[...])
    gate = jax.nn.sigmoid(_dot(xn.astype(BF16), wg_ref[...]) + bg_ref[...])
    o_ref[...] = xn + gate * _dot(p_ref[...].astype(BF16), wp_ref[...])


def _ln_ple(x, y, g, b, p, wp, wg, bg, alpha, tm=512):
    t, d = x.shape
    dp = p.shape[1]
    tile = pl.BlockSpec((tm, d), lambda i: (i, 0))
    vec = _resident((1, d))
    return pl.pallas_call(
        functools.partial(_ln_ple_kernel, alpha=alpha),
        grid=(t // tm,),
        in_specs=[tile, tile, vec, vec, pl.BlockSpec((tm, dp), lambda i: (i, 0)), _resident((dp, d)),
                  _resident((d, d)), vec],
        out_specs=tile,
        out_shape=jax.ShapeDtypeStruct((t, d), F32),
        compiler_params=_params("parallel"),
        name="ln_ple",
    )(x, y, g.reshape(1, d), b.reshape(1, d), p, wp, wg, bg.reshape(1, d))


def _mixer_a(x, w_qkv, sinks):
    d = N_HEADS * HEAD_DIM
    qkv = _proj(x, w_qkv.astype(BF16))
    kvw = A_KV_HEADS * HEAD_DIM
    o, = _band_attention(qkv, q_blk=0, k_blk=d // kvw, v_blk=d // kvw + 1, kv_width=kvw,
                         kv_group=N_HEADS // A_KV_HEADS, dist_scale=1.0, max_dist=A_WINDOW - 1,
                         sinks=sinks)
    return [o], []


def _mixer_b(x, w_qkv):
    d = N_HEADS * HEAD_DIM
    ng = len(B_PATTERNS)
    wk = w_qkv[:, ng * d:(ng + 1) * d]
    wv = w_qkv[:, (ng + 1) * d:]
    os_, lses = [], []
    for gi, (w, r) in enumerate(B_PATTERNS):
        wg = jnp.concatenate([w_qkv[:, gi * d:(gi + 1) * d], wk, wv], axis=1).astype(BF16)
        qkv = _proj(x, wg, dil=r)
        o, lse = _band_attention(qkv, q_blk=0, k_blk=1, v_blk=2, kv_width=d, kv_group=1,
                                 dist_scale=float(r), max_dist=w // r, want_lse=True)
        os_.append(o)
        lses.append(lse)
    return os_, lses


def _mixer_c(x, w_qkv):
    d = N_HEADS * HEAD_DIM
    wq, wk, wv = w_qkv[:, :d], w_qkv[:, d:2 * d], w_qkv[:, 2 * d:]
    qt, k, vt, kmean = _proj_c(x, wq.T.astype(BF16), wk.astype(BF16), wv.T.astype(BF16))
    return [_moba_attention(qt, k, vt, kmean.reshape(-1, d))], []


def kernel(x, p, a_w_qkv, a_sinks, a_w_o, b_w_qkv, b_w_o, c_w_qkv, c_w_o, ln1_g, ln1_b, ln2_g, ln2_b,
           peer_w_q, peer_subkeys, peer_u, peer_v, ple_w, ple_gate_w, ple_gate_b):
    depth = p.shape[0]
    alpha = (2 * depth) ** 0.25
    bsz, seq, d = x.shape
    assert bsz == 1 and seq % (B_PATTERNS[-1][1] * BAND) == 0
    xt = x.reshape(seq, d)
    for i in range(depth):
        kind, j = i % 3, i // 3
        if kind == 0:
            os_, lses = _mixer_a(xt, a_w_qkv[j], a_sinks[j])
            wo = a_w_o[j]
        elif kind == 1:
            os_, lses = _mixer_b(xt, b_w_qkv[j])
            wo = b_w_o[j]
        else:
            os_, lses = _mixer_c(xt, c_w_qkv[j])
            wo = c_w_o[j]
        x1, x1b = _outproj_ln(os_, lses, wo.astype(BF16), xt, ln1_g[i], ln1_b[i], alpha)
        wq_hi, wq_lo = _split(peer_w_q[i])
        s2, e2, a, w = _router(x1, wq_hi, wq_lo, peer_subkeys[i])
        y = _peer_experts(x1b, _pack_rows_xla(peer_u[i]), _pack_rows_xla(peer_v[i].T), s2, e2, a, w)
        xt = _ln_ple(x1, y, ln2_g[i], ln2_b[i], p[i].reshape(seq, -1), ple_w[i].astype(BF16),
                     ple_gate_w[i].astype(BF16), ple_gate_b[i], alpha)
    return xt.reshape(bsz, seq, d)
```

```python
import functools

import jax
import jax.numpy as jnp
from jax import lax
from jax.experimental import pallas as pl
from jax.experimental.pallas import tpu as pltpu

F32 = jnp.float32
BF16 = jnp.bfloat16
NEG_INF = float("-inf")
BIG = 1e30
LOG2E = 1.4426950408889634

D_MODEL = 1024
HEAD_DIM = 64
N_HEADS = 16
BAND = 128
A_KV_HEADS = 2
A_WINDOW = 128
B_PATTERNS = ((128, 1), (512, 4), (2048, 16))
C_BLOCK = 256
C_TOPK = 3
PEER_HEADS = 8
PEER_NKEYS = 128
PEER_TOPK = 16
LN_EPS = 1e-5
VMEM_LIMIT = 56 * 1024 * 1024


def _params(*sem):
    return pltpu.CompilerParams(dimension_semantics=sem, vmem_limit_bytes=VMEM_LIMIT)


def _dot(a, b):
    return jnp.dot(a, b, preferred_element_type=F32)


def _dot_nt(a, b):
    return lax.dot_general(a, b, (((1,), (1,)), ((), ())), preferred_element_type=F32)


def _split(a):
    hi = a.astype(BF16)
    lo = (a - hi.astype(F32)).astype(BF16)
    return hi, lo


def _alibi_slope(h, n):
    return 2.0 ** (-8.0 * (h + 1) / n)


def _resident(shape):
    zeros = (0,) * len(shape)
    return pl.BlockSpec(shape, lambda *_: zeros)


def _proj_kernel(x_ref, w_ref, o_ref):
    o_ref[...] = _dot(x_ref[...].astype(BF16), w_ref[...]).astype(o_ref.dtype)


def _proj(x, w, dil=1, tm=512):
    t, k = x.shape
    n = w.shape[1]
    l = t // dil
    tm = min(tm, l)
    return pl.pallas_call(
        _proj_kernel,
        grid=(dil, l // tm),
        in_specs=[pl.BlockSpec((tm, k), lambda c, i: (i, c)), _resident((k, n))],
        out_specs=pl.BlockSpec((None, tm, n), lambda c, i: (c, i, 0)),
        out_shape=jax.ShapeDtypeStruct((dil, l, n), BF16),
        compiler_params=_params("parallel", "parallel"),
        name="proj",
    )(x.reshape(l, dil * k), w)


def _proj_c_kernel(x_ref, wqt_ref, wk_ref, wvt_ref, qt_ref, k_ref, vt_ref, km_ref, *, nblk):
    xb = x_ref[...].astype(BF16)
    qt_ref[...] = _dot_nt(wqt_ref[...], xb)
    kf = _dot(xb, wk_ref[...])
    k_ref[...] = kf.astype(BF16)
    vt_ref[...] = _dot_nt(wvt_ref[...], xb).astype(BF16)
    for r in range(nblk):
        km_ref[r] = jnp.mean(kf[r * C_BLOCK:(r + 1) * C_BLOCK], axis=0, keepdims=True)


def _proj_c(x, wqt, wk, wvt, tm=512):
    t, k = x.shape
    d = wk.shape[1]
    nblk = tm // C_BLOCK
    return pl.pallas_call(
        functools.partial(_proj_c_kernel, nblk=nblk),
        grid=(t // tm,),
        in_specs=[pl.BlockSpec((tm, k), lambda i: (i, 0)), _resident((d, k)), _resident((k, d)),
                  _resident((d, k))],
        out_specs=[pl.BlockSpec((d, tm), lambda i: (0, i)), pl.BlockSpec((tm, d), lambda i: (i, 0)),
                   pl.BlockSpec((d, tm), lambda i: (0, i)), pl.BlockSpec((nblk, 1, d), lambda i: (i, 0, 0))],
        out_shape=[jax.ShapeDtypeStruct((d, t), F32), jax.ShapeDtypeStruct((t, d), BF16),
                   jax.ShapeDtypeStruct((d, t), BF16), jax.ShapeDtypeStruct((t // C_BLOCK, 1, d), F32)],
        compiler_params=_params("parallel"),
        name="proj_moba",
    )(x, wqt, wk, wvt)


def _band_kernel(*refs, kv_group, dist_scale, max_dist, use_sinks, want_lse):
    q_ref, kp_ref, ko_ref, vp_ref, vo_ref = refs[:5]
    pos = 5
    sink_ref = None
    if use_sinks:
        sink_ref = refs[pos]
        pos += 1
    o_ref = refs[pos]
    lse_ref = refs[pos + 1] if want_lse else None

    b = pl.program_id(1)
    qi = lax.broadcasted_iota(jnp.int32, (BAND, 2 * BAND), 0)
    ki = lax.broadcasted_iota(jnp.int32, (BAND, 2 * BAND), 1)
    dist = qi + BAND - ki
    valid = (dist >= 0) & (dist <= max_dist) & (ki >= jnp.where(b > 0, 0, BAND))
    distf = dist.astype(F32)
    k_all = jnp.concatenate([kp_ref[...], ko_ref[...]], axis=0)
    v_all = jnp.concatenate([vp_ref[...], vo_ref[...]], axis=0)
    outs, lses = [], []
    for h in range(N_HEADS):
        g = h // kv_group
        qh = q_ref[:, h * HEAD_DIM:(h + 1) * HEAD_DIM]
        kh = k_all[:, g * HEAD_DIM:(g + 1) * HEAD_DIM]
        vh = v_all[:, g * HEAD_DIM:(g + 1) * HEAD_DIM]
        s = _dot_nt(qh, kh) * (HEAD_DIM ** -0.5)
        s = jnp.where(valid, s - (_alibi_slope(h, N_HEADS) * dist_scale) * distf, NEG_INF)
        m = jnp.max(s, axis=-1, keepdims=True)
        if use_sinks:
            sk = sink_ref[:, h:h + 1]
            m = jnp.maximum(m, sk)
        e = jnp.exp(s - m)
        den = jnp.sum(e, axis=-1, keepdims=True)
        if use_sinks:
            den = den + jnp.exp(sk - m)
        p = (e / den).astype(BF16)
        outs.append(_dot(p, vh))
        if want_lse:
            lses.append(jnp.broadcast_to(m + jnp.log(den), (BAND, HEAD_DIM)))
    o_ref[...] = jnp.concatenate(outs, axis=1).astype(o_ref.dtype)
    if want_lse:
        lse_ref[...] = jnp.concatenate(lses, axis=1)


def _band_attention(qkv, *, q_blk, k_blk, v_blk, kv_width, kv_group, dist_scale, max_dist,
                    sinks=None, want_lse=False):
    r, l, _ = qkv.shape
    d = N_HEADS * HEAD_DIM
    prev = lambda blk: (lambda c, b: (c, jnp.maximum(b - 1, 0), blk))
    own = lambda blk: (lambda c, b: (c, b, blk))
    in_specs = [pl.BlockSpec((None, BAND, d), own(q_blk)),
                pl.BlockSpec((None, BAND, kv_width), prev(k_blk)),
                pl.BlockSpec((None, BAND, kv_width), own(k_blk)),
                pl.BlockSpec((None, BAND, kv_width), prev(v_blk)),
                pl.BlockSpec((None, BAND, kv_width), own(v_blk))]
    args = [qkv] * 5
    if sinks is not None:
        in_specs.append(_resident((1, N_HEADS)))
        args.append(sinks.reshape(1, N_HEADS).astype(F32))
    o_spec = pl.BlockSpec((BAND, d), lambda c, b: (b, c))
    out_specs = [o_spec]
    out_shape = [jax.ShapeDtypeStruct((l, r * d), BF16)]
    if want_lse:
        out_specs.append(o_spec)
        out_shape.append(jax.ShapeDtypeStruct((l, r * d), F32))
    outs = pl.pallas_call(
        functools.partial(_band_kernel, kv_group=kv_group, dist_scale=dist_scale, max_dist=max_dist,
                          use_sinks=sinks is not None, want_lse=want_lse),
        grid=(r, l // BAND),
        in_specs=in_specs, out_specs=out_specs, out_shape=out_shape,
        compiler_params=_params("parallel", "parallel"),
        name="band_attention",
    )(*args)
    return [o.reshape(l * r, d) for o in outs]


def _dot3(a, b):
    ah, al = _split(a)
    bh, bl = _split(b)
    return _dot(ah, bh) + (_dot(ah, bl) + _dot(al, bh))


def _dot3_nt(a, b):
    ah, al = _split(a)
    bh, bl = _split(b)
    return _dot_nt(ah, bh) + (_dot_nt(ah, bl) + _dot_nt(al, bh))


def _split3(a):
    h = a.astype(BF16).astype(F32)
    r = a - h
    m = r.astype(BF16).astype(F32)
    l = (r - m).astype(BF16).astype(F32)
    return h, m, l


_KV_STEP = 2 * C_BLOCK
_N_AUG = 16


def _moba_kernel(slope_ref, qt_ref, k_ref, vt_ref, km_ref, o_ref, sel_ref, s_ref, p_ref, *, nblk):
    hp = pl.program_id(0)
    qi = pl.program_id(1)
    tq = C_BLOCK
    pw = 2 * HEAD_DIM
    qt = qt_ref[...]
    rows = lax.broadcasted_iota(jnp.int32, (pw, tq), 0)
    blk = lax.broadcasted_iota(jnp.int32, (nblk, tq), 0)
    past = blk < qi
    aug_r = lax.broadcasted_iota(jnp.int32, (_N_AUG, tq), 0)
    tqf = lax.broadcasted_iota(jnp.int32, (_N_AUG, tq), 1).astype(F32)
    kcol = lax.broadcasted_iota(jnp.int32, (_KV_STEP, pw), 1)
    tkf = lax.broadcasted_iota(jnp.int32, (_KV_STEP, pw), 0).astype(F32)
    causal = (lax.broadcasted_iota(jnp.int32, (C_BLOCK, tq), 1)
              >= lax.broadcasted_iota(jnp.int32, (C_BLOCK, tq), 0))
    zpad = jnp.zeros((pw - _N_AUG, tq), BF16)
    k0 = pl.multiple_of(qi * C_BLOCK, C_BLOCK)
    k_own = k_ref[pl.ds(k0, C_BLOCK), :]

    qcat, kpos, c2, init = [], [], [], []
    for hh in range(2):
        c = slope_ref[2 * hp + hh] * LOG2E
        qm = jnp.where((rows >= hh * HEAD_DIM) & (rows < (hh + 1) * HEAD_DIM), qt, 0.0)
        gate = jnp.where(past, _dot3(km_ref[...], qm), NEG_INF)
        g = gate
        for r in range(C_TOPK):
            thr = jnp.max(g, axis=0, keepdims=True)
            if r < C_TOPK - 1:
                g = jnp.where(g >= thr, NEG_INF, g)
        sel_ref[hh] = jnp.where(past & (gate >= thr), 1.0, 0.0)

        qh, qmid, ql = _split3(-c * tqf)
        qpos = jnp.where(aug_r == 0, qh, jnp.where(aug_r == 1, qmid, jnp.where(aug_r == 2, ql,
                         jnp.where(aug_r < 6, 1.0, 0.0))))
        qb = (qm * (LOG2E * HEAD_DIM ** -0.5)).astype(BF16)
        qcat.append(jnp.concatenate([qb, qpos.astype(BF16), zpad], axis=0))
        kh, kmid, kl = _split3(c * tkf)
        kpos.append(jnp.where(kcol < 3, 1.0, jnp.where(kcol == 3, kh, jnp.where(kcol == 4, kmid,
                              jnp.where(kcol == 5, kl, 0.0)))).astype(BF16))
        c2.append(c)

        s = _dot(jnp.concatenate([k_own, kpos[hh][:C_BLOCK]], axis=1), qcat[hh])
        s = jnp.where(causal, s, NEG_INF)
        m0 = jnp.max(s, axis=0, keepdims=True)
        p = jnp.exp2(s - m0)
        l0 = jnp.sum(p, axis=0, keepdims=True)
        acc0 = _dot(vt_ref[pl.ds(hh * HEAD_DIM, HEAD_DIM), pl.ds(k0, C_BLOCK)], p.astype(BF16))
        init.append((m0, l0, acc0))

    nsub = _KV_STEP // C_BLOCK

    nsteps = (qi + nsub - 1) // nsub
    last_step = k_ref.shape[0] // _KV_STEP - 1

    def key_start(n):
        return pl.multiple_of(jnp.clip(n, 0, last_step) * _KV_STEP, _KV_STEP)

    def scores(n, slot):
        kb = k_ref[pl.ds(key_start(n), _KV_STEP), :]
        for hh in range(2):
            s_ref[slot, hh] = _dot(jnp.concatenate([kb, kpos[hh]], axis=1), qcat[hh])

    def pv(n, hh, slot):
        return _dot(vt_ref[pl.ds(hh * HEAD_DIM, HEAD_DIM), pl.ds(key_start(n), _KV_STEP)],
                    p_ref[slot, hh])

    def step(n, slot, carry):
        a_prev, state = carry
        off = (qi * C_BLOCK - n * _KV_STEP).astype(F32)
        scores(n + 1, 1 - slot)
        new_a, new_state = [], []
        for hh in range(2):
            m_i, l_i, acc = state[hh]
            acc = a_prev[hh] * acc + pv(n - 1, hh, 1 - slot)
            shift = c2[hh] * off
            sel = [sel_ref[hh, pl.ds(nsub * n + j, 1), :] > 0.5 for j in range(nsub)]
            m_new = m_i
            for j in range(nsub):
                mj = jnp.max(s_ref[slot, hh, j * C_BLOCK:(j + 1) * C_BLOCK, :], axis=0, keepdims=True)
                m_new = jnp.maximum(m_new, jnp.where(sel[j], mj - shift, NEG_INF))
            l_new = jnp.exp2(m_i - m_new) * l_i
            for j in range(nsub):
                rows = slice(j * C_BLOCK, (j + 1) * C_BLOCK)
                p = jnp.exp2(s_ref[slot, hh, rows, :] - jnp.where(sel[j], m_new + shift, BIG))
                p_ref[slot, hh, rows, :] = p.astype(BF16)
                l_new = l_new + jnp.sum(p, axis=0, keepdims=True)
            new_a.append(jnp.exp2(m_i - m_new))
            new_state.append((m_new, l_new, acc))
        return tuple(new_a), tuple(new_state)

    def body(n2, carry):
        return step(2 * n2 + 1, 1, step(2 * n2, 0, carry))

    p_ref[1] = jnp.zeros_like(p_ref[1])
    scores(0, 0)
    one_a = jnp.ones((1, tq), F32)
    npairs = (nsteps + 1) // 2
    a_prev, state = lax.fori_loop(0, npairs, body, ((one_a, one_a), tuple(init)))
    outs = []
    for hh in range(2):
        _, l, acc = state[hh]
        outs.append((a_prev[hh] * acc + pv(2 * npairs - 1, hh, 1)) / l)
    o_ref[...] = jnp.concatenate(outs, axis=0).T.astype(o_ref.dtype)


def _moba_attention(qt, k, vt, kmean):
    d, t = qt.shape
    nblk = t // C_BLOCK
    slopes = jnp.asarray([_alibi_slope(h, N_HEADS) for h in range(N_HEADS)], F32)
    pw = 2 * HEAD_DIM
    assert t % (2 * _KV_STEP) == 0
    return pl.pallas_call(
        functools.partial(_moba_kernel, nblk=nblk),
        grid=(d // pw, nblk),
        in_specs=[pl.BlockSpec(memory_space=pltpu.SMEM),
                  pl.BlockSpec((pw, C_BLOCK), lambda hp, i: (hp, i)),
                  pl.BlockSpec((t, pw), lambda hp, i: (0, hp)),
                  pl.BlockSpec((pw, t), lambda hp, i: (hp, 0)),
                  pl.BlockSpec((nblk, pw), lambda hp, i: (0, hp))],
        out_specs=pl.BlockSpec((C_BLOCK, pw), lambda hp, i: (i, hp)),
        out_shape=jax.ShapeDtypeStruct((t, d), BF16),
        scratch_shapes=[pltpu.VMEM((2, nblk, C_BLOCK), F32), pltpu.VMEM((2, 2, _KV_STEP, C_BLOCK), F32),
                        pltpu.VMEM((2, 2, _KV_STEP, C_BLOCK), BF16)],
        compiler_params=_params("parallel", "arbitrary"),
        name="moba_attention",
    )(slopes, qt, k, vt, kmean)


def _layer_norm(z, g, b):
    mu = jnp.mean(z, axis=-1, keepdims=True)
    zc = z - mu
    var = jnp.mean(zc * zc, axis=-1, keepdims=True)
    return zc * lax.rsqrt(var + LN_EPS) * g + b


def _outproj_ln_kernel(*refs, n_groups, alpha):
    o_refs = refs[:n_groups]
    lse_refs = refs[n_groups:2 * n_groups] if n_groups > 1 else ()
    pos = n_groups + len(lse_refs)
    wo_ref, x_ref, g_ref, b_ref, xn_ref, xb_ref = refs[pos:pos + 6]
    if n_groups == 1:
        o = o_refs[0][...]
    else:
        lses = [r[...] for r in lse_refs]
        m = functools.reduce(jnp.maximum, lses)
        es = [jnp.exp(l - m) for l in lses]
        den = functools.reduce(jnp.add, es)
        o = functools.reduce(jnp.add, [(e / den) * r[...].astype(F32) for e, r in zip(es, o_refs)])
        o = o.astype(BF16)
    y = _dot(o, wo_ref[...])
    xn = _layer_norm(alpha * x_ref[...] + y, g_ref[...], b_ref[...])
    xn_ref[...] = xn
    xb_ref[...] = _pack_rows(xn)


def _outproj_ln(os_, lses, wo, x, g, b, alpha, tm=512):
    t, d = x.shape
    n = len(os_)
    tile = pl.BlockSpec((tm, d), lambda i: (i, 0))
    return pl.pallas_call(
        functools.partial(_outproj_ln_kernel, n_groups=n, alpha=alpha),
        grid=(t // tm,),
        in_specs=[tile] * (n + len(lses)) + [_resident((d, d)), tile, _resident((1, d)), _resident((1, d))],
        out_specs=[tile, pl.BlockSpec((tm // 2, d), lambda i: (i, 0))],
        out_shape=[jax.ShapeDtypeStruct((t, d), F32), jax.ShapeDtypeStruct((t // 2, d), jnp.uint32)],
        compiler_params=_params("parallel"),
        name="outproj_ln",
    )(*os_, *lses, wo, x, g.reshape(1, d), b.reshape(1, d))


def _pack_rows(a):
    return pltpu.bitcast(a.astype(BF16), jnp.uint32)


def _pack_dup(a):
    bits = pltpu.bitcast(a.astype(BF16).astype(F32), jnp.uint32)
    return bits | (bits >> 16)


def _pack_weight_kernel(w_ref, o_ref):
    o_ref[...] = _pack_rows(w_ref[...])


def _pack_weight(w, tr=2048):
    r, c = w.shape
    return pl.pallas_call(
        _pack_weight_kernel,
        grid=(r // tr,),
        in_specs=[pl.BlockSpec((tr, c), lambda i: (i, 0))],
        out_specs=pl.BlockSpec((tr // 2, c), lambda i: (i, 0)),
        out_shape=jax.ShapeDtypeStruct((r // 2, c), jnp.uint32),
        compiler_params=_params("parallel"),
        name="pack_weight",
    )(w)


def _top_vals(s, k):
    out = []
    for r in range(k):
        m = jnp.max(s, axis=0, keepdims=True)
        out.append(m)
        if r < k - 1:
            s = jnp.where(s >= m, NEG_INF, s)
    return out


_CAND = [(i, j) for i in range(PEER_TOPK + 1) for j in range(PEER_TOPK + 1)
         if (i + 1) * (j + 1) <= PEER_TOPK + 1]
_NCAND = -(-len(_CAND) // 8) * 8


def _router_kernel(x_ref, wh_ref, wl_ref, sk_ref, r2_ref, e2_ref, n1_ref, w_ref, q_ref, cand_ref):
    xh, xl = _split(x_ref[...])
    q_ref[...] = _dot(xh, wh_ref[...]) + (_dot(xh, wl_ref[...]) + _dot(xl, wh_ref[...]))
    tt = x_ref.shape[0]
    cand_ref[...] = jnp.full((_NCAND, tt), NEG_INF, F32)
    nk = PEER_NKEYS
    for h in range(PEER_HEADS):
        s1 = _dot3_nt(sk_ref[0], q_ref[:, (2 * h) * nk:(2 * h + 1) * nk])
        s2 = _dot3_nt(sk_ref[1], q_ref[:, (2 * h + 1) * nk:(2 * h + 2) * nk])
        a = _top_vals(s1, PEER_TOPK + 1)
        b = _top_vals(s2, PEER_TOPK + 1)
        for r, (i, j) in enumerate(_CAND):
            cand_ref[r:r + 1, :] = a[i] + b[j]
        v = _top_vals(cand_ref[...], PEER_TOPK + 1)
        thr = 0.5 * (v[PEER_TOPK - 1] + v[PEER_TOPK])
        z = functools.reduce(jnp.add, [jnp.exp(v[r] - v[0]) for r in range(PEER_TOPK)])
        cut = thr - s1
        rank2 = jnp.zeros_like(s2)
        count1 = jnp.zeros_like(s1)
        for r in range(PEER_TOPK):
            rank2 = jnp.where(s2 < b[r], r + 1.0, rank2)
            count1 = jnp.where(b[r] >= cut, r + 1.0, count1)
        r2_ref[h] = _pack_rows(rank2)
        e2_ref[h] = _pack_rows(jnp.exp(s2 - b[0]))
        n1_ref[h] = _pack_dup(count1)
        w_ref[h] = _pack_dup(jnp.exp(s1 - a[0]) / z)


def _router(xn, wq_hi, wq_lo, subkeys, tt=256):
    t, d = xn.shape
    nq = wq_hi.shape[1]
    shape = (PEER_HEADS, PEER_NKEYS, t)
    ospec = pl.BlockSpec((PEER_HEADS, PEER_NKEYS, tt), lambda i: (0, 0, i))
    pspec = pl.BlockSpec((PEER_HEADS, PEER_NKEYS // 2, tt), lambda i: (0, 0, i))
    packed = jax.ShapeDtypeStruct((PEER_HEADS, PEER_NKEYS // 2, t), jnp.uint32)
    return pl.pallas_call(
        _router_kernel,
        grid=(t // tt,),
        in_specs=[pl.BlockSpec((tt, d), lambda i: (i, 0)), _resident((d, nq)), _resident((d, nq)),
                  _resident(subkeys.shape)],
        out_specs=[pspec, pspec, ospec, ospec],
        out_shape=[packed, packed, jax.ShapeDtypeStruct(shape, jnp.uint32),
                   jax.ShapeDtypeStruct(shape, jnp.uint32)],
        scratch_shapes=[pltpu.VMEM((tt, nq), F32), pltpu.VMEM((_NCAND, tt), F32)],
        compiler_params=_params("parallel"),
        name="peer_router",
    )(xn, wq_hi, wq_lo, subkeys)


def _gelu(x):
    return 0.5 * x * (1.0 + lax.erf(x * (2.0 ** -0.5)))


_BF16_ROWS = 16
_PEER_SUB = 256
_SUBLANES = 8


def _peer_kernel(x_ref, u_ref, v_ref, r2_ref, e2_ref, n1_ref, w_ref, y_ref, acc_ref, h_ref, act_ref):
    j = pl.program_id(1)
    te, tt = h_ref.shape
    nk = PEER_NKEYS

    @pl.when(j == 0)
    def _():
        acc_ref[...] = jnp.zeros_like(acc_ref)

    xb = pltpu.bitcast(x_ref[...], BF16)
    for k in range(te // _PEER_SUB):
        uk = pltpu.bitcast(u_ref[pl.ds(k * _PEER_SUB // 2, _PEER_SUB // 2), :], BF16)
        h_ref[pl.ds(k * _PEER_SUB, _PEER_SUB), :] = _dot_nt(uk, xb)
    n_i1 = te // nk
    for c in range(tt // 128):
        cols = pl.ds(c * 128, 128)
        for v in range(nk // _BF16_ROWS):
            prow = pl.ds(v * _SUBLANES, _SUBLANES)
            g = [jnp.zeros((_BF16_ROWS, 128), BF16) for _ in range(n_i1)]
            for h in range(PEER_HEADS):
                r2 = pltpu.bitcast(r2_ref[h, prow, cols], BF16)
                e2 = pltpu.bitcast(e2_ref[h, prow, cols], BF16)
                for ii in range(n_i1):
                    row = pl.ds(ii, 1)
                    n_row = pltpu.bitcast(jnp.broadcast_to(n1_ref[h, row, cols], (_SUBLANES, 128)), BF16)
                    w_row = pltpu.bitcast(jnp.broadcast_to(w_ref[h, row, cols], (_SUBLANES, 128)), BF16)
                    g[ii] = g[ii] + jnp.where(r2 < n_row, e2, 0.0) * w_row
            for ii in range(n_i1):
                rows = pl.ds(ii * nk + v * _BF16_ROWS, _BF16_ROWS)
                act_ref[rows, cols] = _gelu(h_ref[rows, cols]).astype(BF16) * g[ii]
    acc_ref[...] += lax.dot_general(pltpu.bitcast(v_ref[...], BF16), act_ref[...], (((0,), (0,)), ((), ())),
                                    preferred_element_type=F32)

    @pl.when(j == pl.num_programs(1) - 1)
    def _():
        y_ref[...] = acc_ref[...].T


def _peer_experts(xb, u, v, r2, e2, n1, w, tt=512, te=2048):
    t, d = 2 * xb.shape[0], xb.shape[1]
    ne = 2 * u.shape[0]
    rspec = pl.BlockSpec((PEER_HEADS, te // PEER_NKEYS, tt), lambda i, j: (0, j, i))
    pspec = pl.BlockSpec((PEER_HEADS, PEER_NKEYS // 2, tt), lambda i, j: (0, 0, i))
    return pl.pallas_call(
        _peer_kernel,
        grid=(t // tt, ne // te),
        in_specs=[pl.BlockSpec((tt // 2, d), lambda i, j: (i, 0)),
                  pl.BlockSpec((te // 2, d), lambda i, j: (j, 0)),
                  pl.BlockSpec((te // 2, d), lambda i, j: (j, 0)),
                  pspec, pspec, rspec, rspec],
        out_specs=pl.BlockSpec((tt, d), lambda i, j: (i, 0)),
        out_shape=jax.ShapeDtypeStruct((t, d), F32),
        scratch_shapes=[pltpu.VMEM((d, tt), F32), pltpu.VMEM((te, tt), F32), pltpu.VMEM((te, tt), BF16)],
        compiler_params=_params("parallel", "arbitrary"),
        name="peer_experts",
    )(xb, u, v, r2, e2, n1, w)


def _ln_ple_kernel(x_ref, y_ref, g_ref, b_ref, p_ref, wp_ref, wg_ref, bg_ref, o_ref, *, alpha):
    xn = _layer_norm(alpha * x_ref[...] + y_ref[...], g_ref[...], b_ref[...])
    gate = jax.nn.sigmoid(_dot(xn.astype(BF16), wg_ref[...]) + bg_ref[...])
    o_ref[...] = xn + gate * _dot(p_ref[...].astype(BF16), wp_ref[...])


def _ln_ple(x, y, g, b, p, wp, wg, bg, alpha, tm=512):
    t, d = x.shape
    dp = p.shape[1]
    tile = pl.BlockSpec((tm, d), lambda i: (i, 0))
    vec = _resident((1, d))
    return pl.pallas_call(
        functools.partial(_ln_ple_kernel, alpha=alpha),
        grid=(t // tm,),
        in_specs=[tile, tile, vec, vec, pl.BlockSpec((tm, dp), lambda i: (i, 0)), _resident((dp, d)),
                  _resident((d, d)), vec],
        out_specs=tile,
        out_shape=jax.ShapeDtypeStruct((t, d), F32),
        compiler_params=_params("parallel"),
        name="ln_ple",
    )(x, y, g.reshape(1, d), b.reshape(1, d), p, wp, wg, bg.reshape(1, d))


def _mixer_a(x, w_qkv, sinks):
    d = N_HEADS * HEAD_DIM
    qkv = _proj(x, w_qkv.astype(BF16))
    kvw = A_KV_HEADS * HEAD_DIM
    o, = _band_attention(qkv, q_blk=0, k_blk=d // kvw, v_blk=d // kvw + 1, kv_width=kvw,
                         kv_group=N_HEADS // A_KV_HEADS, dist_scale=1.0, max_dist=A_WINDOW - 1,
                         sinks=sinks)
    return [o], []


def _mixer_b(x, w_qkv):
    d = N_HEADS * HEAD_DIM
    ng = len(B_PATTERNS)
    wk = w_qkv[:, ng * d:(ng + 1) * d]
    wv = w_qkv[:, (ng + 1) * d:]
    os_, lses = [], []
    for gi, (w, r) in enumerate(B_PATTERNS):
        wg = jnp.concatenate([w_qkv[:, gi * d:(gi + 1) * d], wk, wv], axis=1).astype(BF16)
        qkv = _proj(x, wg, dil=r)
        o, lse = _band_attention(qkv, q_blk=0, k_blk=1, v_blk=2, kv_width=d, kv_group=1,
                                 dist_scale=float(r), max_dist=w // r, want_lse=True)
        os_.append(o)
        lses.append(lse)
    return os_, lses


def _mixer_c(x, w_qkv):
    d = N_HEADS * HEAD_DIM
    wq, wk, wv = w_qkv[:, :d], w_qkv[:, d:2 * d], w_qkv[:, 2 * d:]
    qt, k, vt, kmean = _proj_c(x, wq.T.astype(BF16), wk.astype(BF16), wv.T.astype(BF16))
    return [_moba_attention(qt, k, vt, kmean.reshape(-1, d))], []


def kernel(x, p, a_w_qkv, a_sinks, a_w_o, b_w_qkv, b_w_o, c_w_qkv, c_w_o, ln1_g, ln1_b, ln2_g, ln2_b,
           peer_w_q, peer_subkeys, peer_u, peer_v, ple_w, ple_gate_w, ple_gate_b):
    depth = p.shape[0]
    alpha = (2 * depth) ** 0.25
    bsz, seq, d = x.shape
    assert bsz == 1 and seq % (B_PATTERNS[-1][1] * BAND) == 0
    xt = x.reshape(seq, d)
    for i in range(depth):
        kind, j = i % 3, i // 3
        if kind == 0:
            os_, lses = _mixer_a(xt, a_w_qkv[j], a_sinks[j])
            wo = a_w_o[j]
        elif kind == 1:
            os_, lses = _mixer_b(xt, b_w_qkv[j])
            wo = b_w_o[j]
        else:
            os_, lses = _mixer_c(xt, c_w_qkv[j])
            wo = c_w_o[j]
        x1, x1b = _outproj_ln(os_, lses, wo.astype(BF16), xt, ln1_g[i], ln1_b[i], alpha)
        wq_hi, wq_lo = _split(peer_w_q[i])
        routing = _router(x1, wq_hi, wq_lo, peer_subkeys[i])
        y = _peer_experts(x1b, _pack_weight(peer_u[i]), _pack_weight(peer_v[i]), *routing)
        xt = _ln_ple(x1, y, ln2_g[i], ln2_b[i], p[i].reshape(seq, -1), ple_w[i].astype(BF16),
                     ple_gate_w[i].astype(BF16), ple_gate_b[i], alpha)
    return xt.reshape(bsz, seq, d)
```

```python
import functools

import jax
import jax.numpy as jnp
import ml_dtypes
import numpy as np
from jax import lax
from jax.experimental import pallas as pl
from jax.experimental.pallas import tpu as pltpu

F32 = jnp.float32
BF16 = jnp.bfloat16
NEG_INF = float("-inf")
BIG = 1e30
LOG2E = 1.4426950408889634

D_MODEL = 1024
HEAD_DIM = 64
N_HEADS = 16
BAND = 128
A_KV_HEADS = 2
A_WINDOW = 128
B_PATTERNS = ((128, 1), (512, 4), (2048, 16))
C_BLOCK = 256
C_TOPK = 3
PEER_HEADS = 8
PEER_NKEYS = 128
PEER_TOPK = 16
LN_EPS = 1e-5
VMEM_LIMIT = 56 * 1024 * 1024


def _params(*sem):
    return pltpu.CompilerParams(dimension_semantics=sem, vmem_limit_bytes=VMEM_LIMIT)


def _dot(a, b):
    return jnp.dot(a, b, preferred_element_type=F32)


def _dot_nt(a, b):
    return lax.dot_general(a, b, (((1,), (1,)), ((), ())), preferred_element_type=F32)


def _split(a):
    hi = a.astype(BF16)
    lo = (a - hi.astype(F32)).astype(BF16)
    return hi, lo


def _alibi_slope(h, n):
    return 2.0 ** (-8.0 * (h + 1) / n)


def _resident(shape):
    zeros = (0,) * len(shape)
    return pl.BlockSpec(shape, lambda *_: zeros)


def _proj_t_kernel(x_ref, wqt_ref, wk_ref, wvt_ref, qt_ref, k_ref, vt_ref, *, q_scale):
    xb = x_ref[...].astype(BF16)
    qt_ref[...] = _pack_rows(_dot_nt(wqt_ref[...], xb) * q_scale)
    k_ref[...] = _pack_rows(_dot(xb, wk_ref[...]))
    vt_ref[...] = _pack_rows(_dot_nt(wvt_ref[...], xb))


def _proj_t(x, wqt, wk, wvt, dil=1, tm=512):
    t, kdim = x.shape
    d, kw = wqt.shape[0], wk.shape[1]
    l = t // dil
    tm = min(tm, l)
    u32 = jnp.uint32
    return pl.pallas_call(
        functools.partial(_proj_t_kernel, q_scale=LOG2E * HEAD_DIM ** -0.5),
        grid=(dil, l // tm),
        in_specs=[pl.BlockSpec((tm, kdim), lambda c, i: (i, c)), _resident((d, kdim)), _resident((kdim, kw)),
                  _resident((kw, kdim))],
        out_specs=[pl.BlockSpec((None, d // 2, tm), lambda c, i: (c, 0, i)),
                   pl.BlockSpec((None, tm // 2, kw), lambda c, i: (c, i, 0)),
                   pl.BlockSpec((None, kw // 2, tm), lambda c, i: (c, 0, i))],
        out_shape=[jax.ShapeDtypeStruct((dil, d // 2, l), u32), jax.ShapeDtypeStruct((dil, l // 2, kw), u32),
                   jax.ShapeDtypeStruct((dil, kw // 2, l), u32)],
        compiler_params=_params("parallel", "parallel"),
        name="proj_band",
    )(x.reshape(l, dil * kdim), wqt, wk, wvt)


def _proj_c_kernel(x_ref, wqt_ref, wk_ref, wvt_ref, qt_ref, k_ref, vt_ref, km_ref, *, nblk):
    xb = x_ref[...].astype(BF16)
    qt_ref[...] = _dot_nt(wqt_ref[...], xb)
    kf = _dot(xb, wk_ref[...])
    k_ref[...] = kf.astype(BF16)
    vt_ref[...] = _dot_nt(wvt_ref[...], xb).astype(BF16)
    for r in range(nblk):
        km_ref[r] = jnp.mean(kf[r * C_BLOCK:(r + 1) * C_BLOCK], axis=0, keepdims=True)


def _proj_c(x, wqt, wk, wvt, tm=512):
    t, k = x.shape
    d = wk.shape[1]
    nblk = tm // C_BLOCK
    return pl.pallas_call(
        functools.partial(_proj_c_kernel, nblk=nblk),
        grid=(t // tm,),
        in_specs=[pl.BlockSpec((tm, k), lambda i: (i, 0)), _resident((d, k)), _resident((k, d)),
                  _resident((d, k))],
        out_specs=[pl.BlockSpec((d, tm), lambda i: (0, i)), pl.BlockSpec((tm, d), lambda i: (i, 0)),
                   pl.BlockSpec((d, tm), lambda i: (0, i)), pl.BlockSpec((nblk, 1, d), lambda i: (i, 0, 0))],
        out_shape=[jax.ShapeDtypeStruct((d, t), F32), jax.ShapeDtypeStruct((t, d), BF16),
                   jax.ShapeDtypeStruct((d, t), BF16), jax.ShapeDtypeStruct((t // C_BLOCK, 1, d), F32)],
        compiler_params=_params("parallel"),
        name="proj_moba",
    )(x, wqt, wk, wvt)


_N_AUG = 16
_PAIR = 2 * HEAD_DIM


def _split3_np(a):
    bf = ml_dtypes.bfloat16
    h = a.astype(bf).astype(np.float32)
    m = (a - h).astype(bf).astype(np.float32)
    l = (a - h - m).astype(bf).astype(np.float32)
    return h, m, l


def _pack_rows_np(a):
    bits = a.astype(ml_dtypes.bfloat16).view(np.uint16).astype(np.uint32)
    return bits[0::2] | (bits[1::2] << 16)


def _band_tables(dist_scale):
    ki = np.arange(2 * BAND, dtype=np.float32)
    kaug = np.zeros((2 * BAND, _PAIR), np.float32)
    kaug[:, 0:3] = 1.0
    kaug[:, 3:6] = ki[:, None]
    qaug = np.zeros((N_HEADS, _N_AUG, BAND), np.float32)
    qpos = np.arange(BAND, dtype=np.float32) + BAND
    for h in range(N_HEADS):
        c = np.float32(_alibi_slope(h, N_HEADS) * dist_scale * LOG2E)
        qaug[h, 0:3] = np.stack(_split3_np(-c * qpos))
        qaug[h, 3:6] = np.stack(_split3_np(np.full((BAND,), c, np.float32)))
    return _pack_rows_np(kaug), np.stack([_pack_rows_np(qaug[h]) for h in range(N_HEADS)])


def _band_kernel(*refs, kv_group, max_dist, use_sinks, want_lse):
    q_ref, kp_ref, ko_ref, vp_ref, vo_ref, kaug_ref, qaug_ref = refs[:7]
    pos = 7
    sink_ref = None
    if use_sinks:
        sink_ref = refs[pos]
        pos += 1
    o_ref = refs[pos]
    lse_ref = refs[pos + 1] if want_lse else None

    b = pl.program_id(1)
    ki = lax.broadcasted_iota(jnp.int32, (2 * BAND, BAND), 0)
    qi = lax.broadcasted_iota(jnp.int32, (2 * BAND, BAND), 1)
    dist = qi + BAND - ki
    valid = (dist >= 0) & (dist <= max_dist) & (ki >= jnp.where(b > 0, 0, BAND))
    qt = pltpu.bitcast(q_ref[...], BF16)
    k_all = jnp.concatenate([pltpu.bitcast(kp_ref[...], BF16), pltpu.bitcast(ko_ref[...], BF16)], axis=0)
    vt_all = jnp.concatenate([pltpu.bitcast(vp_ref[...], BF16), pltpu.bitcast(vo_ref[...], BF16)], axis=1)
    kaug = pltpu.bitcast(kaug_ref[...], BF16)
    zslot = jnp.zeros((HEAD_DIM, BAND), BF16)
    zpad = jnp.zeros((_PAIR - _N_AUG, BAND), BF16)
    n_kv = N_HEADS // kv_group
    scores = [None] * N_HEADS
    for pair in range(-(-n_kv // 2)):
        heads = [h for h in range(N_HEADS) if (h // kv_group) // 2 == pair]
        qcats = []
        for h in heads:
            qh = qt[h * HEAD_DIM:(h + 1) * HEAD_DIM]
            slot = [qh, zslot] if (h // kv_group) % 2 == 0 else [zslot, qh]
            qcats.append(jnp.concatenate(slot + [pltpu.bitcast(qaug_ref[h], BF16), zpad], axis=0))
        kcat = jnp.concatenate([k_all[:, pair * _PAIR:(pair + 1) * _PAIR], kaug], axis=1)
        s_all = _dot(kcat, jnp.concatenate(qcats, axis=1))
        for n, h in enumerate(heads):
            scores[h] = s_all[:, n * BAND:(n + 1) * BAND]
    probs, dens, lses = [], [], []
    for h in range(N_HEADS):
        s = jnp.where(valid, scores[h], NEG_INF)
        m = jnp.max(s, axis=0, keepdims=True)
        if use_sinks:
            sk = sink_ref[:, h:h + 1] * LOG2E
            m = jnp.maximum(m, sk)
        p = jnp.exp2(s - m)
        den = jnp.sum(p, axis=0, keepdims=True)
        if use_sinks:
            den = den + jnp.exp2(sk - m)
        probs.append(p.astype(BF16))
        dens.append(den)
        if want_lse:
            lses.append(jnp.broadcast_to((m + jnp.log2(den)) * (1.0 / LOG2E), (HEAD_DIM, BAND)))
    outs = []
    for g in range(n_kv):
        hs = range(g * kv_group, (g + 1) * kv_group)
        pv = _dot(vt_all[g * HEAD_DIM:(g + 1) * HEAD_DIM], jnp.concatenate([probs[h] for h in hs], axis=1))
        outs += [pv[:, n * BAND:(n + 1) * BAND] / dens[h] for n, h in enumerate(hs)]
    o_ref[...] = jnp.concatenate(outs, axis=0).T.astype(o_ref.dtype)
    if want_lse:
        lse_ref[...] = jnp.concatenate(lses, axis=0).T


def _band_attention(qt, k, vt, *, kv_group, dist_scale, max_dist, sinks=None, want_lse=False):
    r, _, l = qt.shape
    d = N_HEADS * HEAD_DIM
    kw = k.shape[2]
    prev = lambda c, b: jnp.maximum(b - 1, 0)
    kaug, qaug = _band_tables(dist_scale)
    in_specs = [pl.BlockSpec((None, d // 2, BAND), lambda c, b: (c, 0, b)),
                pl.BlockSpec((None, BAND // 2, kw), lambda c, b: (c, prev(c, b), 0)),
                pl.BlockSpec((None, BAND // 2, kw), lambda c, b: (c, b, 0)),
                pl.BlockSpec((None, kw // 2, BAND), lambda c, b: (c, 0, prev(c, b))),
                pl.BlockSpec((None, kw // 2, BAND), lambda c, b: (c, 0, b)),
                _resident(kaug.shape), _resident(qaug.shape)]
    args = [qt, k, k, vt, vt, jnp.asarray(kaug), jnp.asarray(qaug)]
    if sinks is not None:
        in_specs.append(_resident((1, N_HEADS)))
        args.append(sinks.reshape(1, N_HEADS).astype(F32))
    o_spec = pl.BlockSpec((BAND, d), lambda c, b: (b, c))
    out_specs = [o_spec]
    out_shape = [jax.ShapeDtypeStruct((l, r * d), BF16)]
    if want_lse:
        out_specs.append(o_spec)
        out_shape.append(jax.ShapeDtypeStruct((l, r * d), F32))
    outs = pl.pallas_call(
        functools.partial(_band_kernel, kv_group=kv_group, max_dist=max_dist,
                          use_sinks=sinks is not None, want_lse=want_lse),
        grid=(r, l // BAND),
        in_specs=in_specs, out_specs=out_specs, out_shape=out_shape,
        compiler_params=_params("parallel", "parallel"),
        name="band_attention",
    )(*args)
    return [o.reshape(l * r, d) for o in outs]


def _dot3(a, b):
    ah, al = _split(a)
    bh, bl = _split(b)
    return _dot(ah, bh) + (_dot(ah, bl) + _dot(al, bh))


def _dot3_nt(a, b):
    ah, al = _split(a)
    bh, bl = _split(b)
    return _dot_nt(ah, bh) + (_dot_nt(ah, bl) + _dot_nt(al, bh))


def _split3(a):
    h = a.astype(BF16).astype(F32)
    r = a - h
    m = r.astype(BF16).astype(F32)
    l = (r - m).astype(BF16).astype(F32)
    return h, m, l


_KV_STEP = 2 * C_BLOCK


def _moba_kernel(slope_ref, qt_ref, k_ref, vt_ref, km_ref, o_ref, sel_ref, s_ref, p_ref, *, nblk):
    hp = pl.program_id(0)
    qi = pl.program_id(1)
    tq = C_BLOCK
    pw = 2 * HEAD_DIM
    qt = qt_ref[...]
    rows = lax.broadcasted_iota(jnp.int32, (pw, tq), 0)
    blk = lax.broadcasted_iota(jnp.int32, (nblk, tq), 0)
    past = blk < qi
    aug_r = lax.broadcasted_iota(jnp.int32, (_N_AUG, tq), 0)
    tqf = lax.broadcasted_iota(jnp.int32, (_N_AUG, tq), 1).astype(F32)
    kcol = lax.broadcasted_iota(jnp.int32, (_KV_STEP, pw), 1)
    tkf = lax.broadcasted_iota(jnp.int32, (_KV_STEP, pw), 0).astype(F32)
    causal = (lax.broadcasted_iota(jnp.int32, (C_BLOCK, tq), 1)
              >= lax.broadcasted_iota(jnp.int32, (C_BLOCK, tq), 0))
    zpad = jnp.zeros((pw - _N_AUG, tq), BF16)
    k0 = pl.multiple_of(qi * C_BLOCK, C_BLOCK)
    k_own = k_ref[pl.ds(k0, C_BLOCK), :]

    qcat, kpos, c2, init = [], [], [], []
    for hh in range(2):
        c = slope_ref[2 * hp + hh] * LOG2E
        qm = jnp.where((rows >= hh * HEAD_DIM) & (rows < (hh + 1) * HEAD_DIM), qt, 0.0)
        gate = jnp.where(past, _dot3(km_ref[...], qm), NEG_INF)
        g = gate
        for r in range(C_TOPK):
            thr = jnp.max(g, axis=0, keepdims=True)
            if r < C_TOPK - 1:
                g = jnp.where(g >= thr, NEG_INF, g)
        sel_ref[hh] = jnp.where(past & (gate >= thr), 1.0, 0.0)

        qh, qmid, ql = _split3(-c * tqf)
        qpos = jnp.where(aug_r == 0, qh, jnp.where(aug_r == 1, qmid, jnp.where(aug_r == 2, ql,
                         jnp.where(aug_r < 6, 1.0, 0.0))))
        qb = (qm * (LOG2E * HEAD_DIM ** -0.5)).astype(BF16)
        qcat.append(jnp.concatenate([qb, qpos.astype(BF16), zpad], axis=0))
        kh, kmid, kl = _split3(c * tkf)
        kpos.append(jnp.where(kcol < 3, 1.0, jnp.where(kcol == 3, kh, jnp.where(kcol == 4, kmid,
                              jnp.where(kcol == 5, kl, 0.0)))).astype(BF16))
        c2.append(c)

        s = _dot(jnp.concatenate([k_own, kpos[hh][:C_BLOCK]], axis=1), qcat[hh])
        s = jnp.where(causal, s, NEG_INF)
        m0 = jnp.max(s, axis=0, keepdims=True)
        p = jnp.exp2(s - m0)
        l0 = jnp.sum(p, axis=0, keepdims=True)
        acc0 = _dot(vt_ref[pl.ds(hh * HEAD_DIM, HEAD_DIM), pl.ds(k0, C_BLOCK)], p.astype(BF16))
        init.append((m0, l0, acc0))

    nsub = _KV_STEP // C_BLOCK

    nsteps = (qi + nsub - 1) // nsub
    last_step = k_ref.shape[0] // _KV_STEP - 1

    def key_start(n):
        return pl.multiple_of(jnp.clip(n, 0, last_step) * _KV_STEP, _KV_STEP)

    def scores(n, slot):
        kb = k_ref[pl.ds(key_start(n), _KV_STEP), :]
        for hh in range(2):
            s_ref[slot, hh] = _dot(jnp.concatenate([kb, kpos[hh]], axis=1), qcat[hh])

    def pv(n, hh, slot):
        return _dot(vt_ref[pl.ds(hh * HEAD_DIM, HEAD_DIM), pl.ds(key_start(n), _KV_STEP)],
                    p_ref[slot, hh])

    def step(n, slot, carry):
        a_prev, state = carry
        off = (qi * C_BLOCK - n * _KV_STEP).astype(F32)
        scores(n + 1, 1 - slot)
        new_a, new_state = [], []
        for hh in range(2):
            m_i, l_i, acc = state[hh]
            acc = a_prev[hh] * acc + pv(n - 1, hh, 1 - slot)
            shift = c2[hh] * off
            sel = [sel_ref[hh, pl.ds(nsub * n + j, 1), :] > 0.5 for j in range(nsub)]
            m_new = m_i
            for j in range(nsub):
                mj = jnp.max(s_ref[slot, hh, j * C_BLOCK:(j + 1) * C_BLOCK, :], axis=0, keepdims=True)
                m_new = jnp.maximum(m_new, jnp.where(sel[j], mj - shift, NEG_INF))
            l_new = jnp.exp2(m_i - m_new) * l_i
            for j in range(nsub):
                rows = slice(j * C_BLOCK, (j + 1) * C_BLOCK)
                p = jnp.exp2(s_ref[slot, hh, rows, :] - jnp.where(sel[j], m_new + shift, BIG))
                p_ref[slot, hh, rows, :] = p.astype(BF16)
                l_new = l_new + jnp.sum(p, axis=0, keepdims=True)
            new_a.append(jnp.exp2(m_i - m_new))
            new_state.append((m_new, l_new, acc))
        return tuple(new_a), tuple(new_state)

    def body(n2, carry):
        return step(2 * n2 + 1, 1, step(2 * n2, 0, carry))

    p_ref[1] = jnp.zeros_like(p_ref[1])
    scores(0, 0)
    one_a = jnp.ones((1, tq), F32)
    npairs = (nsteps + 1) // 2
    a_prev, state = lax.fori_loop(0, npairs, body, ((one_a, one_a), tuple(init)))
    outs = []
    for hh in range(2):
        _, l, acc = state[hh]
        outs.append((a_prev[hh] * acc + pv(2 * npairs - 1, hh, 1)) / l)
    o_ref[...] = jnp.concatenate(outs, axis=0).T.astype(o_ref.dtype)


def _moba_attention(qt, k, vt, kmean):
    d, t = qt.shape
    nblk = t // C_BLOCK
    slopes = jnp.asarray([_alibi_slope(h, N_HEADS) for h in range(N_HEADS)], F32)
    pw = 2 * HEAD_DIM
    assert t % (2 * _KV_STEP) == 0
    return pl.pallas_call(
        functools.partial(_moba_kernel, nblk=nblk),
        grid=(d // pw, nblk),
        in_specs=[pl.BlockSpec(memory_space=pltpu.SMEM),
                  pl.BlockSpec((pw, C_BLOCK), lambda hp, i: (hp, i)),
                  pl.BlockSpec((t, pw), lambda hp, i: (0, hp)),
                  pl.BlockSpec((pw, t), lambda hp, i: (hp, 0)),
                  pl.BlockSpec((nblk, pw), lambda hp, i: (0, hp))],
        out_specs=pl.BlockSpec((C_BLOCK, pw), lambda hp, i: (i, hp)),
        out_shape=jax.ShapeDtypeStruct((t, d), BF16),
        scratch_shapes=[pltpu.VMEM((2, nblk, C_BLOCK), F32), pltpu.VMEM((2, 2, _KV_STEP, C_BLOCK), F32),
                        pltpu.VMEM((2, 2, _KV_STEP, C_BLOCK), BF16)],
        compiler_params=_params("parallel", "arbitrary"),
        name="moba_attention",
    )(slopes, qt, k, vt, kmean)


def _layer_norm(z, g, b):
    mu = jnp.mean(z, axis=-1, keepdims=True)
    zc = z - mu
    var = jnp.mean(zc * zc, axis=-1, keepdims=True)
    return zc * lax.rsqrt(var + LN_EPS) * g + b


def _outproj_ln_kernel(*refs, n_groups, alpha):
    o_refs = refs[:n_groups]
    lse_refs = refs[n_groups:2 * n_groups] if n_groups > 1 else ()
    pos = n_groups + len(lse_refs)
    wo_ref, x_ref, g_ref, b_ref, xn_ref, xb_ref = refs[pos:pos + 6]
    if n_groups == 1:
        o = o_refs[0][...]
    else:
        lses = [r[...] for r in lse_refs]
        m = functools.reduce(jnp.maximum, lses)
        es = [jnp.exp(l - m) for l in lses]
        den = functools.reduce(jnp.add, es)
        o = functools.reduce(jnp.add, [(e / den) * r[...].astype(F32) for e, r in zip(es, o_refs)])
        o = o.astype(BF16)
    y = _dot(o, wo_ref[...])
    xn = _layer_norm(alpha * x_ref[...] + y, g_ref[...], b_ref[...])
    xn_ref[...] = xn
    xb_ref[...] = _pack_rows(xn)


def _outproj_ln(os_, lses, wo, x, g, b, alpha, tm=512):
    t, d = x.shape
    n = len(os_)
    tile = pl.BlockSpec((tm, d), lambda i: (i, 0))
    return pl.pallas_call(
        functools.partial(_outproj_ln_kernel, n_groups=n, alpha=alpha),
        grid=(t // tm,),
        in_specs=[tile] * (n + len(lses)) + [_resident((d, d)), tile, _resident((1, d)), _resident((1, d))],
        out_specs=[tile, pl.BlockSpec((tm // 2, d), lambda i: (i, 0))],
        out_shape=[jax.ShapeDtypeStruct((t, d), F32), jax.ShapeDtypeStruct((t // 2, d), jnp.uint32)],
        compiler_params=_params("parallel"),
        name="outproj_ln",
    )(*os_, *lses, wo, x, g.reshape(1, d), b.reshape(1, d))


def _pack_rows(a):
    return pltpu.bitcast(a.astype(BF16), jnp.uint32)


def _pack_dup(a):
    bits = pltpu.bitcast(a.astype(BF16).astype(F32), jnp.uint32)
    return bits | (bits >> 16)


def _pack_weight_kernel(w_ref, o_ref):
    o_ref[...] = _pack_rows(w_ref[...])


def _pack_weight(w, tr=2048):
    r, c = w.shape
    return pl.pallas_call(
        _pack_weight_kernel,
        grid=(r // tr,),
        in_specs=[pl.BlockSpec((tr, c), lambda i: (i, 0))],
        out_specs=pl.BlockSpec((tr // 2, c), lambda i: (i, 0)),
        out_shape=jax.ShapeDtypeStruct((r // 2, c), jnp.uint32),
        compiler_params=_params("parallel"),
        name="pack_weight",
    )(w)


def _top_vals(s, k):
    out = []
    for r in range(k):
        m = jnp.max(s, axis=0, keepdims=True)
        out.append(m)
        if r < k - 1:
            s = jnp.where(s >= m, NEG_INF, s)
    return out


_CAND = [(i, j) for i in range(PEER_TOPK + 1) for j in range(PEER_TOPK + 1)
         if (i + 1) * (j + 1) <= PEER_TOPK + 1]
_NCAND = -(-len(_CAND) // 8) * 8


def _router_kernel(x_ref, wh_ref, wl_ref, sk_ref, r2_ref, e2_ref, n1_ref, w_ref, q_ref, cand_ref):
    xh, xl = _split(x_ref[...])
    q_ref[...] = _dot(xh, wh_ref[...]) + (_dot(xh, wl_ref[...]) + _dot(xl, wh_ref[...]))
    tt = x_ref.shape[0]
    cand_ref[...] = jnp.full((_NCAND, tt), NEG_INF, F32)
    nk = PEER_NKEYS
    for h in range(PEER_HEADS):
        s1 = _dot3_nt(sk_ref[0], q_ref[:, (2 * h) * nk:(2 * h + 1) * nk])
        s2 = _dot3_nt(sk_ref[1], q_ref[:, (2 * h + 1) * nk:(2 * h + 2) * nk])
        a = _top_vals(s1, PEER_TOPK + 1)
        b = _top_vals(s2, PEER_TOPK + 1)
        for r, (i, j) in enumerate(_CAND):
            cand_ref[r:r + 1, :] = a[i] + b[j]
        v = _top_vals(cand_ref[...], PEER_TOPK + 1)
        thr = 0.5 * (v[PEER_TOPK - 1] + v[PEER_TOPK])
        z = functools.reduce(jnp.add, [jnp.exp(v[r] - v[0]) for r in range(PEER_TOPK)])
        cut = thr - s1
        rank2 = jnp.zeros_like(s2)
        count1 = jnp.zeros_like(s1)
        for r in range(PEER_TOPK):
            rank2 = jnp.where(s2 < b[r], r + 1.0, rank2)
            count1 = jnp.where(b[r] >= cut, r + 1.0, count1)
        r2_ref[h] = _pack_rows(rank2)
        e2_ref[h] = _pack_rows(jnp.exp(s2 - b[0]))
        n1_ref[h] = _pack_dup(count1)
        w_ref[h] = _pack_dup(jnp.exp(s1 - a[0]) / z)


def _router(xn, wq_hi, wq_lo, subkeys, tt=256):
    t, d = xn.shape
    nq = wq_hi.shape[1]
    shape = (PEER_HEADS, PEER_NKEYS, t)
    ospec = pl.BlockSpec((PEER_HEADS, PEER_NKEYS, tt), lambda i: (0, 0, i))
    pspec = pl.BlockSpec((PEER_HEADS, PEER_NKEYS // 2, tt), lambda i: (0, 0, i))
    packed = jax.ShapeDtypeStruct((PEER_HEADS, PEER_NKEYS // 2, t), jnp.uint32)
    return pl.pallas_call(
        _router_kernel,
        grid=(t // tt,),
        in_specs=[pl.BlockSpec((tt, d), lambda i: (i, 0)), _resident((d, nq)), _resident((d, nq)),
                  _resident(subkeys.shape)],
        out_specs=[pspec, pspec, ospec, ospec],
        out_shape=[packed, packed, jax.ShapeDtypeStruct(shape, jnp.uint32),
                   jax.ShapeDtypeStruct(shape, jnp.uint32)],
        scratch_shapes=[pltpu.VMEM((tt, nq), F32), pltpu.VMEM((_NCAND, tt), F32)],
        compiler_params=_params("parallel"),
        name="peer_router",
    )(xn, wq_hi, wq_lo, subkeys)


def _gelu(x):
    return 0.5 * x * (1.0 + lax.erf(x * (2.0 ** -0.5)))


_BF16_ROWS = 16
_PEER_SUB = 256
_SUBLANES = 8


def _peer_kernel(x_ref, u_ref, v_ref, r2_ref, e2_ref, n1_ref, w_ref, y_ref, acc_ref, h_ref, act_ref):
    j = pl.program_id(1)
    te, tt = h_ref.shape
    nk = PEER_NKEYS

    @pl.when(j == 0)
    def _():
        acc_ref[...] = jnp.zeros_like(acc_ref)

    xb = pltpu.bitcast(x_ref[...], BF16)
    for k in range(te // _PEER_SUB):
        uk = pltpu.bitcast(u_ref[pl.ds(k * _PEER_SUB // 2, _PEER_SUB // 2), :], BF16)
        h_ref[pl.ds(k * _PEER_SUB, _PEER_SUB), :] = _dot_nt(uk, xb)
    n_i1 = te // nk
    for c in range(tt // 128):
        cols = pl.ds(c * 128, 128)
        for v in range(nk // _BF16_ROWS):
            prow = pl.ds(v * _SUBLANES, _SUBLANES)
            g = [jnp.zeros((_BF16_ROWS, 128), BF16) for _ in range(n_i1)]
            for h in range(PEER_HEADS):
                r2 = pltpu.bitcast(r2_ref[h, prow, cols], BF16)
                e2 = pltpu.bitcast(e2_ref[h, prow, cols], BF16)
                for ii in range(n_i1):
                    row = pl.ds(ii, 1)
                    n_row = pltpu.bitcast(jnp.broadcast_to(n1_ref[h, row, cols], (_SUBLANES, 128)), BF16)
                    w_row = pltpu.bitcast(jnp.broadcast_to(w_ref[h, row, cols], (_SUBLANES, 128)), BF16)
                    g[ii] = g[ii] + jnp.where(r2 < n_row, e2, 0.0) * w_row
            for ii in range(n_i1):
                rows = pl.ds(ii * nk + v * _BF16_ROWS, _BF16_ROWS)
                act_ref[rows, cols] = _gelu(h_ref[rows, cols]).astype(BF16) * g[ii]
    acc_ref[...] += lax.dot_general(pltpu.bitcast(v_ref[...], BF16), act_ref[...], (((0,), (0,)), ((), ())),
                                    preferred_element_type=F32)

    @pl.when(j == pl.num_programs(1) - 1)
    def _():
        y_ref[...] = acc_ref[...].T


def _peer_experts(xb, u, v, r2, e2, n1, w, tt=512, te=2048):
    t, d = 2 * xb.shape[0], xb.shape[1]
    ne = 2 * u.shape[0]
    rspec = pl.BlockSpec((PEER_HEADS, te // PEER_NKEYS, tt), lambda i, j: (0, j, i))
    pspec = pl.BlockSpec((PEER_HEADS, PEER_NKEYS // 2, tt), lambda i, j: (0, 0, i))
    return pl.pallas_call(
        _peer_kernel,
        grid=(t // tt, ne // te),
        in_specs=[pl.BlockSpec((tt // 2, d), lambda i, j: (i, 0)),
                  pl.BlockSpec((te // 2, d), lambda i, j: (j, 0)),
                  pl.BlockSpec((te // 2, d), lambda i, j: (j, 0)),
                  pspec, pspec, rspec, rspec],
        out_specs=pl.BlockSpec((tt, d), lambda i, j: (i, 0)),
        out_shape=jax.ShapeDtypeStruct((t, d), F32),
        scratch_shapes=[pltpu.VMEM((d, tt), F32), pltpu.VMEM((te, tt), F32), pltpu.VMEM((te, tt), BF16)],
        compiler_params=_params("parallel", "arbitrary"),
        name="peer_experts",
    )(xb, u, v, r2, e2, n1, w)


def _ln_ple_kernel(x_ref, y_ref, g_ref, b_ref, p_ref, wp_ref, wg_ref, bg_ref, o_ref, *, alpha):
    xn = _layer_norm(alpha * x_ref[...] + y_ref[...], g_ref[...], b_ref[...])
    gate = jax.nn.sigmoid(_dot(xn.astype(BF16), wg_ref[...]) + bg_ref[...])
    o_ref[...] = xn + gate * _dot(p_ref[...].astype(BF16), wp_ref[...])


def _ln_ple(x, y, g, b, p, wp, wg, bg, alpha, tm=512):
    t, d = x.shape
    dp = p.shape[1]
    tile = pl.BlockSpec((tm, d), lambda i: (i, 0))
    vec = _resident((1, d))
    return pl.pallas_call(
        functools.partial(_ln_ple_kernel, alpha=alpha),
        grid=(t // tm,),
        in_specs=[tile, tile, vec, vec, pl.BlockSpec((tm, dp), lambda i: (i, 0)), _resident((dp, d)),
                  _resident((d, d)), vec],
        out_specs=tile,
        out_shape=jax.ShapeDtypeStruct((t, d), F32),
        compiler_params=_params("parallel"),
        name="ln_ple",
    )(x, y, g.reshape(1, d), b.reshape(1, d), p, wp, wg, bg.reshape(1, d))


def _mixer_a(x, w_qkv, sinks):
    d = N_HEADS * HEAD_DIM
    kvw = A_KV_HEADS * HEAD_DIM
    wq, wk, wv = w_qkv[:, :d], w_qkv[:, d:d + kvw], w_qkv[:, d + kvw:]
    qt, k, vt = _proj_t(x, wq.T.astype(BF16), wk.astype(BF16), wv.T.astype(BF16))
    o, = _band_attention(qt, k, vt, kv_group=N_HEADS // A_KV_HEADS, dist_scale=1.0,
                         max_dist=A_WINDOW - 1, sinks=sinks)
    return [o], []


def _mixer_b(x, w_qkv):
    d = N_HEADS * HEAD_DIM
    ng = len(B_PATTERNS)
    wk = w_qkv[:, ng * d:(ng + 1) * d].astype(BF16)
    wvt = w_qkv[:, (ng + 1) * d:].T.astype(BF16)
    os_, lses = [], []
    for gi, (w, r) in enumerate(B_PATTERNS):
        qt, k, vt = _proj_t(x, w_qkv[:, gi * d:(gi + 1) * d].T.astype(BF16), wk, wvt, dil=r)
        o, lse = _band_attention(qt, k, vt, kv_group=1, dist_scale=float(r), max_dist=w // r, want_lse=True)
        os_.append(o)
        lses.append(lse)
    return os_, lses


def _mixer_c(x, w_qkv):
    d = N_HEADS * HEAD_DIM
    wq, wk, wv = w_qkv[:, :d], w_qkv[:, d:2 * d], w_qkv[:, 2 * d:]
    qt, k, vt, kmean = _proj_c(x, wq.T.astype(BF16), wk.astype(BF16), wv.T.astype(BF16))
    return [_moba_attention(qt, k, vt, kmean.reshape(-1, d))], []


def kernel(x, p, a_w_qkv, a_sinks, a_w_o, b_w_qkv, b_w_o, c_w_qkv, c_w_o, ln1_g, ln1_b, ln2_g, ln2_b,
           peer_w_q, peer_subkeys, peer_u, peer_v, ple_w, ple_gate_w, ple_gate_b):
    depth = p.shape[0]
    alpha = (2 * depth) ** 0.25
    bsz, seq, d = x.shape
    assert bsz == 1 and seq % (B_PATTERNS[-1][1] * BAND) == 0
    xt = x.reshape(seq, d)
    for i in range(depth):
        kind, j = i % 3, i // 3
        if kind == 0:
            os_, lses = _mixer_a(xt, a_w_qkv[j], a_sinks[j])
            wo = a_w_o[j]
        elif kind == 1:
            os_, lses = _mixer_b(xt, b_w_qkv[j])
            wo = b_w_o[j]
        else:
            os_, lses = _mixer_c(xt, c_w_qkv[j])
            wo = c_w_o[j]
        x1, x1b = _outproj_ln(os_, lses, wo.astype(BF16), xt, ln1_g[i], ln1_b[i], alpha)
        wq_hi, wq_lo = _split(peer_w_q[i])
        routing = _router(x1, wq_hi, wq_lo, peer_subkeys[i])
        y = _peer_experts(x1b, _pack_weight(peer_u[i]), _pack_weight(peer_v[i]), *routing)
        xt = _ln_ple(x1, y, ln2_g[i], ln2_b[i], p[i].reshape(seq, -1), ple_w[i].astype(BF16),
                     ple_gate_w[i].astype(BF16), ple_gate_b[i], alpha)
    return xt.reshape(bsz, seq, d)
```

```python
import functools

import jax
import jax.numpy as jnp
import ml_dtypes
import numpy as np
from jax import lax
from jax.experimental import pallas as pl
from jax.experimental.pallas import tpu as pltpu

F32 = jnp.float32
BF16 = jnp.bfloat16
NEG_INF = float("-inf")
BIG = 1e30
LOG2E = 1.4426950408889634

D_MODEL = 1024
HEAD_DIM = 64
N_HEADS = 16
BAND = 128
A_KV_HEADS = 2
A_WINDOW = 128
B_PATTERNS = ((128, 1), (512, 4), (2048, 16))
C_BLOCK = 256
C_TOPK = 3
PEER_HEADS = 8
PEER_NKEYS = 128
PEER_TOPK = 16
LN_EPS = 1e-5
VMEM_LIMIT = 56 * 1024 * 1024


def _params(*sem):
    return pltpu.CompilerParams(dimension_semantics=sem, vmem_limit_bytes=VMEM_LIMIT)


def _dot(a, b):
    return jnp.dot(a, b, preferred_element_type=F32)


def _dot_nt(a, b):
    return lax.dot_general(a, b, (((1,), (1,)), ((), ())), preferred_element_type=F32)


def _split(a):
    hi = a.astype(BF16)
    lo = (a - hi.astype(F32)).astype(BF16)
    return hi, lo


def _alibi_slope(h, n):
    return 2.0 ** (-8.0 * (h + 1) / n)


def _resident(shape):
    zeros = (0,) * len(shape)
    return pl.BlockSpec(shape, lambda *_: zeros)


def _proj_t_kernel(x_ref, wqt_ref, wk_ref, wvt_ref, qt_ref, k_ref, vt_ref, *, q_scale):
    xb = x_ref[...].astype(BF16)
    qt_ref[...] = _pack_rows(_dot_nt(wqt_ref[...], xb) * q_scale)
    k_ref[...] = _pack_rows(_dot(xb, wk_ref[...]))
    vt_ref[...] = _pack_rows(_dot_nt(wvt_ref[...], xb))


def _proj_t(x, wqt, wk, wvt, dil=1, tm=512):
    t, kdim = x.shape
    d, kw = wqt.shape[0], wk.shape[1]
    l = t // dil
    tm = min(tm, l)
    u32 = jnp.uint32
    return pl.pallas_call(
        functools.partial(_proj_t_kernel, q_scale=LOG2E * HEAD_DIM ** -0.5),
        grid=(dil, l // tm),
        in_specs=[pl.BlockSpec((tm, kdim), lambda c, i: (i, c)), _resident((d, kdim)), _resident((kdim, kw)),
                  _resident((kw, kdim))],
        out_specs=[pl.BlockSpec((None, d // 2, tm), lambda c, i: (c, 0, i)),
                   pl.BlockSpec((None, tm // 2, kw), lambda c, i: (c, i, 0)),
                   pl.BlockSpec((None, kw // 2, tm), lambda c, i: (c, 0, i))],
        out_shape=[jax.ShapeDtypeStruct((dil, d // 2, l), u32), jax.ShapeDtypeStruct((dil, l // 2, kw), u32),
                   jax.ShapeDtypeStruct((dil, kw // 2, l), u32)],
        compiler_params=_params("parallel", "parallel"),
        name="proj_band",
    )(x.reshape(l, dil * kdim), wqt, wk, wvt)


def _proj_c_kernel(x_ref, wqt_ref, wk_ref, wvt_ref, qt_ref, k_ref, vt_ref, km_ref, *, nblk):
    xb = x_ref[...].astype(BF16)
    qt_ref[...] = _dot_nt(wqt_ref[...], xb)
    kf = _dot(xb, wk_ref[...])
    k_ref[...] = _pack_rows(kf)
    vt_ref[...] = _pack_rows(_dot_nt(wvt_ref[...], xb))
    for r in range(nblk):
        km_ref[r] = jnp.mean(kf[r * C_BLOCK:(r + 1) * C_BLOCK], axis=0, keepdims=True)


def _proj_c(x, wqt, wk, wvt, tm=512):
    t, k = x.shape
    d = wk.shape[1]
    nblk = tm // C_BLOCK
    return pl.pallas_call(
        functools.partial(_proj_c_kernel, nblk=nblk),
        grid=(t // tm,),
        in_specs=[pl.BlockSpec((tm, k), lambda i: (i, 0)), _resident((d, k)), _resident((k, d)),
                  _resident((d, k))],
        out_specs=[pl.BlockSpec((d, tm), lambda i: (0, i)), pl.BlockSpec((tm // 2, d), lambda i: (i, 0)),
                   pl.BlockSpec((d // 2, tm), lambda i: (0, i)), pl.BlockSpec((nblk, 1, d), lambda i: (i, 0, 0))],
        out_shape=[jax.ShapeDtypeStruct((d, t), F32), jax.ShapeDtypeStruct((t // 2, d), jnp.uint32),
                   jax.ShapeDtypeStruct((d // 2, t), jnp.uint32), jax.ShapeDtypeStruct((t // C_BLOCK, 1, d), F32)],
        compiler_params=_params("parallel"),
        name="proj_moba",
    )(x, wqt, wk, wvt)


_N_AUG = 16
_PAIR = 2 * HEAD_DIM


def _split3_np(a):
    bf = ml_dtypes.bfloat16
    h = a.astype(bf).astype(np.float32)
    m = (a - h).astype(bf).astype(np.float32)
    l = (a - h - m).astype(bf).astype(np.float32)
    return h, m, l


def _pack_rows_np(a):
    bits = a.astype(ml_dtypes.bfloat16).view(np.uint16).astype(np.uint32)
    return bits[0::2] | (bits[1::2] << 16)


def _band_tables(dist_scale):
    ki = np.arange(2 * BAND, dtype=np.float32)
    kaug = np.zeros((2 * BAND, _PAIR), np.float32)
    kaug[:, 0:3] = 1.0
    kaug[:, 3:6] = ki[:, None]
    qaug = np.zeros((N_HEADS, _N_AUG, BAND), np.float32)
    qpos = np.arange(BAND, dtype=np.float32) + BAND
    for h in range(N_HEADS):
        c = np.float32(_alibi_slope(h, N_HEADS) * dist_scale * LOG2E)
        qaug[h, 0:3] = np.stack(_split3_np(-c * qpos))
        qaug[h, 3:6] = np.stack(_split3_np(np.full((BAND,), c, np.float32)))
    return _pack_rows_np(kaug), np.stack([_pack_rows_np(qaug[h]) for h in range(N_HEADS)])


def _band_kernel(*refs, kv_group, max_dist, use_sinks, want_lse):
    q_ref, kp_ref, ko_ref, vp_ref, vo_ref, kaug_ref, qaug_ref = refs[:7]
    pos = 7
    sink_ref = None
    if use_sinks:
        sink_ref = refs[pos]
        pos += 1
    o_ref = refs[pos]
    lse_ref = refs[pos + 1] if want_lse else None

    b = pl.program_id(1)
    ki = lax.broadcasted_iota(jnp.int32, (2 * BAND, BAND), 0)
    qi = lax.broadcasted_iota(jnp.int32, (2 * BAND, BAND), 1)
    dist = qi + BAND - ki
    valid = (dist >= 0) & (dist <= max_dist) & (ki >= jnp.where(b > 0, 0, BAND))
    qt = pltpu.bitcast(q_ref[...], BF16)
    k_all = jnp.concatenate([pltpu.bitcast(kp_ref[...], BF16), pltpu.bitcast(ko_ref[...], BF16)], axis=0)
    vt_all = jnp.concatenate([pltpu.bitcast(vp_ref[...], BF16), pltpu.bitcast(vo_ref[...], BF16)], axis=1)
    kaug = pltpu.bitcast(kaug_ref[...], BF16)
    zslot = jnp.zeros((HEAD_DIM, BAND), BF16)
    zpad = jnp.zeros((_PAIR - _N_AUG, BAND), BF16)
    n_kv = N_HEADS // kv_group
    scores = [None] * N_HEADS
    for pair in range(-(-n_kv // 2)):
        heads = [h for h in range(N_HEADS) if (h // kv_group) // 2 == pair]
        qcats = []
        for h in heads:
            qh = qt[h * HEAD_DIM:(h + 1) * HEAD_DIM]
            slot = [qh, zslot] if (h // kv_group) % 2 == 0 else [zslot, qh]
            qcats.append(jnp.concatenate(slot + [pltpu.bitcast(qaug_ref[h], BF16), zpad], axis=0))
        kcat = jnp.concatenate([k_all[:, pair * _PAIR:(pair + 1) * _PAIR], kaug], axis=1)
        s_all = _dot(kcat, jnp.concatenate(qcats, axis=1))
        for n, h in enumerate(heads):
            scores[h] = s_all[:, n * BAND:(n + 1) * BAND]
    probs, dens, lses = [], [], []
    for h in range(N_HEADS):
        s = jnp.where(valid, scores[h], NEG_INF)
        m = jnp.max(s, axis=0, keepdims=True)
        if use_sinks:
            sk = sink_ref[:, h:h + 1] * LOG2E
            m = jnp.maximum(m, sk)
        p = jnp.exp2(s - m)
        den = jnp.sum(p, axis=0, keepdims=True)
        if use_sinks:
            den = den + jnp.exp2(sk - m)
        probs.append(p.astype(BF16))
        dens.append(den)
        if want_lse:
            lses.append(jnp.broadcast_to((m + jnp.log2(den)) * (1.0 / LOG2E), (HEAD_DIM, BAND)))
    outs = []
    for g in range(n_kv):
        hs = range(g * kv_group, (g + 1) * kv_group)
        pv = _dot(vt_all[g * HEAD_DIM:(g + 1) * HEAD_DIM], jnp.concatenate([probs[h] for h in hs], axis=1))
        outs += [pv[:, n * BAND:(n + 1) * BAND] / dens[h] for n, h in enumerate(hs)]
    o_ref[...] = jnp.concatenate(outs, axis=0).T.astype(o_ref.dtype)
    if want_lse:
        lse_ref[...] = jnp.concatenate(lses, axis=0).T


def _band_attention(qt, k, vt, *, kv_group, dist_scale, max_dist, sinks=None, want_lse=False):
    r, _, l = qt.shape
    d = N_HEADS * HEAD_DIM
    kw = k.shape[2]
    prev = lambda c, b: jnp.maximum(b - 1, 0)
    kaug, qaug = _band_tables(dist_scale)
    in_specs = [pl.BlockSpec((None, d // 2, BAND), lambda c, b: (c, 0, b)),
                pl.BlockSpec((None, BAND // 2, kw), lambda c, b: (c, prev(c, b), 0)),
                pl.BlockSpec((None, BAND // 2, kw), lambda c, b: (c, b, 0)),
                pl.BlockSpec((None, kw // 2, BAND), lambda c, b: (c, 0, prev(c, b))),
                pl.BlockSpec((None, kw // 2, BAND), lambda c, b: (c, 0, b)),
                _resident(kaug.shape), _resident(qaug.shape)]
    args = [qt, k, k, vt, vt, jnp.asarray(kaug), jnp.asarray(qaug)]
    if sinks is not None:
        in_specs.append(_resident((1, N_HEADS)))
        args.append(sinks.reshape(1, N_HEADS).astype(F32))
    o_spec = pl.BlockSpec((BAND, d), lambda c, b: (b, c))
    out_specs = [o_spec]
    out_shape = [jax.ShapeDtypeStruct((l, r * d), BF16)]
    if want_lse:
        out_specs.append(o_spec)
        out_shape.append(jax.ShapeDtypeStruct((l, r * d), F32))
    outs = pl.pallas_call(
        functools.partial(_band_kernel, kv_group=kv_group, max_dist=max_dist,
                          use_sinks=sinks is not None, want_lse=want_lse),
        grid=(r, l // BAND),
        in_specs=in_specs, out_specs=out_specs, out_shape=out_shape,
        compiler_params=_params("parallel", "parallel"),
        name="band_attention",
    )(*args)
    return [o.reshape(l * r, d) for o in outs]


def _dot3(a, b):
    ah, al = _split(a)
    bh, bl = _split(b)
    return _dot(ah, bh) + (_dot(ah, bl) + _dot(al, bh))


def _dot3_nt(a, b):
    ah, al = _split(a)
    bh, bl = _split(b)
    return _dot_nt(ah, bh) + (_dot_nt(ah, bl) + _dot_nt(al, bh))


_KV_STEP = 2 * C_BLOCK
_L_ROWS = 16


def _moba_tables():
    tk = np.arange(_KV_STEP)
    kaug = np.zeros((_KV_STEP, _PAIR), np.float32)
    kaug[:, 0:3] = 1.0
    kaug[:, 3:6] = (tk - tk % 2)[:, None]
    kaug[:, 6:9] = (tk % 2)[:, None]
    qaug = np.zeros((N_HEADS, _N_AUG, C_BLOCK), np.float32)
    tq = np.arange(C_BLOCK, dtype=np.float32)
    cs = np.asarray([_alibi_slope(h, N_HEADS) * LOG2E for h in range(N_HEADS)], np.float32)
    for h in range(N_HEADS):
        qaug[h, 0:3] = np.stack(_split3_np(-cs[h] * tq))
        qaug[h, 3:6] = qaug[h, 6:9] = np.stack(_split3_np(np.full((C_BLOCK,), cs[h], np.float32)))
    return cs, _pack_rows_np(kaug), np.stack([_pack_rows_np(qaug[h]) for h in range(N_HEADS)])


def _moba_kernel(c_ref, qt_ref, k_ref, vt_ref, km_ref, kaug_ref, qaug_ref, o_ref, sel_ref, s_ref, p_ref,
                 *, nblk):
    hp = pl.program_id(0)
    qi = pl.program_id(1)
    tq = C_BLOCK
    qt = qt_ref[...]
    rows = lax.broadcasted_iota(jnp.int32, (_PAIR, tq), 0)
    blk = lax.broadcasted_iota(jnp.int32, (nblk, tq), 0)
    past = blk < qi
    causal = (lax.broadcasted_iota(jnp.int32, (C_BLOCK, tq), 1)
              >= lax.broadcasted_iota(jnp.int32, (C_BLOCK, tq), 0))
    kaug = pltpu.bitcast(kaug_ref[...], BF16)
    zpad = jnp.zeros((_PAIR - _N_AUG, tq), BF16)
    ones = jnp.ones((_L_ROWS, _KV_STEP), BF16)

    qcats, c2 = [], []
    for hh in range(2):
        qm = jnp.where((rows >= hh * HEAD_DIM) & (rows < (hh + 1) * HEAD_DIM), qt, 0.0)
        gate = jnp.where(past, _dot3(km_ref[...], qm), NEG_INF)
        g = gate
        for r in range(C_TOPK):
            thr = jnp.max(g, axis=0, keepdims=True)
            if r < C_TOPK - 1:
                g = jnp.where(g >= thr, NEG_INF, g)
        sel_ref[hh] = jnp.where(past & (gate >= thr), 1.0, 0.0)
        qb = (qm * (LOG2E * HEAD_DIM ** -0.5)).astype(BF16)
        qcats.append(jnp.concatenate([qb, pltpu.bitcast(qaug_ref[hh], BF16), zpad], axis=0))
        c2.append(c_ref[2 * hp + hh])
    qcat = jnp.concatenate(qcats, axis=1)

    def keys(start, size):
        kb = pltpu.bitcast(k_ref[pl.ds(pl.multiple_of(start // 2, size // 2), size // 2), :], BF16)
        return jnp.concatenate([kb, kaug[:size]], axis=1)

    def values(hh, start, size):
        vt = pltpu.bitcast(vt_ref[pl.ds(hh * HEAD_DIM // 2, HEAD_DIM // 2), pl.ds(start, size)], BF16)
        return jnp.concatenate([vt, ones[:, :size]], axis=0)

    k0 = pl.multiple_of(qi * C_BLOCK, C_BLOCK)
    s_own = _dot(keys(k0, C_BLOCK), qcat)
    init = []
    for hh in range(2):
        s = jnp.where(causal, s_own[:, hh * tq:(hh + 1) * tq], NEG_INF)
        m0 = jnp.max(s, axis=0, keepdims=True)
        init.append((m0, _dot(values(hh, k0, C_BLOCK), jnp.exp2(s - m0).astype(BF16))))

    nsub = _KV_STEP // C_BLOCK
    nsteps = (qi + nsub - 1) // nsub
    last_step = 2 * k_ref.shape[0] // _KV_STEP - 1

    def key_start(n):
        return pl.multiple_of(jnp.clip(n, 0, last_step) * _KV_STEP, _KV_STEP)

    def scores(n, slot):
        s_ref[slot] = _dot(keys(key_start(n), _KV_STEP), qcat)

    def pv(n, hh, slot):
        return _dot(values(hh, key_start(n), _KV_STEP), p_ref[slot, hh])

    def step(n, slot, carry):
        a_prev, state = carry
        off = (qi * C_BLOCK - n * _KV_STEP).astype(F32)
        scores(n + 1, 1 - slot)
        new_a, new_state = [], []
        for hh in range(2):
            m_i, acc = state[hh]
            acc = a_prev[hh] * acc + pv(n - 1, hh, 1 - slot)
            shift = c2[hh] * off
            cols = slice(hh * tq, (hh + 1) * tq)
            sel = [sel_ref[hh, pl.ds(nsub * n + j, 1), :] > 0.5 for j in range(nsub)]
            m_new = m_i
            for j in range(nsub):
                mj = jnp.max(s_ref[slot, j * C_BLOCK:(j + 1) * C_BLOCK, cols], axis=0, keepdims=True)
                m_new = jnp.maximum(m_new, jnp.where(sel[j], mj - shift, NEG_INF))
            for j in range(nsub):
                krows = slice(j * C_BLOCK, (j + 1) * C_BLOCK)
                p = jnp.exp2(s_ref[slot, krows, cols] - jnp.where(sel[j], m_new + shift, BIG))
                p_ref[slot, hh, krows, :] = p.astype(BF16)
            new_a.append(jnp.exp2(m_i - m_new))
            new_state.append((m_new, acc))
        return tuple(new_a), tuple(new_state)

    def body(n2, carry):
        return step(2 * n2 + 1, 1, step(2 * n2, 0, carry))

    p_ref[1] = jnp.zeros_like(p_ref[1])
    scores(0, 0)
    one_a = jnp.ones((1, tq), F32)
    npairs = (nsteps + 1) // 2
    a_prev, state = lax.fori_loop(0, npairs, body, ((one_a, one_a), tuple(init)))
    outs = []
    for hh in range(2):
        acc = a_prev[hh] * state[hh][1] + pv(2 * npairs - 1, hh, 1)
        outs.append(acc[:HEAD_DIM] / acc[HEAD_DIM:HEAD_DIM + 1])
    o_ref[...] = jnp.concatenate(outs, axis=0).T.astype(o_ref.dtype)


def _moba_attention(qt, k, vt, kmean):
    d, t = qt.shape
    nblk = t // C_BLOCK
    cs, kaug, qaug = _moba_tables()
    assert t % (2 * _KV_STEP) == 0
    return pl.pallas_call(
        functools.partial(_moba_kernel, nblk=nblk),
        grid=(d // _PAIR, nblk),
        in_specs=[pl.BlockSpec(memory_space=pltpu.SMEM),
                  pl.BlockSpec((_PAIR, C_BLOCK), lambda hp, i: (hp, i)),
                  pl.BlockSpec((t // 2, _PAIR), lambda hp, i: (0, hp)),
                  pl.BlockSpec((_PAIR // 2, t), lambda hp, i: (hp, 0)),
                  pl.BlockSpec((nblk, _PAIR), lambda hp, i: (0, hp)),
                  _resident(kaug.shape),
                  pl.BlockSpec((2,) + qaug.shape[1:], lambda hp, i: (hp, 0, 0))],
        out_specs=pl.BlockSpec((C_BLOCK, _PAIR), lambda hp, i: (i, hp)),
        out_shape=jax.ShapeDtypeStruct((t, d), BF16),
        scratch_shapes=[pltpu.VMEM((2, nblk, C_BLOCK), F32), pltpu.VMEM((2, _KV_STEP, 2 * C_BLOCK), F32),
                        pltpu.VMEM((2, 2, _KV_STEP, C_BLOCK), BF16)],
        compiler_params=_params("parallel", "arbitrary"),
        name="moba_attention",
    )(jnp.asarray(cs), qt, k, vt, kmean, jnp.asarray(kaug), jnp.asarray(qaug))


def _layer_norm(z, g, b):
    mu = jnp.mean(z, axis=-1, keepdims=True)
    zc = z - mu
    var = jnp.mean(zc * zc, axis=-1, keepdims=True)
    return zc * lax.rsqrt(var + LN_EPS) * g + b


def _outproj_ln_kernel(*refs, n_groups, alpha):
    o_refs = refs[:n_groups]
    lse_refs = refs[n_groups:2 * n_groups] if n_groups > 1 else ()
    pos = n_groups + len(lse_refs)
    wo_ref, x_ref, g_ref, b_ref, xn_ref, xb_ref = refs[pos:pos + 6]
    if n_groups == 1:
        o = o_refs[0][...]
    else:
        lses = [r[...] for r in lse_refs]
        m = functools.reduce(jnp.maximum, lses)
        es = [jnp.exp(l - m) for l in lses]
        den = functools.reduce(jnp.add, es)
        o = functools.reduce(jnp.add, [(e / den) * r[...].astype(F32) for e, r in zip(es, o_refs)])
        o = o.astype(BF16)
    y = _dot(o, wo_ref[...])
    xn = _layer_norm(alpha * x_ref[...] + y, g_ref[...], b_ref[...])
    xn_ref[...] = xn
    xb_ref[...] = _pack_rows(xn)


def _outproj_ln(os_, lses, wo, x, g, b, alpha, tm=512):
    t, d = x.shape
    n = len(os_)
    tile = pl.BlockSpec((tm, d), lambda i: (i, 0))
    return pl.pallas_call(
        functools.partial(_outproj_ln_kernel, n_groups=n, alpha=alpha),
        grid=(t // tm,),
        in_specs=[tile] * (n + len(lses)) + [_resident((d, d)), tile, _resident((1, d)), _resident((1, d))],
        out_specs=[tile, pl.BlockSpec((tm // 2, d), lambda i: (i, 0))],
        out_shape=[jax.ShapeDtypeStruct((t, d), F32), jax.ShapeDtypeStruct((t // 2, d), jnp.uint32)],
        compiler_params=_params("parallel"),
        name="outproj_ln",
    )(*os_, *lses, wo, x, g.reshape(1, d), b.reshape(1, d))


def _pack_rows(a):
    return pltpu.bitcast(a.astype(BF16), jnp.uint32)


def _pack_dup(a):
    bits = pltpu.bitcast(a.astype(BF16).astype(F32), jnp.uint32)
    return bits | (bits >> 16)


def _pack_weight_kernel(w_ref, o_ref):
    o_ref[...] = _pack_rows(w_ref[...])


def _pack_weight(w, layer, tr=2048):
    _, r, c = w.shape
    return pl.pallas_call(
        _pack_weight_kernel,
        grid=(r // tr,),
        in_specs=[pl.BlockSpec((None, tr, c), lambda i: (layer, i, 0))],
        out_specs=pl.BlockSpec((tr // 2, c), lambda i: (i, 0)),
        out_shape=jax.ShapeDtypeStruct((r // 2, c), jnp.uint32),
        compiler_params=_params("parallel"),
        name="pack_weight",
    )(w)


def _top_vals(s, k):
    out = []
    for r in range(k):
        m = jnp.max(s, axis=0, keepdims=True)
        out.append(m)
        if r < k - 1:
            s = jnp.where(s >= m, NEG_INF, s)
    return out


_CAND = [(i, j) for i in range(PEER_TOPK + 1) for j in range(PEER_TOPK + 1)
         if (i + 1) * (j + 1) <= PEER_TOPK + 1]
_NCAND = -(-len(_CAND) // 8) * 8


def _router_kernel(x_ref, wh_ref, wl_ref, sk_ref, r2_ref, e2_ref, n1_ref, w_ref, q_ref, cand_ref):
    xh, xl = _split(x_ref[...])
    q_ref[...] = _dot(xh, wh_ref[...]) + (_dot(xh, wl_ref[...]) + _dot(xl, wh_ref[...]))
    tt = x_ref.shape[0]
    cand_ref[...] = jnp.full((_NCAND, tt), NEG_INF, F32)
    nk = PEER_NKEYS
    for h in range(PEER_HEADS):
        s1 = _dot3_nt(sk_ref[0], q_ref[:, (2 * h) * nk:(2 * h + 1) * nk])
        s2 = _dot3_nt(sk_ref[1], q_ref[:, (2 * h + 1) * nk:(2 * h + 2) * nk])
        a = _top_vals(s1, PEER_TOPK + 1)
        b = _top_vals(s2, PEER_TOPK + 1)
        for r, (i, j) in enumerate(_CAND):
            cand_ref[r:r + 1, :] = a[i] + b[j]
        v = _top_vals(cand_ref[...], PEER_TOPK + 1)
        thr = 0.5 * (v[PEER_TOPK - 1] + v[PEER_TOPK])
        z = functools.reduce(jnp.add, [jnp.exp(v[r] - v[0]) for r in range(PEER_TOPK)])
        cut = thr - s1
        rank2 = jnp.zeros_like(s2)
        count1 = jnp.zeros_like(s1)
        for r in range(PEER_TOPK):
            rank2 = jnp.where(s2 < b[r], r + 1.0, rank2)
            count1 = jnp.where(b[r] >= cut, r + 1.0, count1)
        r2_ref[h] = _pack_rows(rank2)
        e2_ref[h] = _pack_rows(jnp.exp(s2 - b[0]))
        n1_ref[h] = _pack_dup(count1)
        w_ref[h] = _pack_dup(jnp.exp(s1 - a[0]) / z)


def _router(xn, wq_hi, wq_lo, subkeys, tt=256):
    t, d = xn.shape
    nq = wq_hi.shape[1]
    shape = (PEER_HEADS, PEER_NKEYS, t)
    ospec = pl.BlockSpec((PEER_HEADS, PEER_NKEYS, tt), lambda i: (0, 0, i))
    pspec = pl.BlockSpec((PEER_HEADS, PEER_NKEYS // 2, tt), lambda i: (0, 0, i))
    packed = jax.ShapeDtypeStruct((PEER_HEADS, PEER_NKEYS // 2, t), jnp.uint32)
    return pl.pallas_call(
        _router_kernel,
        grid=(t // tt,),
        in_specs=[pl.BlockSpec((tt, d), lambda i: (i, 0)), _resident((d, nq)), _resident((d, nq)),
                  _resident(subkeys.shape)],
        out_specs=[pspec, pspec, ospec, ospec],
        out_shape=[packed, packed, jax.ShapeDtypeStruct(shape, jnp.uint32),
                   jax.ShapeDtypeStruct(shape, jnp.uint32)],
        scratch_shapes=[pltpu.VMEM((tt, nq), F32), pltpu.VMEM((_NCAND, tt), F32)],
        compiler_params=_params("parallel"),
        name="peer_router",
    )(xn, wq_hi, wq_lo, subkeys)


def _gelu(x):
    return 0.5 * x * (1.0 + lax.erf(x * (2.0 ** -0.5)))


_BF16_ROWS = 16
_PEER_SUB = 256
_SUBLANES = 8


def _peer_kernel(x_ref, u_ref, v_ref, r2_ref, e2_ref, n1_ref, w_ref, y_ref, acc_ref, h_ref, act_ref):
    j = pl.program_id(1)
    te, tt = h_ref.shape
    nk = PEER_NKEYS

    @pl.when(j == 0)
    def _():
        acc_ref[...] = jnp.zeros_like(acc_ref)

    xb = pltpu.bitcast(x_ref[...], BF16)
    for k in range(te // _PEER_SUB):
        uk = pltpu.bitcast(u_ref[pl.ds(k * _PEER_SUB // 2, _PEER_SUB // 2), :], BF16)
        h_ref[pl.ds(k * _PEER_SUB, _PEER_SUB), :] = _dot_nt(uk, xb)
    n_i1 = te // nk
    for c in range(tt // 128):
        cols = pl.ds(c * 128, 128)
        for v in range(nk // _BF16_ROWS):
            prow = pl.ds(v * _SUBLANES, _SUBLANES)
            g = [jnp.zeros((_BF16_ROWS, 128), BF16) for _ in range(n_i1)]
            for h in range(PEER_HEADS):
                r2 = pltpu.bitcast(r2_ref[h, prow, cols], BF16)
                e2 = pltpu.bitcast(e2_ref[h, prow, cols], BF16)
                for ii in range(n_i1):
                    row = pl.ds(ii, 1)
                    n_row = pltpu.bitcast(jnp.broadcast_to(n1_ref[h, row, cols], (_SUBLANES, 128)), BF16)
                    w_row = pltpu.bitcast(jnp.broadcast_to(w_ref[h, row, cols], (_SUBLANES, 128)), BF16)
                    g[ii] = g[ii] + jnp.where(r2 < n_row, e2, 0.0) * w_row
            for ii in range(n_i1):
                rows = pl.ds(ii * nk + v * _BF16_ROWS, _BF16_ROWS)
                act_ref[rows, cols] = _gelu(h_ref[rows, cols]).astype(BF16) * g[ii]
    acc_ref[...] += lax.dot_general(pltpu.bitcast(v_ref[...], BF16), act_ref[...], (((0,), (0,)), ((), ())),
                                    preferred_element_type=F32)

    @pl.when(j == pl.num_programs(1) - 1)
    def _():
        y_ref[...] = acc_ref[...].T


def _peer_experts(xb, u, v, r2, e2, n1, w, tt=512, te=2048):
    t, d = 2 * xb.shape[0], xb.shape[1]
    ne = 2 * u.shape[0]
    rspec = pl.BlockSpec((PEER_HEADS, te // PEER_NKEYS, tt), lambda i, j: (0, j, i))
    pspec = pl.BlockSpec((PEER_HEADS, PEER_NKEYS // 2, tt), lambda i, j: (0, 0, i))
    return pl.pallas_call(
        _peer_kernel,
        grid=(t // tt, ne // te),
        in_specs=[pl.BlockSpec((tt // 2, d), lambda i, j: (i, 0)),
                  pl.BlockSpec((te // 2, d), lambda i, j: (j, 0)),
                  pl.BlockSpec((te // 2, d), lambda i, j: (j, 0)),
                  pspec, pspec, rspec, rspec],
        out_specs=pl.BlockSpec((tt, d), lambda i, j: (i, 0)),
        out_shape=jax.ShapeDtypeStruct((t, d), F32),
        scratch_shapes=[pltpu.VMEM((d, tt), F32), pltpu.VMEM((te, tt), F32), pltpu.VMEM((te, tt), BF16)],
        compiler_params=_params("parallel", "arbitrary"),
        name="peer_experts",
    )(xb, u, v, r2, e2, n1, w)


def _ln_ple_kernel(x_ref, y_ref, g_ref, b_ref, p_ref, wp_ref, wg_ref, bg_ref, o_ref, *, alpha):
    xn = _layer_norm(alpha * x_ref[...] + y_ref[...], g_ref[...], b_ref[...])
    gate = jax.nn.sigmoid(_dot(xn.astype(BF16), wg_ref[...]) + bg_ref[...])
    o_ref[...] = xn + gate * _dot(p_ref[...].astype(BF16), wp_ref[...])


def _ln_ple(x, y, g, b, p, layer, wp, wg, bg, alpha, tm=512):
    t, d = x.shape
    dp = p.shape[2]
    tile = pl.BlockSpec((tm, d), lambda i: (i, 0))
    vec = _resident((1, d))
    return pl.pallas_call(
        functools.partial(_ln_ple_kernel, alpha=alpha),
        grid=(t // tm,),
        in_specs=[tile, tile, vec, vec, pl.BlockSpec((None, tm, dp), lambda i: (layer, i, 0)), _resident((dp, d)),
                  _resident((d, d)), vec],
        out_specs=tile,
        out_shape=jax.ShapeDtypeStruct((t, d), F32),
        compiler_params=_params("parallel"),
        name="ln_ple",
    )(x, y, g.reshape(1, d), b.reshape(1, d), p, wp, wg, bg.reshape(1, d))


def _mixer_a(x, w_qkv, sinks):
    d = N_HEADS * HEAD_DIM
    kvw = A_KV_HEADS * HEAD_DIM
    wq, wk, wv = w_qkv[:, :d], w_qkv[:, d:d + kvw], w_qkv[:, d + kvw:]
    qt, k, vt = _proj_t(x, wq.T.astype(BF16), wk.astype(BF16), wv.T.astype(BF16))
    o, = _band_attention(qt, k, vt, kv_group=N_HEADS // A_KV_HEADS, dist_scale=1.0,
                         max_dist=A_WINDOW - 1, sinks=sinks)
    return [o], []


def _mixer_b(x, w_qkv):
    d = N_HEADS * HEAD_DIM
    ng = len(B_PATTERNS)
    wk = w_qkv[:, ng * d:(ng + 1) * d].astype(BF16)
    wvt = w_qkv[:, (ng + 1) * d:].T.astype(BF16)
    os_, lses = [], []
    for gi, (w, r) in enumerate(B_PATTERNS):
        qt, k, vt = _proj_t(x, w_qkv[:, gi * d:(gi + 1) * d].T.astype(BF16), wk, wvt, dil=r)
        o, lse = _band_attention(qt, k, vt, kv_group=1, dist_scale=float(r), max_dist=w // r, want_lse=True)
        os_.append(o)
        lses.append(lse)
    return os_, lses


def _mixer_c(x, w_qkv):
    d = N_HEADS * HEAD_DIM
    wq, wk, wv = w_qkv[:, :d], w_qkv[:, d:2 * d], w_qkv[:, 2 * d:]
    qt, k, vt, kmean = _proj_c(x, wq.T.astype(BF16), wk.astype(BF16), wv.T.astype(BF16))
    return [_moba_attention(qt, k, vt, kmean.reshape(-1, d))], []


def kernel(x, p, a_w_qkv, a_sinks, a_w_o, b_w_qkv, b_w_o, c_w_qkv, c_w_o, ln1_g, ln1_b, ln2_g, ln2_b,
           peer_w_q, peer_subkeys, peer_u, peer_v, ple_w, ple_gate_w, ple_gate_b):
    depth = p.shape[0]
    alpha = (2 * depth) ** 0.25
    bsz, seq, d = x.shape
    assert bsz == 1 and seq % (B_PATTERNS[-1][1] * BAND) == 0
    xt = x.reshape(seq, d)
    for i in range(depth):
        kind, j = i % 3, i // 3
        if kind == 0:
            os_, lses = _mixer_a(xt, a_w_qkv[j], a_sinks[j])
            wo = a_w_o[j]
        elif kind == 1:
            os_, lses = _mixer_b(xt, b_w_qkv[j])
            wo = b_w_o[j]
        else:
            os_, lses = _mixer_c(xt, c_w_qkv[j])
            wo = c_w_o[j]
        x1, x1b = _outproj_ln(os_, lses, wo.astype(BF16), xt, ln1_g[i], ln1_b[i], alpha)
        wq_hi, wq_lo = _split(peer_w_q[i])
        routing = _router(x1, wq_hi, wq_lo, peer_subkeys[i])
        y = _peer_experts(x1b, _pack_weight(peer_u, i), _pack_weight(peer_v, i), *routing)
        xt = _ln_ple(x1, y, ln2_g[i], ln2_b[i], p.reshape(depth, seq, -1), i, ple_w[i].astype(BF16),
                     ple_gate_w[i].astype(BF16), ple_gate_b[i], alpha)
    return xt.reshape(bsz, seq, d)
```

```python
import functools

import jax
import jax.numpy as jnp
import ml_dtypes
import numpy as np
from jax import lax
from jax.experimental import pallas as pl
from jax.experimental.pallas import tpu as pltpu

F32 = jnp.float32
BF16 = jnp.bfloat16
NEG_INF = float("-inf")
BIG = 1e30
LOG2E = 1.4426950408889634

D_MODEL = 1024
HEAD_DIM = 64
N_HEADS = 16
BAND = 128
A_KV_HEADS = 2
A_WINDOW = 128
B_PATTERNS = ((128, 1), (512, 4), (2048, 16))
C_BLOCK = 256
C_TOPK = 3
PEER_HEADS = 8
PEER_NKEYS = 128
PEER_TOPK = 16
LN_EPS = 1e-5
VMEM_LIMIT = 56 * 1024 * 1024
_SUBLANES = 8


def _params(*sem):
    return pltpu.CompilerParams(dimension_semantics=sem, vmem_limit_bytes=VMEM_LIMIT)


def _dot(a, b):
    return jnp.dot(a, b, preferred_element_type=F32)


def _dot_nt(a, b):
    return lax.dot_general(a, b, (((1,), (1,)), ((), ())), preferred_element_type=F32)


def _split(a):
    hi = a.astype(BF16)
    lo = (a - hi.astype(F32)).astype(BF16)
    return hi, lo


def _alibi_slope(h, n):
    return 2.0 ** (-8.0 * (h + 1) / n)


def _resident(shape):
    zeros = (0,) * len(shape)
    return pl.BlockSpec(shape, lambda *_: zeros)


def _proj_t_kernel(x_ref, wqt_ref, wk_ref, wvt_ref, qt_ref, k_ref, vt_ref, *, q_scale):
    xb = x_ref[...].astype(BF16)
    qt_ref[...] = _pack_rows(_dot_nt(wqt_ref[...], xb) * q_scale)
    k_ref[...] = _pack_rows(_dot(xb, wk_ref[...]))
    vt_ref[...] = _pack_rows(_dot_nt(wvt_ref[...], xb))


def _proj_t(x, wqt, wk, wvt, dil=1, tm=512):
    t, kdim = x.shape
    d, kw = wqt.shape[0], wk.shape[1]
    l = t // dil
    tm = min(tm, l)
    u32 = jnp.uint32
    return pl.pallas_call(
        functools.partial(_proj_t_kernel, q_scale=LOG2E * HEAD_DIM ** -0.5),
        grid=(dil, l // tm),
        in_specs=[pl.BlockSpec((tm, kdim), lambda c, i: (i, c)), _resident((d, kdim)), _resident((kdim, kw)),
                  _resident((kw, kdim))],
        out_specs=[pl.BlockSpec((None, d // 2, tm), lambda c, i: (c, 0, i)),
                   pl.BlockSpec((None, tm // 2, kw), lambda c, i: (c, i, 0)),
                   pl.BlockSpec((None, kw // 2, tm), lambda c, i: (c, 0, i))],
        out_shape=[jax.ShapeDtypeStruct((dil, d // 2, l), u32), jax.ShapeDtypeStruct((dil, l // 2, kw), u32),
                   jax.ShapeDtypeStruct((dil, kw // 2, l), u32)],
        compiler_params=_params("parallel", "parallel"),
        name="proj_band",
    )(x.reshape(l, dil * kdim), wqt, wk, wvt)


def _proj_c_kernel(x_ref, wqt_ref, wk_ref, wvt_ref, qt_ref, k_ref, vt_ref, km_ref, *, nblk):
    xb = x_ref[...].astype(BF16)
    qt_ref[...] = _dot_nt(wqt_ref[...], xb)
    kf = _dot(xb, wk_ref[...])
    k_ref[...] = kf.astype(BF16)
    vt_ref[...] = _dot_nt(wvt_ref[...], xb).astype(BF16)
    for r in range(nblk):
        km_ref[r] = jnp.mean(kf[r * C_BLOCK:(r + 1) * C_BLOCK], axis=0, keepdims=True)


def _proj_c(x, wqt, wk, wvt, tm=512):
    t, k = x.shape
    d = wk.shape[1]
    nblk = tm // C_BLOCK
    return pl.pallas_call(
        functools.partial(_proj_c_kernel, nblk=nblk),
        grid=(t // tm,),
        in_specs=[pl.BlockSpec((tm, k), lambda i: (i, 0)), _resident((d, k)), _resident((k, d)),
                  _resident((d, k))],
        out_specs=[pl.BlockSpec((d, tm), lambda i: (0, i)), pl.BlockSpec((tm, d), lambda i: (i, 0)),
                   pl.BlockSpec((d, tm), lambda i: (0, i)), pl.BlockSpec((nblk, 1, d), lambda i: (i, 0, 0))],
        out_shape=[jax.ShapeDtypeStruct((d, t), F32), jax.ShapeDtypeStruct((t, d), BF16),
                   jax.ShapeDtypeStruct((d, t), BF16), jax.ShapeDtypeStruct((t // C_BLOCK, 1, d), F32)],
        compiler_params=_params("parallel"),
        name="proj_moba",
    )(x, wqt, wk, wvt)


_N_AUG = 16
_PAIR = 2 * HEAD_DIM


def _split3_np(a):
    bf = ml_dtypes.bfloat16
    h = a.astype(bf).astype(np.float32)
    m = (a - h).astype(bf).astype(np.float32)
    l = (a - h - m).astype(bf).astype(np.float32)
    return h, m, l


def _pack_rows_np(a):
    bits = a.astype(ml_dtypes.bfloat16).view(np.uint16).astype(np.uint32)
    return bits[0::2] | (bits[1::2] << 16)


def _band_tables(dist_scale):
    ki = np.arange(2 * BAND, dtype=np.float32)
    kaug = np.zeros((2 * BAND, _PAIR), np.float32)
    kaug[:, 0:3] = 1.0
    kaug[:, 3:6] = ki[:, None]
    qaug = np.zeros((N_HEADS, _N_AUG, BAND), np.float32)
    qpos = np.arange(BAND, dtype=np.float32) + BAND
    for h in range(N_HEADS):
        c = np.float32(_alibi_slope(h, N_HEADS) * dist_scale * LOG2E)
        qaug[h, 0:3] = np.stack(_split3_np(-c * qpos))
        qaug[h, 3:6] = np.stack(_split3_np(np.full((BAND,), c, np.float32)))
    return _pack_rows_np(kaug), np.stack([_pack_rows_np(qaug[h]) for h in range(N_HEADS)])


def _band_kernel(*refs, kv_group, max_dist, use_sinks, want_lse):
    q_ref, kp_ref, ko_ref, vp_ref, vo_ref, kaug_ref, qaug_ref = refs[:7]
    pos = 7
    sink_ref = None
    if use_sinks:
        sink_ref = refs[pos]
        pos += 1
    o_ref = refs[pos]
    lse_ref = refs[pos + 1] if want_lse else None

    b = pl.program_id(1)
    ki = lax.broadcasted_iota(jnp.int32, (2 * BAND, BAND), 0)
    qi = lax.broadcasted_iota(jnp.int32, (2 * BAND, BAND), 1)
    dist = qi + BAND - ki
    valid = (dist >= 0) & (dist <= max_dist) & (ki >= jnp.where(b > 0, 0, BAND))
    qt = pltpu.bitcast(q_ref[...], BF16)
    k_all = jnp.concatenate([pltpu.bitcast(kp_ref[...], BF16), pltpu.bitcast(ko_ref[...], BF16)], axis=0)
    vt_all = jnp.concatenate([pltpu.bitcast(vp_ref[...], BF16), pltpu.bitcast(vo_ref[...], BF16)], axis=1)
    kaug = pltpu.bitcast(kaug_ref[...], BF16)
    zslot = jnp.zeros((HEAD_DIM, BAND), BF16)
    zpad = jnp.zeros((_PAIR - _N_AUG, BAND), BF16)
    n_kv = N_HEADS // kv_group
    scores = [None] * N_HEADS
    for pair in range(-(-n_kv // 2)):
        heads = [h for h in range(N_HEADS) if (h // kv_group) // 2 == pair]
        qcats = []
        for h in heads:
            qh = qt[h * HEAD_DIM:(h + 1) * HEAD_DIM]
            slot = [qh, zslot] if (h // kv_group) % 2 == 0 else [zslot, qh]
            qcats.append(jnp.concatenate(slot + [pltpu.bitcast(qaug_ref[h], BF16), zpad], axis=0))
        kcat = jnp.concatenate([k_all[:, pair * _PAIR:(pair + 1) * _PAIR], kaug], axis=1)
        s_all = _dot(kcat, jnp.concatenate(qcats, axis=1))
        for n, h in enumerate(heads):
            scores[h] = s_all[:, n * BAND:(n + 1) * BAND]
    probs, dens, lses = [], [], []
    for h in range(N_HEADS):
        s = jnp.where(valid, scores[h], NEG_INF)
        m = jnp.max(s, axis=0, keepdims=True)
        if use_sinks:
            sk = sink_ref[:, h:h + 1] * LOG2E
            m = jnp.maximum(m, sk)
        p = jnp.exp2(s - m)
        den = jnp.sum(p, axis=0, keepdims=True)
        if use_sinks:
            den = den + jnp.exp2(sk - m)
        probs.append(p.astype(BF16))
        dens.append(den)
        if want_lse:
            lses.append(jnp.broadcast_to((m + jnp.log2(den)) * (1.0 / LOG2E), (HEAD_DIM, BAND)))
    outs = []
    for g in range(n_kv):
        hs = range(g * kv_group, (g + 1) * kv_group)
        pv = _dot(vt_all[g * HEAD_DIM:(g + 1) * HEAD_DIM], jnp.concatenate([probs[h] for h in hs], axis=1))
        outs += [pv[:, n * BAND:(n + 1) * BAND] / dens[h] for n, h in enumerate(hs)]
    o_ref[...] = jnp.concatenate(outs, axis=0).T.astype(o_ref.dtype)
    if want_lse:
        lse_ref[...] = jnp.concatenate(lses, axis=0).T


def _band_attention(qt, k, vt, *, kv_group, dist_scale, max_dist, sinks=None, want_lse=False):
    r, _, l = qt.shape
    d = N_HEADS * HEAD_DIM
    kw = k.shape[2]
    prev = lambda c, b: jnp.maximum(b - 1, 0)
    kaug, qaug = _band_tables(dist_scale)
    in_specs = [pl.BlockSpec((None, d // 2, BAND), lambda c, b: (c, 0, b)),
                pl.BlockSpec((None, BAND // 2, kw), lambda c, b: (c, prev(c, b), 0)),
                pl.BlockSpec((None, BAND // 2, kw), lambda c, b: (c, b, 0)),
                pl.BlockSpec((None, kw // 2, BAND), lambda c, b: (c, 0, prev(c, b))),
                pl.BlockSpec((None, kw // 2, BAND), lambda c, b: (c, 0, b)),
                _resident(kaug.shape), _resident(qaug.shape)]
    args = [qt, k, k, vt, vt, jnp.asarray(kaug), jnp.asarray(qaug)]
    if sinks is not None:
        in_specs.append(_resident((1, N_HEADS)))
        args.append(sinks.reshape(1, N_HEADS).astype(F32))
    o_spec = pl.BlockSpec((BAND, d), lambda c, b: (b, c))
    out_specs = [o_spec]
    out_shape = [jax.ShapeDtypeStruct((l, r * d), BF16)]
    if want_lse:
        out_specs.append(o_spec)
        out_shape.append(jax.ShapeDtypeStruct((l, r * d), F32))
    outs = pl.pallas_call(
        functools.partial(_band_kernel, kv_group=kv_group, max_dist=max_dist,
                          use_sinks=sinks is not None, want_lse=want_lse),
        grid=(r, l // BAND),
        in_specs=in_specs, out_specs=out_specs, out_shape=out_shape,
        compiler_params=_params("parallel", "parallel"),
        name="band_attention",
    )(*args)
    return [o.reshape(l * r, d) for o in outs]


def _dot3(a, b):
    ah, al = _split(a)
    bh, bl = _split(b)
    return _dot(ah, bh) + (_dot(ah, bl) + _dot(al, bh))


def _dot3_nt(a, b):
    ah, al = _split(a)
    bh, bl = _split(b)
    return _dot_nt(ah, bh) + (_dot_nt(ah, bl) + _dot_nt(al, bh))


def _split3(a):
    h = a.astype(BF16).astype(F32)
    r = a - h
    m = r.astype(BF16).astype(F32)
    l = (r - m).astype(BF16).astype(F32)
    return h, m, l


_KV_STEP = 2 * C_BLOCK


def _moba_kernel(slope_ref, qt_ref, k_ref, vt_ref, km_ref, o_ref, sel_ref, s_ref, p_ref, *, nblk):
    hp = pl.program_id(0)
    qi = pl.program_id(1)
    tq = C_BLOCK
    pw = 2 * HEAD_DIM
    qt = qt_ref[...]
    rows = lax.broadcasted_iota(jnp.int32, (pw, tq), 0)
    blk = lax.broadcasted_iota(jnp.int32, (nblk, tq), 0)
    past = blk < qi
    aug_r = lax.broadcasted_iota(jnp.int32, (_N_AUG, tq), 0)
    tqf = lax.broadcasted_iota(jnp.int32, (_N_AUG, tq), 1).astype(F32)
    kcol = lax.broadcasted_iota(jnp.int32, (_KV_STEP, pw), 1)
    tkf = lax.broadcasted_iota(jnp.int32, (_KV_STEP, pw), 0).astype(F32)
    causal = (lax.broadcasted_iota(jnp.int32, (C_BLOCK, tq), 1)
              >= lax.broadcasted_iota(jnp.int32, (C_BLOCK, tq), 0))
    zpad = jnp.zeros((pw - _N_AUG, tq), BF16)
    k0 = pl.multiple_of(qi * C_BLOCK, C_BLOCK)
    k_own = k_ref[pl.ds(k0, C_BLOCK), :]

    qcat, kpos, c2, init = [], [], [], []
    for hh in range(2):
        c = slope_ref[2 * hp + hh] * LOG2E
        qm = jnp.where((rows >= hh * HEAD_DIM) & (rows < (hh + 1) * HEAD_DIM), qt, 0.0)
        gate = jnp.where(past, _dot3(km_ref[...], qm), NEG_INF)
        g = gate
        for r in range(C_TOPK):
            thr = jnp.max(g, axis=0, keepdims=True)
            if r < C_TOPK - 1:
                g = jnp.where(g >= thr, NEG_INF, g)
        sel_ref[hh] = jnp.where(past & (gate >= thr), 1.0, 0.0)

        qh, qmid, ql = _split3(-c * tqf)
        qpos = jnp.where(aug_r == 0, qh, jnp.where(aug_r == 1, qmid, jnp.where(aug_r == 2, ql,
                         jnp.where(aug_r < 6, 1.0, 0.0))))
        qb = (qm * (LOG2E * HEAD_DIM ** -0.5)).astype(BF16)
        qcat.append(jnp.concatenate([qb, qpos.astype(BF16), zpad], axis=0))
        kh, kmid, kl = _split3(c * tkf)
        kpos.append(jnp.where(kcol < 3, 1.0, jnp.where(kcol == 3, kh, jnp.where(kcol == 4, kmid,
                              jnp.where(kcol == 5, kl, 0.0)))).astype(BF16))
        c2.append(c)

        s = _dot(jnp.concatenate([k_own, kpos[hh][:C_BLOCK]], axis=1), qcat[hh])
        s = jnp.where(causal, s, NEG_INF)
        m0 = jnp.max(s, axis=0, keepdims=True)
        p = jnp.exp2(s - m0)
        l0 = jnp.sum(p, axis=0, keepdims=True)
        acc0 = _dot(vt_ref[pl.ds(hh * HEAD_DIM, HEAD_DIM), pl.ds(k0, C_BLOCK)], p.astype(BF16))
        init.append((m0, l0, acc0))

    nsub = _KV_STEP // C_BLOCK

    nsteps = (qi + nsub - 1) // nsub
    last_step = k_ref.shape[0] // _KV_STEP - 1

    def key_start(n):
        return pl.multiple_of(jnp.clip(n, 0, last_step) * _KV_STEP, _KV_STEP)

    def scores(n, slot):
        kb = k_ref[pl.ds(key_start(n), _KV_STEP), :]
        for hh in range(2):
            s_ref[slot, hh] = _dot(jnp.concatenate([kb, kpos[hh]], axis=1), qcat[hh])

    def pv(n, hh, slot):
        return _dot(vt_ref[pl.ds(hh * HEAD_DIM, HEAD_DIM), pl.ds(key_start(n), _KV_STEP)],
                    p_ref[slot, hh])

    def step(n, slot, carry):
        a_prev, state = carry
        off = (qi * C_BLOCK - n * _KV_STEP).astype(F32)
        scores(n + 1, 1 - slot)
        new_a, new_state = [], []
        for hh in range(2):
            m_i, l_i, acc = state[hh]
            acc = a_prev[hh] * acc + pv(n - 1, hh, 1 - slot)
            shift = c2[hh] * off
            sel = [sel_ref[hh, pl.ds(nsub * n + j, 1), :] > 0.5 for j in range(nsub)]
            m_new = m_i
            for j in range(nsub):
                mj = jnp.max(s_ref[slot, hh, j * C_BLOCK:(j + 1) * C_BLOCK, :], axis=0, keepdims=True)
                m_new = jnp.maximum(m_new, jnp.where(sel[j], mj - shift, NEG_INF))
            l_new = jnp.exp2(m_i - m_new) * l_i
            for j in range(nsub):
                rows = slice(j * C_BLOCK, (j + 1) * C_BLOCK)
                p = jnp.exp2(s_ref[slot, hh, rows, :] - jnp.where(sel[j], m_new + shift, BIG))
                p_ref[slot, hh, rows, :] = p.astype(BF16)
                l_new = l_new + jnp.sum(p, axis=0, keepdims=True)
            new_a.append(jnp.exp2(m_i - m_new))
            new_state.append((m_new, l_new, acc))
        return tuple(new_a), tuple(new_state)

    def body(n2, carry):
        return step(2 * n2 + 1, 1, step(2 * n2, 0, carry))

    p_ref[1] = jnp.zeros_like(p_ref[1])
    scores(0, 0)
    one_a = jnp.ones((1, tq), F32)
    npairs = (nsteps + 1) // 2
    a_prev, state = lax.fori_loop(0, npairs, body, ((one_a, one_a), tuple(init)))
    outs = []
    for hh in range(2):
        _, l, acc = state[hh]
        outs.append((a_prev[hh] * acc + pv(2 * npairs - 1, hh, 1)) / l)
    o_ref[...] = jnp.concatenate(outs, axis=0).T.astype(o_ref.dtype)


def _moba_attention(qt, k, vt, kmean):
    d, t = qt.shape
    nblk = t // C_BLOCK
    slopes = jnp.asarray([_alibi_slope(h, N_HEADS) for h in range(N_HEADS)], F32)
    pw = 2 * HEAD_DIM
    assert t % (2 * _KV_STEP) == 0
    return pl.pallas_call(
        functools.partial(_moba_kernel, nblk=nblk),
        grid=(d // pw, nblk),
        in_specs=[pl.BlockSpec(memory_space=pltpu.SMEM),
                  pl.BlockSpec((pw, C_BLOCK), lambda hp, i: (hp, i)),
                  pl.BlockSpec((t, pw), lambda hp, i: (0, hp)),
                  pl.BlockSpec((pw, t), lambda hp, i: (hp, 0)),
                  pl.BlockSpec((nblk, pw), lambda hp, i: (0, hp))],
        out_specs=pl.BlockSpec((C_BLOCK, pw), lambda hp, i: (i, hp)),
        out_shape=jax.ShapeDtypeStruct((t, d), BF16),
        scratch_shapes=[pltpu.VMEM((2, nblk, C_BLOCK), F32), pltpu.VMEM((2, 2, _KV_STEP, C_BLOCK), F32),
                        pltpu.VMEM((2, 2, _KV_STEP, C_BLOCK), BF16)],
        compiler_params=_params("parallel", "arbitrary"),
        name="moba_attention",
    )(slopes, qt, k, vt, kmean)


def _layer_norm(z, g, b):
    mu = jnp.mean(z, axis=-1, keepdims=True)
    zc = z - mu
    var = jnp.mean(zc * zc, axis=-1, keepdims=True)
    return zc * lax.rsqrt(var + LN_EPS) * g + b


def _outproj_ln_kernel(*refs, n_groups, alpha):
    o_refs = refs[:n_groups]
    lse_refs = refs[n_groups:2 * n_groups] if n_groups > 1 else ()
    pos = n_groups + len(lse_refs)
    wo_ref, x_ref, g_ref, b_ref, xn_ref, xb_ref = refs[pos:pos + 6]
    if n_groups == 1:
        o = o_refs[0][...]
    else:
        lses = [r[...] for r in lse_refs]
        m = functools.reduce(jnp.maximum, lses)
        es = [jnp.exp(l - m) for l in lses]
        den = functools.reduce(jnp.add, es)
        o = functools.reduce(jnp.add, [(e / den) * r[...].astype(F32) for e, r in zip(es, o_refs)])
        o = o.astype(BF16)
    y = _dot(o, wo_ref[...])
    xn = _layer_norm(alpha * x_ref[...] + y, g_ref[...], b_ref[...])
    xn_ref[...] = xn
    xb_ref[...] = _pack_rows(xn)


def _outproj_ln(os_, lses, wo, x, g, b, alpha, tm=512):
    t, d = x.shape
    n = len(os_)
    tile = pl.BlockSpec((tm, d), lambda i: (i, 0))
    return pl.pallas_call(
        functools.partial(_outproj_ln_kernel, n_groups=n, alpha=alpha),
        grid=(t // tm,),
        in_specs=[tile] * (n + len(lses)) + [_resident((d, d)), tile, _resident((1, d)), _resident((1, d))],
        out_specs=[tile, pl.BlockSpec((tm // 2, d), lambda i: (i, 0))],
        out_shape=[jax.ShapeDtypeStruct((t, d), F32), jax.ShapeDtypeStruct((t // 2, d), jnp.uint32)],
        compiler_params=_params("parallel"),
        name="outproj_ln",
    )(*os_, *lses, wo, x, g.reshape(1, d), b.reshape(1, d))


def _pack_rows(a):
    return pltpu.bitcast(a.astype(BF16), jnp.uint32)


def _pack_dup(a):
    bits = pltpu.bitcast(a.astype(BF16).astype(F32), jnp.uint32)
    return bits | (bits >> 16)


def _pack_weight_kernel(w_ref, o_ref):
    o_ref[...] = _pack_rows(w_ref[...])


def _pack_weight(w, layer, tr=2048):
    _, r, c = w.shape
    return pl.pallas_call(
        _pack_weight_kernel,
        grid=(r // tr,),
        in_specs=[pl.BlockSpec((None, tr, c), lambda i: (layer, i, 0))],
        out_specs=pl.BlockSpec((tr // 2, c), lambda i: (i, 0)),
        out_shape=jax.ShapeDtypeStruct((r // 2, c), jnp.uint32),
        compiler_params=_params("parallel"),
        name="pack_weight",
    )(w)


def _oddeven_merge(lo, hi, r):
    step = r * 2
    if step < hi - lo:
        yield from _oddeven_merge(lo, hi, step)
        yield from _oddeven_merge(lo + r, hi, step)
        yield from [(i, i + r) for i in range(lo + r, hi - r, step)]
    else:
        yield (lo, lo + r)


def _oddeven_sort(lo, hi):
    if hi > lo:
        mid = lo + (hi - lo) // 2
        yield from _oddeven_sort(lo, mid)
        yield from _oddeven_sort(mid + 1, hi)
        yield from _oddeven_merge(lo, hi, 1)


_SORT16 = tuple(_oddeven_sort(0, PEER_TOPK - 1))


def _top16_sorted(s):
    n = PEER_TOPK
    x = [s[_SUBLANES * j:_SUBLANES * (j + 1)] for j in range(n)]

    def cmpx(i, j):
        x[i], x[j] = jnp.maximum(x[i], x[j]), jnp.minimum(x[i], x[j])

    for i, j in _SORT16:
        cmpx(i, j)
    for shift in (4, 2, 1):
        y = [pltpu.roll(v, shift, 0) for v in x]
        x = [jnp.maximum(x[i], y[n - 1 - i]) for i in range(n)]
        stride = n // 2
        while stride:
            for i in range(n):
                if not i & stride:
                    cmpx(i, i + stride)
            stride //= 2
    top = [v[0:1] for v in x]
    nxt = jnp.max(jnp.where(s < top[n - 1], s, NEG_INF), axis=0, keepdims=True)
    return top + [nxt]


def _top_vals(s, k):
    out = []
    for r in range(k):
        m = jnp.max(s, axis=0, keepdims=True)
        out.append(m)
        if r < k - 1:
            s = jnp.where(s >= m, NEG_INF, s)
    return out


_CAND = [(i, j) for i in range(PEER_TOPK + 1) for j in range(PEER_TOPK + 1)
         if (i + 1) * (j + 1) <= PEER_TOPK + 1]
_NCAND = -(-len(_CAND) // 8) * 8


def _router_kernel(x_ref, wh_ref, wl_ref, sk_ref, r2_ref, e2_ref, n1_ref, w_ref, q_ref, cand_ref):
    xh, xl = _split(x_ref[...])
    q_ref[...] = _dot(xh, wh_ref[...]) + (_dot(xh, wl_ref[...]) + _dot(xl, wh_ref[...]))
    tt = x_ref.shape[0]
    cand_ref[...] = jnp.full((_NCAND, tt), NEG_INF, F32)
    nk = PEER_NKEYS
    for h in range(PEER_HEADS):
        s1 = _dot3_nt(sk_ref[0], q_ref[:, (2 * h) * nk:(2 * h + 1) * nk])
        s2 = _dot3_nt(sk_ref[1], q_ref[:, (2 * h + 1) * nk:(2 * h + 2) * nk])
        a = _top16_sorted(s1)
        b = _top16_sorted(s2)
        for r, (i, j) in enumerate(_CAND):
            cand_ref[r:r + 1, :] = a[i] + b[j]
        v = _top_vals(cand_ref[...], PEER_TOPK + 1)
        thr = 0.5 * (v[PEER_TOPK - 1] + v[PEER_TOPK])
        z = functools.reduce(jnp.add, [jnp.exp(v[r] - v[0]) for r in range(PEER_TOPK)])
        cut = thr - s1
        rank2 = jnp.zeros_like(s2)
        count1 = jnp.zeros_like(s1)
        for r in range(PEER_TOPK):
            rank2 = jnp.where(s2 < b[r], r + 1.0, rank2)
            count1 = jnp.where(b[r] >= cut, r + 1.0, count1)
        r2_ref[h] = _pack_rows(rank2)
        e2_ref[h] = _pack_rows(jnp.exp(s2 - b[0]))
        n1_ref[h] = _pack_dup(count1)
        w_ref[h] = _pack_dup(jnp.exp(s1 - a[0]) / z)


def _router(xn, wq_hi, wq_lo, subkeys, tt=256):
    t, d = xn.shape
    nq = wq_hi.shape[1]
    shape = (PEER_HEADS, PEER_NKEYS, t)
    ospec = pl.BlockSpec((PEER_HEADS, PEER_NKEYS, tt), lambda i: (0, 0, i))
    pspec = pl.BlockSpec((PEER_HEADS, PEER_NKEYS // 2, tt), lambda i: (0, 0, i))
    packed = jax.ShapeDtypeStruct((PEER_HEADS, PEER_NKEYS // 2, t), jnp.uint32)
    return pl.pallas_call(
        _router_kernel,
        grid=(t // tt,),
        in_specs=[pl.BlockSpec((tt, d), lambda i: (i, 0)), _resident((d, nq)), _resident((d, nq)),
                  _resident(subkeys.shape)],
        out_specs=[pspec, pspec, ospec, ospec],
        out_shape=[packed, packed, jax.ShapeDtypeStruct(shape, jnp.uint32),
                   jax.ShapeDtypeStruct(shape, jnp.uint32)],
        scratch_shapes=[pltpu.VMEM((tt, nq), F32), pltpu.VMEM((_NCAND, tt), F32)],
        compiler_params=_params("parallel"),
        name="peer_router",
    )(xn, wq_hi, wq_lo, subkeys)


def _gelu(x):
    return 0.5 * x * (1.0 + lax.erf(x * (2.0 ** -0.5)))


_BF16_ROWS = 16
_PEER_SUB = 256


def _peer_kernel(x_ref, u_ref, v_ref, r2_ref, e2_ref, n1_ref, w_ref, y_ref, acc_ref, h_ref, act_ref):
    j = pl.program_id(1)
    te, tt = h_ref.shape
    nk = PEER_NKEYS

    @pl.when(j == 0)
    def _():
        acc_ref[...] = jnp.zeros_like(acc_ref)

    xb = pltpu.bitcast(x_ref[...], BF16)
    for k in range(te // _PEER_SUB):
        uk = pltpu.bitcast(u_ref[pl.ds(k * _PEER_SUB // 2, _PEER_SUB // 2), :], BF16)
        h_ref[pl.ds(k * _PEER_SUB, _PEER_SUB), :] = _dot_nt(uk, xb)
    n_i1 = te // nk
    for c in range(tt // 128):
        cols = pl.ds(c * 128, 128)
        for v in range(nk // _BF16_ROWS):
            prow = pl.ds(v * _SUBLANES, _SUBLANES)
            g = [jnp.zeros((_BF16_ROWS, 128), BF16) for _ in range(n_i1)]
            for h in range(PEER_HEADS):
                r2 = pltpu.bitcast(r2_ref[h, prow, cols], BF16)
                e2 = pltpu.bitcast(e2_ref[h, prow, cols], BF16)
                for ii in range(n_i1):
                    row = pl.ds(ii, 1)
                    n_row = pltpu.bitcast(jnp.broadcast_to(n1_ref[h, row, cols], (_SUBLANES, 128)), BF16)
                    w_row = pltpu.bitcast(jnp.broadcast_to(w_ref[h, row, cols], (_SUBLANES, 128)), BF16)
                    g[ii] = g[ii] + jnp.where(r2 < n_row, e2, 0.0) * w_row
            for ii in range(n_i1):
                rows = pl.ds(ii * nk + v * _BF16_ROWS, _BF16_ROWS)
                act_ref[rows, cols] = _gelu(h_ref[rows, cols]).astype(BF16) * g[ii]
    acc_ref[...] += lax.dot_general(pltpu.bitcast(v_ref[...], BF16), act_ref[...], (((0,), (0,)), ((), ())),
                                    preferred_element_type=F32)

    @pl.when(j == pl.num_programs(1) - 1)
    def _():
        y_ref[...] = acc_ref[...].T


def _peer_experts(xb, u, v, r2, e2, n1, w, tt=512, te=2048):
    t, d = 2 * xb.shape[0], xb.shape[1]
    ne = 2 * u.shape[0]
    rspec = pl.BlockSpec((PEER_HEADS, te // PEER_NKEYS, tt), lambda i, j: (0, j, i))
    pspec = pl.BlockSpec((PEER_HEADS, PEER_NKEYS // 2, tt), lambda i, j: (0, 0, i))
    return pl.pallas_call(
        _peer_kernel,
        grid=(t // tt, ne // te),
        in_specs=[pl.BlockSpec((tt // 2, d), lambda i, j: (i, 0)),
                  pl.BlockSpec((te // 2, d), lambda i, j: (j, 0)),
                  pl.BlockSpec((te // 2, d), lambda i, j: (j, 0)),
                  pspec, pspec, rspec, rspec],
        out_specs=pl.BlockSpec((tt, d), lambda i, j: (i, 0)),
        out_shape=jax.ShapeDtypeStruct((t, d), F32),
        scratch_shapes=[pltpu.VMEM((d, tt), F32), pltpu.VMEM((te, tt), F32), pltpu.VMEM((te, tt), BF16)],
        compiler_params=_params("parallel", "arbitrary"),
        name="peer_experts",
    )(xb, u, v, r2, e2, n1, w)


def _ln_ple_kernel(x_ref, y_ref, g_ref, b_ref, p_ref, wp_ref, wg_ref, bg_ref, o_ref, *, alpha):
    xn = _layer_norm(alpha * x_ref[...] + y_ref[...], g_ref[...], b_ref[...])
    gate = jax.nn.sigmoid(_dot(xn.astype(BF16), wg_ref[...]) + bg_ref[...])
    o_ref[...] = xn + gate * _dot(p_ref[...].astype(BF16), wp_ref[...])


def _ln_ple(x, y, g, b, p, layer, wp, wg, bg, alpha, tm=512):
    t, d = x.shape
    dp = p.shape[2]
    tile = pl.BlockSpec((tm, d), lambda i: (i, 0))
    vec = _resident((1, d))
    return pl.pallas_call(
        functools.partial(_ln_ple_kernel, alpha=alpha),
        grid=(t // tm,),
        in_specs=[tile, tile, vec, vec, pl.BlockSpec((None, tm, dp), lambda i: (layer, i, 0)), _resident((dp, d)),
                  _resident((d, d)), vec],
        out_specs=tile,
        out_shape=jax.ShapeDtypeStruct((t, d), F32),
        compiler_params=_params("parallel"),
        name="ln_ple",
    )(x, y, g.reshape(1, d), b.reshape(1, d), p, wp, wg, bg.reshape(1, d))


def _mixer_a(x, w_qkv, sinks):
    d = N_HEADS * HEAD_DIM
    kvw = A_KV_HEADS * HEAD_DIM
    wq, wk, wv = w_qkv[:, :d], w_qkv[:, d:d + kvw], w_qkv[:, d + kvw:]
    qt, k, vt = _proj_t(x, wq.T.astype(BF16), wk.astype(BF16), wv.T.astype(BF16))
    o, = _band_attention(qt, k, vt, kv_group=N_HEADS // A_KV_HEADS, dist_scale=1.0,
                         max_dist=A_WINDOW - 1, sinks=sinks)
    return [o], []


def _mixer_b(x, w_qkv):
    d = N_HEADS * HEAD_DIM
    ng = len(B_PATTERNS)
    wk = w_qkv[:, ng * d:(ng + 1) * d].astype(BF16)
    wvt = w_qkv[:, (ng + 1) * d:].T.astype(BF16)
    os_, lses = [], []
    for gi, (w, r) in enumerate(B_PATTERNS):
        qt, k, vt = _proj_t(x, w_qkv[:, gi * d:(gi + 1) * d].T.astype(BF16), wk, wvt, dil=r)
        o, lse = _band_attention(qt, k, vt, kv_group=1, dist_scale=float(r), max_dist=w // r, want_lse=True)
        os_.append(o)
        lses.append(lse)
    return os_, lses


def _mixer_c(x, w_qkv):
    d = N_HEADS * HEAD_DIM
    wq, wk, wv = w_qkv[:, :d], w_qkv[:, d:2 * d], w_qkv[:, 2 * d:]
    qt, k, vt, kmean = _proj_c(x, wq.T.astype(BF16), wk.astype(BF16), wv.T.astype(BF16))
    return [_moba_attention(qt, k, vt, kmean.reshape(-1, d))], []


def kernel(x, p, a_w_qkv, a_sinks, a_w_o, b_w_qkv, b_w_o, c_w_qkv, c_w_o, ln1_g, ln1_b, ln2_g, ln2_b,
           peer_w_q, peer_subkeys, peer_u, peer_v, ple_w, ple_gate_w, ple_gate_b):
    depth = p.shape[0]
    alpha = (2 * depth) ** 0.25
    bsz, seq, d = x.shape
    assert bsz == 1 and seq % (B_PATTERNS[-1][1] * BAND) == 0
    xt = x.reshape(seq, d)
    for i in range(depth):
        kind, j = i % 3, i // 3
        if kind == 0:
            os_, lses = _mixer_a(xt, a_w_qkv[j], a_sinks[j])
            wo = a_w_o[j]
        elif kind == 1:
            os_, lses = _mixer_b(xt, b_w_qkv[j])
            wo = b_w_o[j]
        else:
            os_, lses = _mixer_c(xt, c_w_qkv[j])
            wo = c_w_o[j]
        x1, x1b = _outproj_ln(os_, lses, wo.astype(BF16), xt, ln1_g[i], ln1_b[i], alpha)
        wq_hi, wq_lo = _split(peer_w_q[i])
        routing = _router(x1, wq_hi, wq_lo, peer_subkeys[i])
        y = _peer_experts(x1b, _pack_weight(peer_u, i), _pack_weight(peer_v, i), *routing)
        xt = _ln_ple(x1, y, ln2_g[i], ln2_b[i], p.reshape(depth, seq, -1), i, ple_w[i].astype(BF16),
                     ple_gate_w[i].astype(BF16), ple_gate_b[i], alpha)
    return xt.reshape(bsz, seq, d)
```

```python
import functools

import jax
import jax.numpy as jnp
import ml_dtypes
import numpy as np
from jax import lax
from jax.experimental import pallas as pl
from jax.experimental.pallas import tpu as pltpu

F32 = jnp.float32
BF16 = jnp.bfloat16
NEG_INF = float("-inf")
BIG = 1e30
LOG2E = 1.4426950408889634

D_MODEL = 1024
HEAD_DIM = 64
N_HEADS = 16
BAND = 128
A_KV_HEADS = 2
A_WINDOW = 128
B_PATTERNS = ((128, 1), (512, 4), (2048, 16))
C_BLOCK = 256
C_TOPK = 3
PEER_HEADS = 8
PEER_NKEYS = 128
PEER_TOPK = 16
LN_EPS = 1e-5
VMEM_LIMIT = 56 * 1024 * 1024
_SUBLANES = 8


def _params(*sem):
    return pltpu.CompilerParams(dimension_semantics=sem, vmem_limit_bytes=VMEM_LIMIT)


def _dot(a, b):
    return jnp.dot(a, b, preferred_element_type=F32)


def _dot_nt(a, b):
    return lax.dot_general(a, b, (((1,), (1,)), ((), ())), preferred_element_type=F32)


def _split(a):
    hi = a.astype(BF16)
    lo = (a - hi.astype(F32)).astype(BF16)
    return hi, lo


def _alibi_slope(h, n):
    return 2.0 ** (-8.0 * (h + 1) / n)


def _resident(shape):
    zeros = (0,) * len(shape)
    return pl.BlockSpec(shape, lambda *_: zeros)


def _proj_t_kernel(x_ref, wqt_ref, wk_ref, wvt_ref, qt_ref, k_ref, vt_ref, *, q_scale):
    xb = x_ref[...].astype(BF16)
    qt_ref[...] = _pack_rows(_dot_nt(wqt_ref[...], xb) * q_scale)
    k_ref[...] = _pack_rows(_dot(xb, wk_ref[...]))
    vt_ref[...] = _pack_rows(_dot_nt(wvt_ref[...], xb))


def _proj_t(x, wqt, wk, wvt, dil=1, tm=512):
    t, kdim = x.shape
    d, kw = wqt.shape[0], wk.shape[1]
    l = t // dil
    tm = min(tm, l)
    u32 = jnp.uint32
    return pl.pallas_call(
        functools.partial(_proj_t_kernel, q_scale=LOG2E * HEAD_DIM ** -0.5),
        grid=(dil, l // tm),
        in_specs=[pl.BlockSpec((tm, kdim), lambda c, i: (i, c)), _resident((d, kdim)), _resident((kdim, kw)),
                  _resident((kw, kdim))],
        out_specs=[pl.BlockSpec((None, d // 2, tm), lambda c, i: (c, 0, i)),
                   pl.BlockSpec((None, tm // 2, kw), lambda c, i: (c, i, 0)),
                   pl.BlockSpec((None, kw // 2, tm), lambda c, i: (c, 0, i))],
        out_shape=[jax.ShapeDtypeStruct((dil, d // 2, l), u32), jax.ShapeDtypeStruct((dil, l // 2, kw), u32),
                   jax.ShapeDtypeStruct((dil, kw // 2, l), u32)],
        compiler_params=_params("parallel", "parallel"),
        name="proj_band",
    )(x.reshape(l, dil * kdim), wqt, wk, wvt)


def _proj_c_kernel(x_ref, wqt_ref, wk_ref, wvt_ref, qt_ref, k_ref, vt_ref, km_ref, *, nblk):
    xb = x_ref[...].astype(BF16)
    qt_ref[...] = _dot_nt(wqt_ref[...], xb)
    kf = _dot(xb, wk_ref[...])
    k_ref[...] = kf.astype(BF16)
    vt_ref[...] = _dot_nt(wvt_ref[...], xb).astype(BF16)
    for r in range(nblk):
        km_ref[r] = jnp.mean(kf[r * C_BLOCK:(r + 1) * C_BLOCK], axis=0, keepdims=True)


def _proj_c(x, wqt, wk, wvt, tm=512):
    t, k = x.shape
    d = wk.shape[1]
    nblk = tm // C_BLOCK
    return pl.pallas_call(
        functools.partial(_proj_c_kernel, nblk=nblk),
        grid=(t // tm,),
        in_specs=[pl.BlockSpec((tm, k), lambda i: (i, 0)), _resident((d, k)), _resident((k, d)),
                  _resident((d, k))],
        out_specs=[pl.BlockSpec((d, tm), lambda i: (0, i)), pl.BlockSpec((tm, d), lambda i: (i, 0)),
                   pl.BlockSpec((d, tm), lambda i: (0, i)), pl.BlockSpec((nblk, 1, d), lambda i: (i, 0, 0))],
        out_shape=[jax.ShapeDtypeStruct((d, t), F32), jax.ShapeDtypeStruct((t, d), BF16),
                   jax.ShapeDtypeStruct((d, t), BF16), jax.ShapeDtypeStruct((t // C_BLOCK, 1, d), F32)],
        compiler_params=_params("parallel"),
        name="proj_moba",
    )(x, wqt, wk, wvt)


_N_AUG = 16
_PAIR = 2 * HEAD_DIM


def _split3_np(a):
    bf = ml_dtypes.bfloat16
    h = a.astype(bf).astype(np.float32)
    m = (a - h).astype(bf).astype(np.float32)
    l = (a - h - m).astype(bf).astype(np.float32)
    return h, m, l


def _pack_rows_np(a):
    bits = a.astype(ml_dtypes.bfloat16).view(np.uint16).astype(np.uint32)
    return bits[0::2] | (bits[1::2] << 16)


def _band_tables(dist_scale):
    ki = np.arange(2 * BAND, dtype=np.float32)
    kaug = np.zeros((2 * BAND, _PAIR), np.float32)
    kaug[:, 0:3] = 1.0
    kaug[:, 3:6] = ki[:, None]
    qaug = np.zeros((N_HEADS, _N_AUG, BAND), np.float32)
    qpos = np.arange(BAND, dtype=np.float32) + BAND
    for h in range(N_HEADS):
        c = np.float32(_alibi_slope(h, N_HEADS) * dist_scale * LOG2E)
        qaug[h, 0:3] = np.stack(_split3_np(-c * qpos))
        qaug[h, 3:6] = np.stack(_split3_np(np.full((BAND,), c, np.float32)))
    return _pack_rows_np(kaug), np.stack([_pack_rows_np(qaug[h]) for h in range(N_HEADS)])


def _band_kernel(*refs, kv_group, max_dist, use_sinks, want_lse):
    q_ref, kp_ref, ko_ref, vp_ref, vo_ref, kaug_ref, qaug_ref = refs[:7]
    pos = 7
    sink_ref = None
    if use_sinks:
        sink_ref = refs[pos]
        pos += 1
    o_ref = refs[pos]
    lse_ref = refs[pos + 1] if want_lse else None

    b = pl.program_id(1)
    ki = lax.broadcasted_iota(jnp.int32, (2 * BAND, BAND), 0)
    qi = lax.broadcasted_iota(jnp.int32, (2 * BAND, BAND), 1)
    dist = qi + BAND - ki
    valid = (dist >= 0) & (dist <= max_dist) & (ki >= jnp.where(b > 0, 0, BAND))
    qt = pltpu.bitcast(q_ref[...], BF16)
    k_all = jnp.concatenate([pltpu.bitcast(kp_ref[...], BF16), pltpu.bitcast(ko_ref[...], BF16)], axis=0)
    vt_all = jnp.concatenate([pltpu.bitcast(vp_ref[...], BF16), pltpu.bitcast(vo_ref[...], BF16)], axis=1)
    kaug = pltpu.bitcast(kaug_ref[...], BF16)
    zslot = jnp.zeros((HEAD_DIM, BAND), BF16)
    zpad = jnp.zeros((_PAIR - _N_AUG, BAND), BF16)
    n_kv = N_HEADS // kv_group
    scores = [None] * N_HEADS
    for pair in range(-(-n_kv // 2)):
        heads = [h for h in range(N_HEADS) if (h // kv_group) // 2 == pair]
        qcats = []
        for h in heads:
            qh = qt[h * HEAD_DIM:(h + 1) * HEAD_DIM]
            slot = [qh, zslot] if (h // kv_group) % 2 == 0 else [zslot, qh]
            qcats.append(jnp.concatenate(slot + [pltpu.bitcast(qaug_ref[h], BF16), zpad], axis=0))
        kcat = jnp.concatenate([k_all[:, pair * _PAIR:(pair + 1) * _PAIR], kaug], axis=1)
        s_all = _dot(kcat, jnp.concatenate(qcats, axis=1))
        for n, h in enumerate(heads):
            scores[h] = s_all[:, n * BAND:(n + 1) * BAND]
    probs, dens, lses = [], [], []
    for h in range(N_HEADS):
        s = jnp.where(valid, scores[h], NEG_INF)
        m = jnp.max(s, axis=0, keepdims=True)
        if use_sinks:
            sk = sink_ref[:, h:h + 1] * LOG2E
            m = jnp.maximum(m, sk)
        p = jnp.exp2(s - m)
        den = jnp.sum(p, axis=0, keepdims=True)
        if use_sinks:
            den = den + jnp.exp2(sk - m)
        probs.append(p.astype(BF16))
        dens.append(den)
        if want_lse:
            lses.append(jnp.broadcast_to((m + jnp.log2(den)) * (1.0 / LOG2E), (HEAD_DIM, BAND)))
    outs = []
    for g in range(n_kv):
        hs = range(g * kv_group, (g + 1) * kv_group)
        pv = _dot(vt_all[g * HEAD_DIM:(g + 1) * HEAD_DIM], jnp.concatenate([probs[h] for h in hs], axis=1))
        outs += [pv[:, n * BAND:(n + 1) * BAND] / dens[h] for n, h in enumerate(hs)]
    o_ref[...] = jnp.concatenate(outs, axis=0).T.astype(o_ref.dtype)
    if want_lse:
        lse_ref[...] = jnp.concatenate(lses, axis=0).T


def _band_attention(qt, k, vt, *, kv_group, dist_scale, max_dist, sinks=None, want_lse=False):
    r, _, l = qt.shape
    d = N_HEADS * HEAD_DIM
    kw = k.shape[2]
    prev = lambda c, b: jnp.maximum(b - 1, 0)
    kaug, qaug = _band_tables(dist_scale)
    in_specs = [pl.BlockSpec((None, d // 2, BAND), lambda c, b: (c, 0, b)),
                pl.BlockSpec((None, BAND // 2, kw), lambda c, b: (c, prev(c, b), 0)),
                pl.BlockSpec((None, BAND // 2, kw), lambda c, b: (c, b, 0)),
                pl.BlockSpec((None, kw // 2, BAND), lambda c, b: (c, 0, prev(c, b))),
                pl.BlockSpec((None, kw // 2, BAND), lambda c, b: (c, 0, b)),
                _resident(kaug.shape), _resident(qaug.shape)]
    args = [qt, k, k, vt, vt, jnp.asarray(kaug), jnp.asarray(qaug)]
    if sinks is not None:
        in_specs.append(_resident((1, N_HEADS)))
        args.append(sinks.reshape(1, N_HEADS).astype(F32))
    o_spec = pl.BlockSpec((BAND, d), lambda c, b: (b, c))
    out_specs = [o_spec]
    out_shape = [jax.ShapeDtypeStruct((l, r * d), BF16)]
    if want_lse:
        out_specs.append(o_spec)
        out_shape.append(jax.ShapeDtypeStruct((l, r * d), F32))
    outs = pl.pallas_call(
        functools.partial(_band_kernel, kv_group=kv_group, max_dist=max_dist,
                          use_sinks=sinks is not None, want_lse=want_lse),
        grid=(r, l // BAND),
        in_specs=in_specs, out_specs=out_specs, out_shape=out_shape,
        compiler_params=_params("parallel", "parallel"),
        name="band_attention",
    )(*args)
    return [o.reshape(l * r, d) for o in outs]


def _dot3(a, b):
    ah, al = _split(a)
    bh, bl = _split(b)
    return _dot(ah, bh) + (_dot(ah, bl) + _dot(al, bh))


def _dot3_nt(a, b):
    ah, al = _split(a)
    bh, bl = _split(b)
    return _dot_nt(ah, bh) + (_dot_nt(ah, bl) + _dot_nt(al, bh))


def _split3(a):
    h = a.astype(BF16).astype(F32)
    r = a - h
    m = r.astype(BF16).astype(F32)
    l = (r - m).astype(BF16).astype(F32)
    return h, m, l


_KV_STEP = 2 * C_BLOCK


def _moba_kernel(slope_ref, qt_ref, k_ref, vt_ref, km_ref, o_ref, sel_ref, s_ref, p_ref, *, nblk):
    hp = pl.program_id(0)
    qi = pl.program_id(1)
    tq = C_BLOCK
    pw = 2 * HEAD_DIM
    qt = qt_ref[...]
    rows = lax.broadcasted_iota(jnp.int32, (pw, tq), 0)
    blk = lax.broadcasted_iota(jnp.int32, (nblk, tq), 0)
    past = blk < qi
    aug_r = lax.broadcasted_iota(jnp.int32, (_N_AUG, tq), 0)
    tqf = lax.broadcasted_iota(jnp.int32, (_N_AUG, tq), 1).astype(F32)
    kcol = lax.broadcasted_iota(jnp.int32, (_KV_STEP, pw), 1)
    tkf = lax.broadcasted_iota(jnp.int32, (_KV_STEP, pw), 0).astype(F32)
    causal = (lax.broadcasted_iota(jnp.int32, (C_BLOCK, tq), 1)
              >= lax.broadcasted_iota(jnp.int32, (C_BLOCK, tq), 0))
    zpad = jnp.zeros((pw - _N_AUG, tq), BF16)
    k0 = pl.multiple_of(qi * C_BLOCK, C_BLOCK)
    k_own = k_ref[pl.ds(k0, C_BLOCK), :]

    qcat, kpos, c2, init = [], [], [], []
    for hh in range(2):
        c = slope_ref[2 * hp + hh] * LOG2E
        qm = jnp.where((rows >= hh * HEAD_DIM) & (rows < (hh + 1) * HEAD_DIM), qt, 0.0)
        gate = jnp.where(past, _dot3(km_ref[...], qm), NEG_INF)
        g = gate
        for r in range(C_TOPK):
            thr = jnp.max(g, axis=0, keepdims=True)
            if r < C_TOPK - 1:
                g = jnp.where(g >= thr, NEG_INF, g)
        sel_ref[hh] = jnp.where(past & (gate >= thr), 1.0, 0.0)

        qh, qmid, ql = _split3(-c * tqf)
        qpos = jnp.where(aug_r == 0, qh, jnp.where(aug_r == 1, qmid, jnp.where(aug_r == 2, ql,
                         jnp.where(aug_r < 6, 1.0, 0.0))))
        qb = (qm * (LOG2E * HEAD_DIM ** -0.5)).astype(BF16)
        qcat.append(jnp.concatenate([qb, qpos.astype(BF16), zpad], axis=0))
        kh, kmid, kl = _split3(c * tkf)
        kpos.append(jnp.where(kcol < 3, 1.0, jnp.where(kcol == 3, kh, jnp.where(kcol == 4, kmid,
                              jnp.where(kcol == 5, kl, 0.0)))).astype(BF16))
        c2.append(c)

        s = _dot(jnp.concatenate([k_own, kpos[hh][:C_BLOCK]], axis=1), qcat[hh])
        s = jnp.where(causal, s, NEG_INF)
        m0 = jnp.max(s, axis=0, keepdims=True)
        p = jnp.exp2(s - m0)
        l0 = jnp.sum(p, axis=0, keepdims=True)
        acc0 = _dot(vt_ref[pl.ds(hh * HEAD_DIM, HEAD_DIM), pl.ds(k0, C_BLOCK)], p.astype(BF16))
        init.append((m0, l0, acc0))

    nsub = _KV_STEP // C_BLOCK

    nsteps = (qi + nsub - 1) // nsub
    last_step = k_ref.shape[0] // _KV_STEP - 1

    def key_start(n):
        return pl.multiple_of(jnp.clip(n, 0, last_step) * _KV_STEP, _KV_STEP)

    def scores(n, slot):
        kb = k_ref[pl.ds(key_start(n), _KV_STEP), :]
        for hh in range(2):
            s_ref[slot, hh] = _dot(jnp.concatenate([kb, kpos[hh]], axis=1), qcat[hh])

    def pv(n, hh, slot):
        return _dot(vt_ref[pl.ds(hh * HEAD_DIM, HEAD_DIM), pl.ds(key_start(n), _KV_STEP)],
                    p_ref[slot, hh])

    def step(n, slot, carry):
        a_prev, state = carry
        off = (qi * C_BLOCK - n * _KV_STEP).astype(F32)
        scores(n + 1, 1 - slot)
        new_a, new_state = [], []
        for hh in range(2):
            m_i, l_i, acc = state[hh]
            acc = a_prev[hh] * acc + pv(n - 1, hh, 1 - slot)
            shift = c2[hh] * off
            sel = [sel_ref[hh, pl.ds(nsub * n + j, 1), :] > 0.5 for j in range(nsub)]
            m_new = m_i
            for j in range(nsub):
                mj = jnp.max(s_ref[slot, hh, j * C_BLOCK:(j + 1) * C_BLOCK, :], axis=0, keepdims=True)
                m_new = jnp.maximum(m_new, jnp.where(sel[j], mj - shift, NEG_INF))
            l_new = jnp.exp2(m_i - m_new) * l_i
            for j in range(nsub):
                rows = slice(j * C_BLOCK, (j + 1) * C_BLOCK)
                p = jnp.exp2(s_ref[slot, hh, rows, :] - jnp.where(sel[j], m_new + shift, BIG))
                p_ref[slot, hh, rows, :] = p.astype(BF16)
                l_new = l_new + jnp.sum(p, axis=0, keepdims=True)
            new_a.append(jnp.exp2(m_i - m_new))
            new_state.append((m_new, l_new, acc))
        return tuple(new_a), tuple(new_state)

    def body(n2, carry):
        return step(2 * n2 + 1, 1, step(2 * n2, 0, carry))

    p_ref[1] = jnp.zeros_like(p_ref[1])
    scores(0, 0)
    one_a = jnp.ones((1, tq), F32)
    npairs = (nsteps + 1) // 2
    a_prev, state = lax.fori_loop(0, npairs, body, ((one_a, one_a), tuple(init)))
    outs = []
    for hh in range(2):
        _, l, acc = state[hh]
        outs.append((a_prev[hh] * acc + pv(2 * npairs - 1, hh, 1)) / l)
    o_ref[...] = jnp.concatenate(outs, axis=0).T.astype(o_ref.dtype)


def _moba_attention(qt, k, vt, kmean):
    d, t = qt.shape
    nblk = t // C_BLOCK
    slopes = jnp.asarray([_alibi_slope(h, N_HEADS) for h in range(N_HEADS)], F32)
    pw = 2 * HEAD_DIM
    assert t % (2 * _KV_STEP) == 0
    return pl.pallas_call(
        functools.partial(_moba_kernel, nblk=nblk),
        grid=(d // pw, nblk),
        in_specs=[pl.BlockSpec(memory_space=pltpu.SMEM),
                  pl.BlockSpec((pw, C_BLOCK), lambda hp, i: (hp, i)),
                  pl.BlockSpec((t, pw), lambda hp, i: (0, hp)),
                  pl.BlockSpec((pw, t), lambda hp, i: (hp, 0)),
                  pl.BlockSpec((nblk, pw), lambda hp, i: (0, hp))],
        out_specs=pl.BlockSpec((C_BLOCK, pw), lambda hp, i: (i, hp)),
        out_shape=jax.ShapeDtypeStruct((t, d), BF16),
        scratch_shapes=[pltpu.VMEM((2, nblk, C_BLOCK), F32), pltpu.VMEM((2, 2, _KV_STEP, C_BLOCK), F32),
                        pltpu.VMEM((2, 2, _KV_STEP, C_BLOCK), BF16)],
        compiler_params=_params("parallel", "arbitrary"),
        name="moba_attention",
    )(slopes, qt, k, vt, kmean)


def _layer_norm(z, g, b):
    mu = jnp.mean(z, axis=-1, keepdims=True)
    zc = z - mu
    var = jnp.mean(zc * zc, axis=-1, keepdims=True)
    return zc * lax.rsqrt(var + LN_EPS) * g + b


def _outproj_ln_kernel(*refs, n_groups, alpha):
    o_refs = refs[:n_groups]
    lse_refs = refs[n_groups:2 * n_groups] if n_groups > 1 else ()
    pos = n_groups + len(lse_refs)
    wo_ref, x_ref, g_ref, b_ref, xn_ref, xb_ref = refs[pos:pos + 6]
    if n_groups == 1:
        o = o_refs[0][...]
    else:
        lses = [r[...] for r in lse_refs]
        m = functools.reduce(jnp.maximum, lses)
        es = [jnp.exp(l - m) for l in lses]
        den = functools.reduce(jnp.add, es)
        o = functools.reduce(jnp.add, [(e / den) * r[...].astype(F32) for e, r in zip(es, o_refs)])
        o = o.astype(BF16)
    y = _dot(o, wo_ref[...])
    xn = _layer_norm(alpha * x_ref[...] + y, g_ref[...], b_ref[...])
    xn_ref[...] = xn
    xb_ref[...] = _pack_rows(xn)


def _outproj_ln(os_, lses, wo, x, g, b, alpha, tm=512):
    t, d = x.shape
    n = len(os_)
    tile = pl.BlockSpec((tm, d), lambda i: (i, 0))
    return pl.pallas_call(
        functools.partial(_outproj_ln_kernel, n_groups=n, alpha=alpha),
        grid=(t // tm,),
        in_specs=[tile] * (n + len(lses)) + [_resident((d, d)), tile, _resident((1, d)), _resident((1, d))],
        out_specs=[tile, pl.BlockSpec((tm // 2, d), lambda i: (i, 0))],
        out_shape=[jax.ShapeDtypeStruct((t, d), F32), jax.ShapeDtypeStruct((t // 2, d), jnp.uint32)],
        compiler_params=_params("parallel"),
        name="outproj_ln",
    )(*os_, *lses, wo, x, g.reshape(1, d), b.reshape(1, d))


def _pack_rows(a):
    return pltpu.bitcast(a.astype(BF16), jnp.uint32)


def _pack_dup(a):
    bits = pltpu.bitcast(a.astype(BF16).astype(F32), jnp.uint32)
    return bits | (bits >> 16)


def _pack_weight_kernel(w_ref, o_ref):
    o_ref[...] = _pack_rows(w_ref[...])


def _pack_weight(w, layer, tr=2048):
    _, r, c = w.shape
    return pl.pallas_call(
        _pack_weight_kernel,
        grid=(r // tr,),
        in_specs=[pl.BlockSpec((None, tr, c), lambda i: (layer, i, 0))],
        out_specs=pl.BlockSpec((tr // 2, c), lambda i: (i, 0)),
        out_shape=jax.ShapeDtypeStruct((r // 2, c), jnp.uint32),
        compiler_params=_params("parallel"),
        name="pack_weight",
    )(w)


def _oddeven_merge(lo, hi, r):
    step = r * 2
    if step < hi - lo:
        yield from _oddeven_merge(lo, hi, step)
        yield from _oddeven_merge(lo + r, hi, step)
        yield from [(i, i + r) for i in range(lo + r, hi - r, step)]
    else:
        yield (lo, lo + r)


def _oddeven_sort(lo, hi):
    if hi > lo:
        mid = lo + (hi - lo) // 2
        yield from _oddeven_sort(lo, mid)
        yield from _oddeven_sort(mid + 1, hi)
        yield from _oddeven_merge(lo, hi, 1)


_SORT16 = tuple(_oddeven_sort(0, PEER_TOPK - 1))


def _top16_sorted(s):
    n = PEER_TOPK
    x = [s[_SUBLANES * j:_SUBLANES * (j + 1)] for j in range(n)]

    def cmpx(i, j):
        x[i], x[j] = jnp.maximum(x[i], x[j]), jnp.minimum(x[i], x[j])

    for i, j in _SORT16:
        cmpx(i, j)
    for shift in (4, 2, 1):
        y = [pltpu.roll(v, shift, 0) for v in x]
        x = [jnp.maximum(x[i], y[n - 1 - i]) for i in range(n)]
        stride = n // 2
        while stride:
            for i in range(n):
                if not i & stride:
                    cmpx(i, i + stride)
            stride //= 2
    top = [v[0:1] for v in x]
    nxt = jnp.max(jnp.where(s < top[n - 1], s, NEG_INF), axis=0, keepdims=True)
    return top + [nxt]


def _top_vals(s, k):
    out = []
    for r in range(k):
        m = jnp.max(s, axis=0, keepdims=True)
        out.append(m)
        if r < k - 1:
            s = jnp.where(s >= m, NEG_INF, s)
    return out


_CAND = [(i, j) for i in range(PEER_TOPK + 1) for j in range(PEER_TOPK + 1)
         if (i + 1) * (j + 1) <= PEER_TOPK + 1]
_NCAND = -(-len(_CAND) // 8) * 8


def _router_kernel(x_ref, wh_ref, wl_ref, sk_ref, r2_ref, e2_ref, n1_ref, w_ref, q_ref, cand_ref):
    xh, xl = _split(x_ref[...])
    q_ref[...] = _dot(xh, wh_ref[...]) + (_dot(xh, wl_ref[...]) + _dot(xl, wh_ref[...]))
    tt = x_ref.shape[0]
    cand_ref[...] = jnp.full((_NCAND, tt), NEG_INF, F32)
    nk = PEER_NKEYS
    for h in range(PEER_HEADS):
        s1 = _dot3_nt(sk_ref[0], q_ref[:, (2 * h) * nk:(2 * h + 1) * nk])
        s2 = _dot3_nt(sk_ref[1], q_ref[:, (2 * h + 1) * nk:(2 * h + 2) * nk])
        a = _top16_sorted(s1)
        b = _top16_sorted(s2)
        for r, (i, j) in enumerate(_CAND):
            cand_ref[r:r + 1, :] = a[i] + b[j]
        v = _top_vals(cand_ref[...], PEER_TOPK + 1)
        thr = 0.5 * (v[PEER_TOPK - 1] + v[PEER_TOPK])
        z = functools.reduce(jnp.add, [jnp.exp(v[r] - v[0]) for r in range(PEER_TOPK)])
        cut = thr - s1
        rank2 = jnp.zeros_like(s2)
        count1 = jnp.zeros_like(s1)
        for r in range(PEER_TOPK):
            rank2 = jnp.where(s2 < b[r], r + 1.0, rank2)
            count1 = jnp.where(b[r] >= cut, r + 1.0, count1)
        r2_ref[h] = _pack_rows(rank2)
        e2_ref[h] = _pack_rows(jnp.exp(s2 - b[0]))
        n1_ref[h] = _pack_dup(count1)
        w_ref[h] = _pack_dup(jnp.exp(s1 - a[0]) / z)


def _router(xn, wq_hi, wq_lo, subkeys, tt=256):
    t, d = xn.shape
    nq = wq_hi.shape[1]
    shape = (PEER_HEADS, PEER_NKEYS, t)
    ospec = pl.BlockSpec((PEER_HEADS, PEER_NKEYS, tt), lambda i: (0, 0, i))
    pspec = pl.BlockSpec((PEER_HEADS, PEER_NKEYS // 2, tt), lambda i: (0, 0, i))
    packed = jax.ShapeDtypeStruct((PEER_HEADS, PEER_NKEYS // 2, t), jnp.uint32)
    return pl.pallas_call(
        _router_kernel,
        grid=(t // tt,),
        in_specs=[pl.BlockSpec((tt, d), lambda i: (i, 0)), _resident((d, nq)), _resident((d, nq)),
                  _resident(subkeys.shape)],
        out_specs=[pspec, pspec, ospec, ospec],
        out_shape=[packed, packed, jax.ShapeDtypeStruct(shape, jnp.uint32),
                   jax.ShapeDtypeStruct(shape, jnp.uint32)],
        scratch_shapes=[pltpu.VMEM((tt, nq), F32), pltpu.VMEM((_NCAND, tt), F32)],
        compiler_params=_params("parallel"),
        name="peer_router",
    )(xn, wq_hi, wq_lo, subkeys)


def _gelu(x):
    return 0.5 * x * (1.0 + lax.erf(x * (2.0 ** -0.5)))


_BF16_ROWS = 16
_PEER_SUB = 256
_PEER_COLS = 256
_PEER_KQ = 512


def _peer_kernel(x_ref, u_ref, v_ref, r2_ref, e2_ref, n1_ref, w_ref, y_ref, acc_ref, h_ref, act_ref,
                 *, n_tiles, n_e):
    s = pl.program_id(0)
    _, te, tt = h_ref.shape
    nk = PEER_NKEYS
    j_c = jnp.clip(s - 2, 0, n_tiles - 1) % n_e

    @pl.when(s == 0)
    def _():
        h_ref[...] = jnp.zeros_like(h_ref)
        act_ref[...] = jnp.zeros_like(act_ref)

    @pl.when(j_c == 0)
    def _():
        acc_ref[...] = jnp.zeros_like(acc_ref)

    n_i1 = te // nk

    def first_matmul(slot, k):
        xb = pltpu.bitcast(x_ref[...], BF16)
        uk = pltpu.bitcast(u_ref[pl.ds(k * _PEER_SUB // 2, _PEER_SUB // 2), :], BF16)
        h_ref[slot, pl.ds(k * _PEER_SUB, _PEER_SUB), :] = _dot_nt(uk, xb)

    def second_matmul(slot, n, kq):
        cols = pl.ds(n * _PEER_COLS, _PEER_COLS)
        vq = pltpu.bitcast(v_ref[pl.ds(kq * _PEER_KQ // 2, _PEER_KQ // 2), :], BF16)
        acc_ref[:, cols] += lax.dot_general(vq, act_ref[slot, pl.ds(kq * _PEER_KQ, _PEER_KQ), cols],
                                            (((0,), (0,)), ((), ())), preferred_element_type=F32)

    def gate(slot, c, v):
        cols = pl.ds(c * 128, 128)
        prow = pl.ds(v * _SUBLANES, _SUBLANES)
        g = [jnp.zeros((_BF16_ROWS, 128), BF16) for _ in range(n_i1)]
        for h in range(PEER_HEADS):
            r2 = pltpu.bitcast(r2_ref[h, prow, cols], BF16)
            e2 = pltpu.bitcast(e2_ref[h, prow, cols], BF16)
            for ii in range(n_i1):
                row = pl.ds(ii, 1)
                n_row = pltpu.bitcast(jnp.broadcast_to(n1_ref[h, row, cols], (_SUBLANES, 128)), BF16)
                w_row = pltpu.bitcast(jnp.broadcast_to(w_ref[h, row, cols], (_SUBLANES, 128)), BF16)
                g[ii] = g[ii] + jnp.where(r2 < n_row, e2, 0.0) * w_row
        for ii in range(n_i1):
            rows = pl.ds(ii * nk + v * _BF16_ROWS, _BF16_ROWS)
            act_ref[slot, rows, cols] = _gelu(h_ref[slot, rows, cols]).astype(BF16) * g[ii]

    def stages(slot):
        pieces = [functools.partial(first_matmul, slot, k) for k in range(te // _PEER_SUB)]
        pieces += [functools.partial(second_matmul, slot, n, kq)
                   for n in range(tt // _PEER_COLS) for kq in range(te // _PEER_KQ)]
        blocks = [(c, v) for c in range(tt // 128) for v in range(nk // _BF16_ROWS)]
        per = -(-len(blocks) // len(pieces))
        for p, piece in enumerate(pieces):
            piece()
            for c, v in blocks[p * per:(p + 1) * per]:
                gate(1 - slot, c, v)

    @pl.when(s % 2 == 0)
    def _():
        stages(0)

    @pl.when(s % 2 == 1)
    def _():
        stages(1)

    @pl.when(j_c == n_e - 1)
    def _():
        y_ref[...] = acc_ref[...].T


def _peer_experts(xb, u, v, r2, e2, n1, w, tt=512, te=2048):
    t, d = 2 * xb.shape[0], xb.shape[1]
    ne = 2 * u.shape[0]
    n_e = ne // te
    n_tiles = (t // tt) * n_e
    pair = lambda lag: (lambda s: jnp.clip(s - lag, 0, n_tiles - 1))
    tok = lambda lag: (lambda s: pair(lag)(s) // n_e)
    exp = lambda lag: (lambda s: pair(lag)(s) % n_e)
    wspec = lambda lag: pl.BlockSpec((te // 2, d), lambda s: (exp(lag)(s), 0))
    pspec = pl.BlockSpec((PEER_HEADS, PEER_NKEYS // 2, tt), lambda s: (0, 0, tok(1)(s)))
    rspec = pl.BlockSpec((PEER_HEADS, te // PEER_NKEYS, tt), lambda s: (0, exp(1)(s), tok(1)(s)))
    return pl.pallas_call(
        functools.partial(_peer_kernel, n_tiles=n_tiles, n_e=n_e),
        grid=(n_tiles + 2,),
        in_specs=[pl.BlockSpec((tt // 2, d), lambda s: (tok(0)(s), 0)), wspec(0), wspec(2),
                  pspec, pspec, rspec, rspec],
        out_specs=pl.BlockSpec((tt, d), lambda s: (tok(2)(s), 0)),
        out_shape=jax.ShapeDtypeStruct((t, d), F32),
        scratch_shapes=[pltpu.VMEM((d, tt), F32), pltpu.VMEM((2, te, tt), F32), pltpu.VMEM((2, te, tt), BF16)],
        compiler_params=_params("arbitrary"),
        name="peer_experts",
    )(xb, u, v, r2, e2, n1, w)


def _ln_ple_kernel(x_ref, y_ref, g_ref, b_ref, p_ref, wp_ref, wg_ref, bg_ref, o_ref, *, alpha):
    xn = _layer_norm(alpha * x_ref[...] + y_ref[...], g_ref[...], b_ref[...])
    gate = jax.nn.sigmoid(_dot(xn.astype(BF16), wg_ref[...]) + bg_ref[...])
    o_ref[...] = xn + gate * _dot(p_ref[...].astype(BF16), wp_ref[...])


def _ln_ple(x, y, g, b, p, layer, wp, wg, bg, alpha, tm=512):
    t, d = x.shape
    dp = p.shape[2]
    tile = pl.BlockSpec((tm, d), lambda i: (i, 0))
    vec = _resident((1, d))
    return pl.pallas_call(
        functools.partial(_ln_ple_kernel, alpha=alpha),
        grid=(t // tm,),
        in_specs=[tile, tile, vec, vec, pl.BlockSpec((None, tm, dp), lambda i: (layer, i, 0)), _resident((dp, d)),
                  _resident((d, d)), vec],
        out_specs=tile,
        out_shape=jax.ShapeDtypeStruct((t, d), F32),
        compiler_params=_params("parallel"),
        name="ln_ple",
    )(x, y, g.reshape(1, d), b.reshape(1, d), p, wp, wg, bg.reshape(1, d))


def _mixer_a(x, w_qkv, sinks):
    d = N_HEADS * HEAD_DIM
    kvw = A_KV_HEADS * HEAD_DIM
    wq, wk, wv = w_qkv[:, :d], w_qkv[:, d:d + kvw], w_qkv[:, d + kvw:]
    qt, k, vt = _proj_t(x, wq.T.astype(BF16), wk.astype(BF16), wv.T.astype(BF16))
    o, = _band_attention(qt, k, vt, kv_group=N_HEADS // A_KV_HEADS, dist_scale=1.0,
                         max_dist=A_WINDOW - 1, sinks=sinks)
    return [o], []


def _mixer_b(x, w_qkv):
    d = N_HEADS * HEAD_DIM
    ng = len(B_PATTERNS)
    wk = w_qkv[:, ng * d:(ng + 1) * d].astype(BF16)
    wvt = w_qkv[:, (ng + 1) * d:].T.astype(BF16)
    os_, lses = [], []
    for gi, (w, r) in enumerate(B_PATTERNS):
        qt, k, vt = _proj_t(x, w_qkv[:, gi * d:(gi + 1) * d].T.astype(BF16), wk, wvt, dil=r)
        o, lse = _band_attention(qt, k, vt, kv_group=1, dist_scale=float(r), max_dist=w // r, want_lse=True)
        os_.append(o)
        lses.append(lse)
    return os_, lses


def _mixer_c(x, w_qkv):
    d = N_HEADS * HEAD_DIM
    wq, wk, wv = w_qkv[:, :d], w_qkv[:, d:2 * d], w_qkv[:, 2 * d:]
    qt, k, vt, kmean = _proj_c(x, wq.T.astype(BF16), wk.astype(BF16), wv.T.astype(BF16))
    return [_moba_attention(qt, k, vt, kmean.reshape(-1, d))], []


def kernel(x, p, a_w_qkv, a_sinks, a_w_o, b_w_qkv, b_w_o, c_w_qkv, c_w_o, ln1_g, ln1_b, ln2_g, ln2_b,
           peer_w_q, peer_subkeys, peer_u, peer_v, ple_w, ple_gate_w, ple_gate_b):
    depth = p.shape[0]
    alpha = (2 * depth) ** 0.25
    bsz, seq, d = x.shape
    assert bsz == 1 and seq % (B_PATTERNS[-1][1] * BAND) == 0
    xt = x.reshape(seq, d)
    for i in range(depth):
        kind, j = i % 3, i // 3
        if kind == 0:
            os_, lses = _mixer_a(xt, a_w_qkv[j], a_sinks[j])
            wo = a_w_o[j]
        elif kind == 1:
            os_, lses = _mixer_b(xt, b_w_qkv[j])
            wo = b_w_o[j]
        else:
            os_, lses = _mixer_c(xt, c_w_qkv[j])
            wo = c_w_o[j]
        x1, x1b = _outproj_ln(os_, lses, wo.astype(BF16), xt, ln1_g[i], ln1_b[i], alpha)
        wq_hi, wq_lo = _split(peer_w_q[i])
        routing = _router(x1, wq_hi, wq_lo, peer_subkeys[i])
        y = _peer_experts(x1b, _pack_weight(peer_u, i), _pack_weight(peer_v, i), *routing)
        xt = _ln_ple(x1, y, ln2_g[i], ln2_b[i], p.reshape(depth, seq, -1), i, ple_w[i].astype(BF16),
                     ple_gate_w[i].astype(BF16), ple_gate_b[i], alpha)
    return xt.reshape(bsz, seq, d)
```

```python
import functools

import jax
import jax.numpy as jnp
import ml_dtypes
import numpy as np
from jax import lax
from jax.experimental import pallas as pl
from jax.experimental.pallas import tpu as pltpu

F32 = jnp.float32
BF16 = jnp.bfloat16
NEG_INF = float("-inf")
BIG = 1e30
LOG2E = 1.4426950408889634

D_MODEL = 1024
HEAD_DIM = 64
N_HEADS = 16
BAND = 128
A_KV_HEADS = 2
A_WINDOW = 128
B_PATTERNS = ((128, 1), (512, 4), (2048, 16))
C_BLOCK = 256
C_TOPK = 3
PEER_HEADS = 8
PEER_NKEYS = 128
PEER_TOPK = 16
LN_EPS = 1e-5
VMEM_LIMIT = 56 * 1024 * 1024
_SUBLANES = 8


def _params(*sem):
    return pltpu.CompilerParams(dimension_semantics=sem, vmem_limit_bytes=VMEM_LIMIT)


def _dot(a, b):
    return jnp.dot(a, b, preferred_element_type=F32)


def _dot_nt(a, b):
    return lax.dot_general(a, b, (((1,), (1,)), ((), ())), preferred_element_type=F32)


def _split(a):
    hi = a.astype(BF16)
    lo = (a - hi.astype(F32)).astype(BF16)
    return hi, lo


def _alibi_slope(h, n):
    return 2.0 ** (-8.0 * (h + 1) / n)


def _resident(shape):
    zeros = (0,) * len(shape)
    return pl.BlockSpec(shape, lambda *_: zeros)


def _proj_t_kernel(x_ref, wqt_ref, wk_ref, wvt_ref, qt_ref, k_ref, vt_ref, *, q_scale):
    xb = x_ref[...].astype(BF16)
    qt_ref[...] = _pack_rows(_dot_nt(wqt_ref[...], xb) * q_scale)
    k_ref[...] = _pack_rows(_dot(xb, wk_ref[...]))
    vt_ref[...] = _pack_rows(_dot_nt(wvt_ref[...], xb))


def _proj_t(x, wqt, wk, wvt, dil=1, tm=512):
    t, kdim = x.shape
    d, kw = wqt.shape[0], wk.shape[1]
    l = t // dil
    tm = min(tm, l)
    u32 = jnp.uint32
    return pl.pallas_call(
        functools.partial(_proj_t_kernel, q_scale=LOG2E * HEAD_DIM ** -0.5),
        grid=(dil, l // tm),
        in_specs=[pl.BlockSpec((tm, kdim), lambda c, i: (i, c)), _resident((d, kdim)), _resident((kdim, kw)),
                  _resident((kw, kdim))],
        out_specs=[pl.BlockSpec((None, d // 2, tm), lambda c, i: (c, 0, i)),
                   pl.BlockSpec((None, tm // 2, kw), lambda c, i: (c, i, 0)),
                   pl.BlockSpec((None, kw // 2, tm), lambda c, i: (c, 0, i))],
        out_shape=[jax.ShapeDtypeStruct((dil, d // 2, l), u32), jax.ShapeDtypeStruct((dil, l // 2, kw), u32),
                   jax.ShapeDtypeStruct((dil, kw // 2, l), u32)],
        compiler_params=_params("parallel", "parallel"),
        name="proj_band",
    )(x.reshape(l, dil * kdim), wqt, wk, wvt)


def _proj_c_kernel(x_ref, wqt_ref, wk_ref, wvt_ref, qt_ref, k_ref, vt_ref, km_ref, *, nblk):
    xb = x_ref[...].astype(BF16)
    qt_ref[...] = _dot_nt(wqt_ref[...], xb)
    kf = _dot(xb, wk_ref[...])
    k_ref[...] = kf.astype(BF16)
    vt_ref[...] = _dot_nt(wvt_ref[...], xb).astype(BF16)
    for r in range(nblk):
        km_ref[r] = jnp.mean(kf[r * C_BLOCK:(r + 1) * C_BLOCK], axis=0, keepdims=True)


def _proj_c(x, wqt, wk, wvt, tm=512):
    t, k = x.shape
    d = wk.shape[1]
    nblk = tm // C_BLOCK
    return pl.pallas_call(
        functools.partial(_proj_c_kernel, nblk=nblk),
        grid=(t // tm,),
        in_specs=[pl.BlockSpec((tm, k), lambda i: (i, 0)), _resident((d, k)), _resident((k, d)),
                  _resident((d, k))],
        out_specs=[pl.BlockSpec((d, tm), lambda i: (0, i)), pl.BlockSpec((tm, d), lambda i: (i, 0)),
                   pl.BlockSpec((d, tm), lambda i: (0, i)), pl.BlockSpec((nblk, 1, d), lambda i: (i, 0, 0))],
        out_shape=[jax.ShapeDtypeStruct((d, t), F32), jax.ShapeDtypeStruct((t, d), BF16),
                   jax.ShapeDtypeStruct((d, t), BF16), jax.ShapeDtypeStruct((t // C_BLOCK, 1, d), F32)],
        compiler_params=_params("parallel"),
        name="proj_moba",
    )(x, wqt, wk, wvt)


_N_AUG = 16
_PAIR = 2 * HEAD_DIM


def _split3_np(a):
    bf = ml_dtypes.bfloat16
    h = a.astype(bf).astype(np.float32)
    m = (a - h).astype(bf).astype(np.float32)
    l = (a - h - m).astype(bf).astype(np.float32)
    return h, m, l


def _pack_rows_np(a):
    bits = a.astype(ml_dtypes.bfloat16).view(np.uint16).astype(np.uint32)
    return bits[0::2] | (bits[1::2] << 16)


def _band_tables(dist_scale):
    ki = np.arange(2 * BAND, dtype=np.float32)
    kaug = np.zeros((2 * BAND, _PAIR), np.float32)
    kaug[:, 0:3] = 1.0
    kaug[:, 3:6] = ki[:, None]
    qaug = np.zeros((N_HEADS, _N_AUG, BAND), np.float32)
    qpos = np.arange(BAND, dtype=np.float32) + BAND
    for h in range(N_HEADS):
        c = np.float32(_alibi_slope(h, N_HEADS) * dist_scale * LOG2E)
        qaug[h, 0:3] = np.stack(_split3_np(-c * qpos))
        qaug[h, 3:6] = np.stack(_split3_np(np.full((BAND,), c, np.float32)))
    return _pack_rows_np(kaug), np.stack([_pack_rows_np(qaug[h]) for h in range(N_HEADS)])


def _band_kernel(*refs, kv_group, max_dist, use_sinks, want_lse):
    q_ref, kp_ref, ko_ref, vp_ref, vo_ref, kaug_ref, qaug_ref = refs[:7]
    pos = 7
    sink_ref = None
    if use_sinks:
        sink_ref = refs[pos]
        pos += 1
    o_ref = refs[pos]
    lse_ref = refs[pos + 1] if want_lse else None

    b = pl.program_id(1)
    ki = lax.broadcasted_iota(jnp.int32, (2 * BAND, BAND), 0)
    qi = lax.broadcasted_iota(jnp.int32, (2 * BAND, BAND), 1)
    dist = qi + BAND - ki
    valid = (dist >= 0) & (dist <= max_dist) & (ki >= jnp.where(b > 0, 0, BAND))
    qt = pltpu.bitcast(q_ref[...], BF16)
    k_all = jnp.concatenate([pltpu.bitcast(kp_ref[...], BF16), pltpu.bitcast(ko_ref[...], BF16)], axis=0)
    vt_all = jnp.concatenate([pltpu.bitcast(vp_ref[...], BF16), pltpu.bitcast(vo_ref[...], BF16)], axis=1)
    kaug = pltpu.bitcast(kaug_ref[...], BF16)
    zslot = jnp.zeros((HEAD_DIM, BAND), BF16)
    zpad = jnp.zeros((_PAIR - _N_AUG, BAND), BF16)
    n_kv = N_HEADS // kv_group
    scores = [None] * N_HEADS
    for pair in range(-(-n_kv // 2)):
        heads = [h for h in range(N_HEADS) if (h // kv_group) // 2 == pair]
        qcats = []
        for h in heads:
            qh = qt[h * HEAD_DIM:(h + 1) * HEAD_DIM]
            slot = [qh, zslot] if (h // kv_group) % 2 == 0 else [zslot, qh]
            qcats.append(jnp.concatenate(slot + [pltpu.bitcast(qaug_ref[h], BF16), zpad], axis=0))
        kcat = jnp.concatenate([k_all[:, pair * _PAIR:(pair + 1) * _PAIR], kaug], axis=1)
        s_all = _dot(kcat, jnp.concatenate(qcats, axis=1))
        for n, h in enumerate(heads):
            scores[h] = s_all[:, n * BAND:(n + 1) * BAND]
    probs, dens, lses = [], [], []
    for h in range(N_HEADS):
        s = jnp.where(valid, scores[h], NEG_INF)
        m = jnp.max(s, axis=0, keepdims=True)
        if use_sinks:
            sk = sink_ref[:, h:h + 1] * LOG2E
            m = jnp.maximum(m, sk)
        p = jnp.exp2(s - m)
        den = jnp.sum(p, axis=0, keepdims=True)
        if use_sinks:
            den = den + jnp.exp2(sk - m)
        probs.append(p.astype(BF16))
        dens.append(den)
        if want_lse:
            lses.append(jnp.broadcast_to((m + jnp.log2(den)) * (1.0 / LOG2E), (HEAD_DIM, BAND)))
    outs = []
    for g in range(n_kv):
        hs = range(g * kv_group, (g + 1) * kv_group)
        pv = _dot(vt_all[g * HEAD_DIM:(g + 1) * HEAD_DIM], jnp.concatenate([probs[h] for h in hs], axis=1))
        outs += [pv[:, n * BAND:(n + 1) * BAND] / dens[h] for n, h in enumerate(hs)]
    o_ref[...] = jnp.concatenate(outs, axis=0).T.astype(o_ref.dtype)
    if want_lse:
        lse_ref[...] = jnp.concatenate(lses, axis=0).T


def _band_attention(qt, k, vt, *, kv_group, dist_scale, max_dist, sinks=None, want_lse=False):
    r, _, l = qt.shape
    d = N_HEADS * HEAD_DIM
    kw = k.shape[2]
    prev = lambda c, b: jnp.maximum(b - 1, 0)
    kaug, qaug = _band_tables(dist_scale)
    in_specs = [pl.BlockSpec((None, d // 2, BAND), lambda c, b: (c, 0, b)),
                pl.BlockSpec((None, BAND // 2, kw), lambda c, b: (c, prev(c, b), 0)),
                pl.BlockSpec((None, BAND // 2, kw), lambda c, b: (c, b, 0)),
                pl.BlockSpec((None, kw // 2, BAND), lambda c, b: (c, 0, prev(c, b))),
                pl.BlockSpec((None, kw // 2, BAND), lambda c, b: (c, 0, b)),
                _resident(kaug.shape), _resident(qaug.shape)]
    args = [qt, k, k, vt, vt, jnp.asarray(kaug), jnp.asarray(qaug)]
    if sinks is not None:
        in_specs.append(_resident((1, N_HEADS)))
        args.append(sinks.reshape(1, N_HEADS).astype(F32))
    o_spec = pl.BlockSpec((BAND, d), lambda c, b: (b, c))
    out_specs = [o_spec]
    out_shape = [jax.ShapeDtypeStruct((l, r * d), BF16)]
    if want_lse:
        out_specs.append(o_spec)
        out_shape.append(jax.ShapeDtypeStruct((l, r * d), F32))
    outs = pl.pallas_call(
        functools.partial(_band_kernel, kv_group=kv_group, max_dist=max_dist,
                          use_sinks=sinks is not None, want_lse=want_lse),
        grid=(r, l // BAND),
        in_specs=in_specs, out_specs=out_specs, out_shape=out_shape,
        compiler_params=_params("parallel", "parallel"),
        name="band_attention",
    )(*args)
    return [o.reshape(l * r, d) for o in outs]


def _dot3(a, b):
    ah, al = _split(a)
    bh, bl = _split(b)
    return _dot(ah, bh) + (_dot(ah, bl) + _dot(al, bh))


def _dot3_nt(a, b):
    ah, al = _split(a)
    bh, bl = _split(b)
    return _dot_nt(ah, bh) + (_dot_nt(ah, bl) + _dot_nt(al, bh))


def _split3(a):
    h = a.astype(BF16).astype(F32)
    r = a - h
    m = r.astype(BF16).astype(F32)
    l = (r - m).astype(BF16).astype(F32)
    return h, m, l


_KV_STEP = 2 * C_BLOCK


def _moba_kernel(slope_ref, qt_ref, k_ref, vt_ref, km_ref, o_ref, sel_ref, s_ref, p_ref, *, nblk):
    hp = pl.program_id(0)
    qi = pl.program_id(1)
    tq = C_BLOCK
    pw = 2 * HEAD_DIM
    qt = qt_ref[...]
    rows = lax.broadcasted_iota(jnp.int32, (pw, tq), 0)
    blk = lax.broadcasted_iota(jnp.int32, (nblk, tq), 0)
    past = blk < qi
    aug_r = lax.broadcasted_iota(jnp.int32, (_N_AUG, tq), 0)
    tqf = lax.broadcasted_iota(jnp.int32, (_N_AUG, tq), 1).astype(F32)
    kcol = lax.broadcasted_iota(jnp.int32, (_KV_STEP, pw), 1)
    tkf = lax.broadcasted_iota(jnp.int32, (_KV_STEP, pw), 0).astype(F32)
    causal = (lax.broadcasted_iota(jnp.int32, (C_BLOCK, tq), 1)
              >= lax.broadcasted_iota(jnp.int32, (C_BLOCK, tq), 0))
    zpad = jnp.zeros((pw - _N_AUG, tq), BF16)
    k0 = pl.multiple_of(qi * C_BLOCK, C_BLOCK)
    k_own = k_ref[pl.ds(k0, C_BLOCK), :]

    qcat, kpos, c2, init = [], [], [], []
    for hh in range(2):
        c = slope_ref[2 * hp + hh] * LOG2E
        qm = jnp.where((rows >= hh * HEAD_DIM) & (rows < (hh + 1) * HEAD_DIM), qt, 0.0)
        gate = jnp.where(past, _dot3(km_ref[...], qm), NEG_INF)
        g = gate
        for r in range(C_TOPK):
            thr = jnp.max(g, axis=0, keepdims=True)
            if r < C_TOPK - 1:
                g = jnp.where(g >= thr, NEG_INF, g)
        sel_ref[hh] = jnp.where(past & (gate >= thr), 1.0, 0.0)

        qh, qmid, ql = _split3(-c * tqf)
        qpos = jnp.where(aug_r == 0, qh, jnp.where(aug_r == 1, qmid, jnp.where(aug_r == 2, ql,
                         jnp.where(aug_r < 6, 1.0, 0.0))))
        qb = (qm * (LOG2E * HEAD_DIM ** -0.5)).astype(BF16)
        qcat.append(jnp.concatenate([qb, qpos.astype(BF16), zpad], axis=0))
        kh, kmid, kl = _split3(c * tkf)
        kpos.append(jnp.where(kcol < 3, 1.0, jnp.where(kcol == 3, kh, jnp.where(kcol == 4, kmid,
                              jnp.where(kcol == 5, kl, 0.0)))).astype(BF16))
        c2.append(c)

        s = _dot(jnp.concatenate([k_own, kpos[hh][:C_BLOCK]], axis=1), qcat[hh])
        s = jnp.where(causal, s, NEG_INF)
        m0 = jnp.max(s, axis=0, keepdims=True)
        p = jnp.exp2(s - m0)
        l0 = jnp.sum(p, axis=0, keepdims=True)
        acc0 = _dot(vt_ref[pl.ds(hh * HEAD_DIM, HEAD_DIM), pl.ds(k0, C_BLOCK)], p.astype(BF16))
        init.append((m0, l0, acc0))

    nsub = _KV_STEP // C_BLOCK

    nsteps = (qi + nsub - 1) // nsub
    last_step = k_ref.shape[0] // _KV_STEP - 1

    def key_start(n):
        return pl.multiple_of(jnp.clip(n, 0, last_step) * _KV_STEP, _KV_STEP)

    def score_piece(n, slot, hh, j):
        rows = slice(j * C_BLOCK, (j + 1) * C_BLOCK)
        kb = k_ref[pl.ds(key_start(n) + j * C_BLOCK, C_BLOCK), :]
        s_ref[slot, hh, rows, :] = _dot(jnp.concatenate([kb, kpos[hh][rows]], axis=1), qcat[hh])

    def scores(n, slot):
        for hh in range(2):
            for j in range(nsub):
                score_piece(n, slot, hh, j)

    def pv(n, hh, slot):
        return _dot(vt_ref[pl.ds(hh * HEAD_DIM, HEAD_DIM), pl.ds(key_start(n), _KV_STEP)],
                    p_ref[slot, hh])

    def step(n, slot, carry):
        a_prev, state = carry
        off = (qi * C_BLOCK - n * _KV_STEP).astype(F32)
        mxu = [functools.partial(score_piece, n + 1, 1 - slot, hh, j) for hh in range(2) for j in range(nsub)]
        new_a, new_state = [], []
        for hh in range(2):
            m_i, l_i, acc = state[hh]
            shift = c2[hh] * off
            sel = [sel_ref[hh, pl.ds(nsub * n + j, 1), :] > 0.5 for j in range(nsub)]
            m_new = m_i
            for j in range(nsub):
                mxu.pop(0)()
                mj = jnp.max(s_ref[slot, hh, j * C_BLOCK:(j + 1) * C_BLOCK, :], axis=0, keepdims=True)
                m_new = jnp.maximum(m_new, jnp.where(sel[j], mj - shift, NEG_INF))
            acc = a_prev[hh] * acc + pv(n - 1, hh, 1 - slot)
            l_new = jnp.exp2(m_i - m_new) * l_i
            for j in range(nsub):
                rows = slice(j * C_BLOCK, (j + 1) * C_BLOCK)
                p = jnp.exp2(s_ref[slot, hh, rows, :] - jnp.where(sel[j], m_new + shift, BIG))
                p_ref[slot, hh, rows, :] = p.astype(BF16)
                l_new = l_new + jnp.sum(p, axis=0, keepdims=True)
            new_a.append(jnp.exp2(m_i - m_new))
            new_state.append((m_new, l_new, acc))
        return tuple(new_a), tuple(new_state)

    def body(n2, carry):
        return step(2 * n2 + 1, 1, step(2 * n2, 0, carry))

    p_ref[1] = jnp.zeros_like(p_ref[1])
    scores(0, 0)
    one_a = jnp.ones((1, tq), F32)
    npairs = (nsteps + 1) // 2
    a_prev, state = lax.fori_loop(0, npairs, body, ((one_a, one_a), tuple(init)))
    outs = []
    for hh in range(2):
        _, l, acc = state[hh]
        outs.append((a_prev[hh] * acc + pv(2 * npairs - 1, hh, 1)) / l)
    o_ref[...] = jnp.concatenate(outs, axis=0).T.astype(o_ref.dtype)


def _moba_attention(qt, k, vt, kmean):
    d, t = qt.shape
    nblk = t // C_BLOCK
    slopes = jnp.asarray([_alibi_slope(h, N_HEADS) for h in range(N_HEADS)], F32)
    pw = 2 * HEAD_DIM
    assert t % (2 * _KV_STEP) == 0
    return pl.pallas_call(
        functools.partial(_moba_kernel, nblk=nblk),
        grid=(d // pw, nblk),
        in_specs=[pl.BlockSpec(memory_space=pltpu.SMEM),
                  pl.BlockSpec((pw, C_BLOCK), lambda hp, i: (hp, i)),
                  pl.BlockSpec((t, pw), lambda hp, i: (0, hp)),
                  pl.BlockSpec((pw, t), lambda hp, i: (hp, 0)),
                  pl.BlockSpec((nblk, pw), lambda hp, i: (0, hp))],
        out_specs=pl.BlockSpec((C_BLOCK, pw), lambda hp, i: (i, hp)),
        out_shape=jax.ShapeDtypeStruct((t, d), BF16),
        scratch_shapes=[pltpu.VMEM((2, nblk, C_BLOCK), F32), pltpu.VMEM((2, 2, _KV_STEP, C_BLOCK), F32),
                        pltpu.VMEM((2, 2, _KV_STEP, C_BLOCK), BF16)],
        compiler_params=_params("parallel", "arbitrary"),
        name="moba_attention",
    )(slopes, qt, k, vt, kmean)


def _layer_norm(z, g, b):
    mu = jnp.mean(z, axis=-1, keepdims=True)
    zc = z - mu
    var = jnp.mean(zc * zc, axis=-1, keepdims=True)
    return zc * lax.rsqrt(var + LN_EPS) * g + b


def _outproj_ln_kernel(*refs, n_groups, alpha):
    o_refs = refs[:n_groups]
    lse_refs = refs[n_groups:2 * n_groups] if n_groups > 1 else ()
    pos = n_groups + len(lse_refs)
    wo_ref, x_ref, g_ref, b_ref, xn_ref, xb_ref = refs[pos:pos + 6]
    if n_groups == 1:
        o = o_refs[0][...]
    else:
        lses = [r[...] for r in lse_refs]
        m = functools.reduce(jnp.maximum, lses)
        es = [jnp.exp(l - m) for l in lses]
        den = functools.reduce(jnp.add, es)
        o = functools.reduce(jnp.add, [(e / den) * r[...].astype(F32) for e, r in zip(es, o_refs)])
        o = o.astype(BF16)
    y = _dot(o, wo_ref[...])
    xn = _layer_norm(alpha * x_ref[...] + y, g_ref[...], b_ref[...])
    xn_ref[...] = xn
    xb_ref[...] = _pack_rows(xn)


def _outproj_ln(os_, lses, wo, x, g, b, alpha, tm=512):
    t, d = x.shape
    n = len(os_)
    tile = pl.BlockSpec((tm, d), lambda i: (i, 0))
    return pl.pallas_call(
        functools.partial(_outproj_ln_kernel, n_groups=n, alpha=alpha),
        grid=(t // tm,),
        in_specs=[tile] * (n + len(lses)) + [_resident((d, d)), tile, _resident((1, d)), _resident((1, d))],
        out_specs=[tile, pl.BlockSpec((tm // 2, d), lambda i: (i, 0))],
        out_shape=[jax.ShapeDtypeStruct((t, d), F32), jax.ShapeDtypeStruct((t // 2, d), jnp.uint32)],
        compiler_params=_params("parallel"),
        name="outproj_ln",
    )(*os_, *lses, wo, x, g.reshape(1, d), b.reshape(1, d))


def _pack_rows(a):
    return pltpu.bitcast(a.astype(BF16), jnp.uint32)


def _pack_dup(a):
    bits = pltpu.bitcast(a.astype(BF16).astype(F32), jnp.uint32)
    return bits | (bits >> 16)


def _pack_weight_kernel(w_ref, o_ref):
    o_ref[...] = _pack_rows(w_ref[...])


def _pack_weight(w, layer, tr=2048):
    _, r, c = w.shape
    return pl.pallas_call(
        _pack_weight_kernel,
        grid=(r // tr,),
        in_specs=[pl.BlockSpec((None, tr, c), lambda i: (layer, i, 0))],
        out_specs=pl.BlockSpec((tr // 2, c), lambda i: (i, 0)),
        out_shape=jax.ShapeDtypeStruct((r // 2, c), jnp.uint32),
        compiler_params=_params("parallel"),
        name="pack_weight",
    )(w)


def _oddeven_merge(lo, hi, r):
    step = r * 2
    if step < hi - lo:
        yield from _oddeven_merge(lo, hi, step)
        yield from _oddeven_merge(lo + r, hi, step)
        yield from [(i, i + r) for i in range(lo + r, hi - r, step)]
    else:
        yield (lo, lo + r)


def _oddeven_sort(lo, hi):
    if hi > lo:
        mid = lo + (hi - lo) // 2
        yield from _oddeven_sort(lo, mid)
        yield from _oddeven_sort(mid + 1, hi)
        yield from _oddeven_merge(lo, hi, 1)


_SORT16 = tuple(_oddeven_sort(0, PEER_TOPK - 1))


def _top16_sorted(s):
    n = PEER_TOPK
    x = [s[_SUBLANES * j:_SUBLANES * (j + 1)] for j in range(n)]

    def cmpx(i, j):
        x[i], x[j] = jnp.maximum(x[i], x[j]), jnp.minimum(x[i], x[j])

    for i, j in _SORT16:
        cmpx(i, j)
    for shift in (4, 2, 1):
        y = [pltpu.roll(v, shift, 0) for v in x]
        x = [jnp.maximum(x[i], y[n - 1 - i]) for i in range(n)]
        stride = n // 2
        while stride:
            for i in range(n):
                if not i & stride:
                    cmpx(i, i + stride)
            stride //= 2
    top = [v[0:1] for v in x]
    nxt = jnp.max(jnp.where(s < top[n - 1], s, NEG_INF), axis=0, keepdims=True)
    return top + [nxt]


def _top_vals(s, k):
    out = []
    for r in range(k):
        m = jnp.max(s, axis=0, keepdims=True)
        out.append(m)
        if r < k - 1:
            s = jnp.where(s >= m, NEG_INF, s)
    return out


_CAND = [(i, j) for i in range(PEER_TOPK + 1) for j in range(PEER_TOPK + 1)
         if (i + 1) * (j + 1) <= PEER_TOPK + 1]
_NCAND = -(-len(_CAND) // 8) * 8


def _router_kernel(x_ref, wh_ref, wl_ref, sk_ref, r2_ref, e2_ref, n1_ref, w_ref, q_ref, cand_ref):
    xh, xl = _split(x_ref[...])
    q_ref[...] = _dot(xh, wh_ref[...]) + (_dot(xh, wl_ref[...]) + _dot(xl, wh_ref[...]))
    tt = x_ref.shape[0]
    cand_ref[...] = jnp.full((_NCAND, tt), NEG_INF, F32)
    nk = PEER_NKEYS
    for h in range(PEER_HEADS):
        s1 = _dot3_nt(sk_ref[0], q_ref[:, (2 * h) * nk:(2 * h + 1) * nk])
        s2 = _dot3_nt(sk_ref[1], q_ref[:, (2 * h + 1) * nk:(2 * h + 2) * nk])
        a = _top16_sorted(s1)
        b = _top16_sorted(s2)
        for r, (i, j) in enumerate(_CAND):
            cand_ref[r:r + 1, :] = a[i] + b[j]
        v = _top_vals(cand_ref[...], PEER_TOPK + 1)
        thr = 0.5 * (v[PEER_TOPK - 1] + v[PEER_TOPK])
        z = functools.reduce(jnp.add, [jnp.exp(v[r] - v[0]) for r in range(PEER_TOPK)])
        cut = thr - s1
        rank2 = jnp.zeros_like(s2)
        count1 = jnp.zeros_like(s1)
        for r in range(PEER_TOPK):
            rank2 = jnp.where(s2 < b[r], r + 1.0, rank2)
            count1 = jnp.where(b[r] >= cut, r + 1.0, count1)
        r2_ref[h] = _pack_rows(rank2)
        e2_ref[h] = _pack_rows(jnp.exp(s2 - b[0]))
        n1_ref[h] = _pack_dup(count1)
        w_ref[h] = _pack_dup(jnp.exp(s1 - a[0]) / z)


def _router(xn, wq_hi, wq_lo, subkeys, tt=256):
    t, d = xn.shape
    nq = wq_hi.shape[1]
    shape = (PEER_HEADS, PEER_NKEYS, t)
    ospec = pl.BlockSpec((PEER_HEADS, PEER_NKEYS, tt), lambda i: (0, 0, i))
    pspec = pl.BlockSpec((PEER_HEADS, PEER_NKEYS // 2, tt), lambda i: (0, 0, i))
    packed = jax.ShapeDtypeStruct((PEER_HEADS, PEER_NKEYS // 2, t), jnp.uint32)
    return pl.pallas_call(
        _router_kernel,
        grid=(t // tt,),
        in_specs=[pl.BlockSpec((tt, d), lambda i: (i, 0)), _resident((d, nq)), _resident((d, nq)),
                  _resident(subkeys.shape)],
        out_specs=[pspec, pspec, ospec, ospec],
        out_shape=[packed, packed, jax.ShapeDtypeStruct(shape, jnp.uint32),
                   jax.ShapeDtypeStruct(shape, jnp.uint32)],
        scratch_shapes=[pltpu.VMEM((tt, nq), F32), pltpu.VMEM((_NCAND, tt), F32)],
        compiler_params=_params("parallel"),
        name="peer_router",
    )(xn, wq_hi, wq_lo, subkeys)


def _gelu(x):
    return 0.5 * x * (1.0 + lax.erf(x * (2.0 ** -0.5)))


_BF16_ROWS = 16
_PEER_SUB = 512
_PEER_COLS = 256
_PEER_KQ = 512


def _peer_kernel(x_ref, u_ref, v_ref, r2_ref, e2_ref, n1_ref, w_ref, y_ref, acc_ref, h_ref, act_ref,
                 *, n_tiles, n_e):
    s = pl.program_id(0)
    _, te, tt = h_ref.shape
    nk = PEER_NKEYS
    j_c = jnp.clip(s - 2, 0, n_tiles - 1) % n_e

    @pl.when(s == 0)
    def _():
        h_ref[...] = jnp.zeros_like(h_ref)
        act_ref[...] = jnp.zeros_like(act_ref)

    @pl.when(j_c == 0)
    def _():
        acc_ref[...] = jnp.zeros_like(acc_ref)

    n_i1 = te // nk

    def first_matmul(slot, k):
        xb = pltpu.bitcast(x_ref[...], BF16)
        uk = pltpu.bitcast(u_ref[pl.ds(k * _PEER_SUB // 2, _PEER_SUB // 2), :], BF16)
        h_ref[slot, pl.ds(k * _PEER_SUB, _PEER_SUB), :] = _dot_nt(uk, xb)

    def second_matmul(slot, n, kq):
        cols = pl.ds(n * _PEER_COLS, _PEER_COLS)
        vq = pltpu.bitcast(v_ref[pl.ds(kq * _PEER_KQ // 2, _PEER_KQ // 2), :], BF16)
        acc_ref[:, cols] += lax.dot_general(vq, act_ref[slot, pl.ds(kq * _PEER_KQ, _PEER_KQ), cols],
                                            (((0,), (0,)), ((), ())), preferred_element_type=F32)

    def gate(slot, c, v):
        cols = pl.ds(c * 128, 128)
        prow = pl.ds(v * _SUBLANES, _SUBLANES)
        g = [jnp.zeros((_BF16_ROWS, 128), BF16) for _ in range(n_i1)]
        for h in range(PEER_HEADS):
            r2 = pltpu.bitcast(r2_ref[h, prow, cols], BF16)
            e2 = pltpu.bitcast(e2_ref[h, prow, cols], BF16)
            for ii in range(n_i1):
                row = pl.ds(ii, 1)
                n_row = pltpu.bitcast(jnp.broadcast_to(n1_ref[h, row, cols], (_SUBLANES, 128)), BF16)
                w_row = pltpu.bitcast(jnp.broadcast_to(w_ref[h, row, cols], (_SUBLANES, 128)), BF16)
                g[ii] = g[ii] + jnp.where(r2 < n_row, e2, 0.0) * w_row
        for ii in range(n_i1):
            rows = pl.ds(ii * nk + v * _BF16_ROWS, _BF16_ROWS)
            act_ref[slot, rows, cols] = _gelu(h_ref[slot, rows, cols]).astype(BF16) * g[ii]

    def stages(slot):
        pieces = [functools.partial(first_matmul, slot, k) for k in range(te // _PEER_SUB)]
        pieces += [functools.partial(second_matmul, slot, n, kq)
                   for n in range(tt // _PEER_COLS) for kq in range(te // _PEER_KQ)]
        blocks = [(c, v) for c in range(tt // 128) for v in range(nk // _BF16_ROWS)]
        per = -(-len(blocks) // len(pieces))
        for p, piece in enumerate(pieces):
            piece()
            for c, v in blocks[p * per:(p + 1) * per]:
                gate(1 - slot, c, v)

    @pl.when(s % 2 == 0)
    def _():
        stages(0)

    @pl.when(s % 2 == 1)
    def _():
        stages(1)

    @pl.when(j_c == n_e - 1)
    def _():
        y_ref[...] = acc_ref[...].T


def _peer_experts(xb, u, v, r2, e2, n1, w, tt=512, te=2048):
    t, d = 2 * xb.shape[0], xb.shape[1]
    ne = 2 * u.shape[0]
    n_e = ne // te
    n_tiles = (t // tt) * n_e
    pair = lambda lag: (lambda s: jnp.clip(s - lag, 0, n_tiles - 1))
    tok = lambda lag: (lambda s: pair(lag)(s) // n_e)
    exp = lambda lag: (lambda s: pair(lag)(s) % n_e)
    wspec = lambda lag: pl.BlockSpec((te // 2, d), lambda s: (exp(lag)(s), 0))
    pspec = pl.BlockSpec((PEER_HEADS, PEER_NKEYS // 2, tt), lambda s: (0, 0, tok(1)(s)))
    rspec = pl.BlockSpec((PEER_HEADS, te // PEER_NKEYS, tt), lambda s: (0, exp(1)(s), tok(1)(s)))
    return pl.pallas_call(
        functools.partial(_peer_kernel, n_tiles=n_tiles, n_e=n_e),
        grid=(n_tiles + 2,),
        in_specs=[pl.BlockSpec((tt // 2, d), lambda s: (tok(0)(s), 0)), wspec(0), wspec(2),
                  pspec, pspec, rspec, rspec],
        out_specs=pl.BlockSpec((tt, d), lambda s: (tok(2)(s), 0)),
        out_shape=jax.ShapeDtypeStruct((t, d), F32),
        scratch_shapes=[pltpu.VMEM((d, tt), F32), pltpu.VMEM((2, te, tt), F32), pltpu.VMEM((2, te, tt), BF16)],
        compiler_params=_params("arbitrary"),
        name="peer_experts",
    )(xb, u, v, r2, e2, n1, w)


def _ln_ple_kernel(x_ref, y_ref, g_ref, b_ref, p_ref, wp_ref, wg_ref, bg_ref, o_ref, *, alpha):
    xn = _layer_norm(alpha * x_ref[...] + y_ref[...], g_ref[...], b_ref[...])
    gate = jax.nn.sigmoid(_dot(xn.astype(BF16), wg_ref[...]) + bg_ref[...])
    o_ref[...] = xn + gate * _dot(p_ref[...].astype(BF16), wp_ref[...])


def _ln_ple(x, y, g, b, p, layer, wp, wg, bg, alpha, tm=512):
    t, d = x.shape
    dp = p.shape[2]
    tile = pl.BlockSpec((tm, d), lambda i: (i, 0))
    vec = _resident((1, d))
    return pl.pallas_call(
        functools.partial(_ln_ple_kernel, alpha=alpha),
        grid=(t // tm,),
        in_specs=[tile, tile, vec, vec, pl.BlockSpec((None, tm, dp), lambda i: (layer, i, 0)), _resident((dp, d)),
                  _resident((d, d)), vec],
        out_specs=tile,
        out_shape=jax.ShapeDtypeStruct((t, d), F32),
        compiler_params=_params("parallel"),
        name="ln_ple",
    )(x, y, g.reshape(1, d), b.reshape(1, d), p, wp, wg, bg.reshape(1, d))


def _mixer_a(x, w_qkv, sinks):
    d = N_HEADS * HEAD_DIM
    kvw = A_KV_HEADS * HEAD_DIM
    wq, wk, wv = w_qkv[:, :d], w_qkv[:, d:d + kvw], w_qkv[:, d + kvw:]
    qt, k, vt = _proj_t(x, wq.T.astype(BF16), wk.astype(BF16), wv.T.astype(BF16))
    o, = _band_attention(qt, k, vt, kv_group=N_HEADS // A_KV_HEADS, dist_scale=1.0,
                         max_dist=A_WINDOW - 1, sinks=sinks)
    return [o], []


def _mixer_b(x, w_qkv):
    d = N_HEADS * HEAD_DIM
    ng = len(B_PATTERNS)
    wk = w_qkv[:, ng * d:(ng + 1) * d].astype(BF16)
    wvt = w_qkv[:, (ng + 1) * d:].T.astype(BF16)
    os_, lses = [], []
    for gi, (w, r) in enumerate(B_PATTERNS):
        qt, k, vt = _proj_t(x, w_qkv[:, gi * d:(gi + 1) * d].T.astype(BF16), wk, wvt, dil=r)
        o, lse = _band_attention(qt, k, vt, kv_group=1, dist_scale=float(r), max_dist=w // r, want_lse=True)
        os_.append(o)
        lses.append(lse)
    return os_, lses


def _mixer_c(x, w_qkv):
    d = N_HEADS * HEAD_DIM
    wq, wk, wv = w_qkv[:, :d], w_qkv[:, d:2 * d], w_qkv[:, 2 * d:]
    qt, k, vt, kmean = _proj_c(x, wq.T.astype(BF16), wk.astype(BF16), wv.T.astype(BF16))
    return [_moba_attention(qt, k, vt, kmean.reshape(-1, d))], []


def kernel(x, p, a_w_qkv, a_sinks, a_w_o, b_w_qkv, b_w_o, c_w_qkv, c_w_o, ln1_g, ln1_b, ln2_g, ln2_b,
           peer_w_q, peer_subkeys, peer_u, peer_v, ple_w, ple_gate_w, ple_gate_b):
    depth = p.shape[0]
    alpha = (2 * depth) ** 0.25
    bsz, seq, d = x.shape
    assert bsz == 1 and seq % (B_PATTERNS[-1][1] * BAND) == 0
    xt = x.reshape(seq, d)
    for i in range(depth):
        kind, j = i % 3, i // 3
        if kind == 0:
            os_, lses = _mixer_a(xt, a_w_qkv[j], a_sinks[j])
            wo = a_w_o[j]
        elif kind == 1:
            os_, lses = _mixer_b(xt, b_w_qkv[j])
            wo = b_w_o[j]
        else:
            os_, lses = _mixer_c(xt, c_w_qkv[j])
            wo = c_w_o[j]
        x1, x1b = _outproj_ln(os_, lses, wo.astype(BF16), xt, ln1_g[i], ln1_b[i], alpha)
        wq_hi, wq_lo = _split(peer_w_q[i])
        routing = _router(x1, wq_hi, wq_lo, peer_subkeys[i])
        y = _peer_experts(x1b, _pack_weight(peer_u, i), _pack_weight(peer_v, i), *routing)
        xt = _ln_ple(x1, y, ln2_g[i], ln2_b[i], p.reshape(depth, seq, -1), i, ple_w[i].astype(BF16),
                     ple_gate_w[i].astype(BF16), ple_gate_b[i], alpha)
    return xt.reshape(bsz, seq, d)
```

```python
import functools

import jax
import jax.numpy as jnp
import ml_dtypes
import numpy as np
from jax import lax
from jax.experimental import pallas as pl
from jax.experimental.pallas import tpu as pltpu

F32 = jnp.float32
BF16 = jnp.bfloat16
NEG_INF = float("-inf")
BIG = 1e30
LOG2E = 1.4426950408889634

D_MODEL = 1024
HEAD_DIM = 64
N_HEADS = 16
BAND = 128
A_KV_HEADS = 2
A_WINDOW = 128
B_PATTERNS = ((128, 1), (512, 4), (2048, 16))
C_BLOCK = 256
C_TOPK = 3
PEER_HEADS = 8
PEER_NKEYS = 128
PEER_TOPK = 16
LN_EPS = 1e-5
VMEM_LIMIT = 56 * 1024 * 1024
_SUBLANES = 8


def _params(*sem):
    return pltpu.CompilerParams(dimension_semantics=sem, vmem_limit_bytes=VMEM_LIMIT)


def _dot(a, b):
    return jnp.dot(a, b, preferred_element_type=F32)


def _dot_nt(a, b):
    return lax.dot_general(a, b, (((1,), (1,)), ((), ())), preferred_element_type=F32)


def _split(a):
    hi = a.astype(BF16)
    lo = (a - hi.astype(F32)).astype(BF16)
    return hi, lo


def _alibi_slope(h, n):
    return 2.0 ** (-8.0 * (h + 1) / n)


def _resident(shape):
    zeros = (0,) * len(shape)
    return pl.BlockSpec(shape, lambda *_: zeros)


def _proj_t_kernel(x_ref, wqt_ref, wk_ref, wvt_ref, qt_ref, k_ref, vt_ref, *, q_scale):
    xb = x_ref[...].astype(BF16)
    qt_ref[...] = _pack_rows(_dot_nt(wqt_ref[...], xb) * q_scale)
    k_ref[...] = _pack_rows(_dot(xb, wk_ref[...]))
    vt_ref[...] = _pack_rows(_dot_nt(wvt_ref[...], xb))


def _proj_t(x, wqt, wk, wvt, dil=1, tm=512):
    t, kdim = x.shape
    d, kw = wqt.shape[0], wk.shape[1]
    l = t // dil
    tm = min(tm, l)
    u32 = jnp.uint32
    return pl.pallas_call(
        functools.partial(_proj_t_kernel, q_scale=LOG2E * HEAD_DIM ** -0.5),
        grid=(dil, l // tm),
        in_specs=[pl.BlockSpec((tm, kdim), lambda c, i: (i, c)), _resident((d, kdim)), _resident((kdim, kw)),
                  _resident((kw, kdim))],
        out_specs=[pl.BlockSpec((None, d // 2, tm), lambda c, i: (c, 0, i)),
                   pl.BlockSpec((None, tm // 2, kw), lambda c, i: (c, i, 0)),
                   pl.BlockSpec((None, kw // 2, tm), lambda c, i: (c, 0, i))],
        out_shape=[jax.ShapeDtypeStruct((dil, d // 2, l), u32), jax.ShapeDtypeStruct((dil, l // 2, kw), u32),
                   jax.ShapeDtypeStruct((dil, kw // 2, l), u32)],
        compiler_params=_params("parallel", "parallel"),
        name="proj_band",
    )(x.reshape(l, dil * kdim), wqt, wk, wvt)


def _proj_c_kernel(x_ref, wqt_ref, wk_ref, wvt_ref, qt_ref, k_ref, vt_ref, km_ref, *, nblk):
    xb = x_ref[...].astype(BF16)
    qt_ref[...] = _dot_nt(wqt_ref[...], xb)
    kf = _dot(xb, wk_ref[...])
    k_ref[...] = kf.astype(BF16)
    vt_ref[...] = _dot_nt(wvt_ref[...], xb).astype(BF16)
    for r in range(nblk):
        km_ref[r] = jnp.mean(kf[r * C_BLOCK:(r + 1) * C_BLOCK], axis=0, keepdims=True)


def _proj_c(x, wqt, wk, wvt, tm=512):
    t, k = x.shape
    d = wk.shape[1]
    nblk = tm // C_BLOCK
    return pl.pallas_call(
        functools.partial(_proj_c_kernel, nblk=nblk),
        grid=(t // tm,),
        in_specs=[pl.BlockSpec((tm, k), lambda i: (i, 0)), _resident((d, k)), _resident((k, d)),
                  _resident((d, k))],
        out_specs=[pl.BlockSpec((d, tm), lambda i: (0, i)), pl.BlockSpec((tm, d), lambda i: (i, 0)),
                   pl.BlockSpec((d, tm), lambda i: (0, i)), pl.BlockSpec((nblk, 1, d), lambda i: (i, 0, 0))],
        out_shape=[jax.ShapeDtypeStruct((d, t), F32), jax.ShapeDtypeStruct((t, d), BF16),
                   jax.ShapeDtypeStruct((d, t), BF16), jax.ShapeDtypeStruct((t // C_BLOCK, 1, d), F32)],
        compiler_params=_params("parallel"),
        name="proj_moba",
    )(x, wqt, wk, wvt)


_N_AUG = 16
_PAIR = 2 * HEAD_DIM


def _split3_np(a):
    bf = ml_dtypes.bfloat16
    h = a.astype(bf).astype(np.float32)
    m = (a - h).astype(bf).astype(np.float32)
    l = (a - h - m).astype(bf).astype(np.float32)
    return h, m, l


def _pack_rows_np(a):
    bits = a.astype(ml_dtypes.bfloat16).view(np.uint16).astype(np.uint32)
    return bits[0::2] | (bits[1::2] << 16)


def _band_tables(dist_scale):
    ki = np.arange(2 * BAND, dtype=np.float32)
    kaug = np.zeros((2 * BAND, _PAIR), np.float32)
    kaug[:, 0:3] = 1.0
    kaug[:, 3:6] = ki[:, None]
    qaug = np.zeros((N_HEADS, _N_AUG, BAND), np.float32)
    qpos = np.arange(BAND, dtype=np.float32) + BAND
    for h in range(N_HEADS):
        c = np.float32(_alibi_slope(h, N_HEADS) * dist_scale * LOG2E)
        qaug[h, 0:3] = np.stack(_split3_np(-c * qpos))
        qaug[h, 3:6] = np.stack(_split3_np(np.full((BAND,), c, np.float32)))
    return _pack_rows_np(kaug), np.stack([_pack_rows_np(qaug[h]) for h in range(N_HEADS)])


def _band_kernel(*refs, kv_group, max_dist, use_sinks, want_lse):
    q_ref, kp_ref, ko_ref, vp_ref, vo_ref, kaug_ref, qaug_ref = refs[:7]
    pos = 7
    sink_ref = None
    if use_sinks:
        sink_ref = refs[pos]
        pos += 1
    o_ref = refs[pos]
    lse_ref = refs[pos + 1] if want_lse else None

    b = pl.program_id(1)
    ki = lax.broadcasted_iota(jnp.int32, (2 * BAND, BAND), 0)
    qi = lax.broadcasted_iota(jnp.int32, (2 * BAND, BAND), 1)
    dist = qi + BAND - ki
    valid = (dist >= 0) & (dist <= max_dist) & (ki >= jnp.where(b > 0, 0, BAND))
    qt = pltpu.bitcast(q_ref[...], BF16)
    k_all = jnp.concatenate([pltpu.bitcast(kp_ref[...], BF16), pltpu.bitcast(ko_ref[...], BF16)], axis=0)
    vt_all = jnp.concatenate([pltpu.bitcast(vp_ref[...], BF16), pltpu.bitcast(vo_ref[...], BF16)], axis=1)
    kaug = pltpu.bitcast(kaug_ref[...], BF16)
    zslot = jnp.zeros((HEAD_DIM, BAND), BF16)
    zpad = jnp.zeros((_PAIR - _N_AUG, BAND), BF16)
    n_kv = N_HEADS // kv_group
    scores = [None] * N_HEADS
    for pair in range(-(-n_kv // 2)):
        heads = [h for h in range(N_HEADS) if (h // kv_group) // 2 == pair]
        qcats = []
        for h in heads:
            qh = qt[h * HEAD_DIM:(h + 1) * HEAD_DIM]
            slot = [qh, zslot] if (h // kv_group) % 2 == 0 else [zslot, qh]
            qcats.append(jnp.concatenate(slot + [pltpu.bitcast(qaug_ref[h], BF16), zpad], axis=0))
        kcat = jnp.concatenate([k_all[:, pair * _PAIR:(pair + 1) * _PAIR], kaug], axis=1)
        s_all = _dot(kcat, jnp.concatenate(qcats, axis=1))
        for n, h in enumerate(heads):
            scores[h] = s_all[:, n * BAND:(n + 1) * BAND]
    probs, dens, lses = [], [], []
    for h in range(N_HEADS):
        s = jnp.where(valid, scores[h], NEG_INF)
        m = jnp.max(s, axis=0, keepdims=True)
        if use_sinks:
            sk = sink_ref[:, h:h + 1] * LOG2E
            m = jnp.maximum(m, sk)
        p = jnp.exp2(s - m)
        den = jnp.sum(p, axis=0, keepdims=True)
        if use_sinks:
            den = den + jnp.exp2(sk - m)
        probs.append(p.astype(BF16))
        dens.append(den)
        if want_lse:
            lses.append(jnp.broadcast_to((m + jnp.log2(den)) * (1.0 / LOG2E), (HEAD_DIM, BAND)))
    outs = []
    for g in range(n_kv):
        hs = range(g * kv_group, (g + 1) * kv_group)
        pv = _dot(vt_all[g * HEAD_DIM:(g + 1) * HEAD_DIM], jnp.concatenate([probs[h] for h in hs], axis=1))
        outs += [pv[:, n * BAND:(n + 1) * BAND] / dens[h] for n, h in enumerate(hs)]
    o_ref[...] = jnp.concatenate(outs, axis=0).T.astype(o_ref.dtype)
    if want_lse:
        lse_ref[...] = jnp.concatenate(lses, axis=0).T


def _band_attention(qt, k, vt, *, kv_group, dist_scale, max_dist, sinks=None, want_lse=False):
    r, _, l = qt.shape
    d = N_HEADS * HEAD_DIM
    kw = k.shape[2]
    prev = lambda c, b: jnp.maximum(b - 1, 0)
    kaug, qaug = _band_tables(dist_scale)
    in_specs = [pl.BlockSpec((None, d // 2, BAND), lambda c, b: (c, 0, b)),
                pl.BlockSpec((None, BAND // 2, kw), lambda c, b: (c, prev(c, b), 0)),
                pl.BlockSpec((None, BAND // 2, kw), lambda c, b: (c, b, 0)),
                pl.BlockSpec((None, kw // 2, BAND), lambda c, b: (c, 0, prev(c, b))),
                pl.BlockSpec((None, kw // 2, BAND), lambda c, b: (c, 0, b)),
                _resident(kaug.shape), _resident(qaug.shape)]
    args = [qt, k, k, vt, vt, jnp.asarray(kaug), jnp.asarray(qaug)]
    if sinks is not None:
        in_specs.append(_resident((1, N_HEADS)))
        args.append(sinks.reshape(1, N_HEADS).astype(F32))
    o_spec = pl.BlockSpec((BAND, d), lambda c, b: (b, c))
    out_specs = [o_spec]
    out_shape = [jax.ShapeDtypeStruct((l, r * d), BF16)]
    if want_lse:
        out_specs.append(o_spec)
        out_shape.append(jax.ShapeDtypeStruct((l, r * d), F32))
    outs = pl.pallas_call(
        functools.partial(_band_kernel, kv_group=kv_group, max_dist=max_dist,
                          use_sinks=sinks is not None, want_lse=want_lse),
        grid=(r, l // BAND),
        in_specs=in_specs, out_specs=out_specs, out_shape=out_shape,
        compiler_params=_params("parallel", "parallel"),
        name="band_attention",
    )(*args)
    return [o.reshape(l * r, d) for o in outs]


def _dot3(a, b):
    ah, al = _split(a)
    bh, bl = _split(b)
    return _dot(ah, bh) + (_dot(ah, bl) + _dot(al, bh))


def _dot3_nt(a, b):
    ah, al = _split(a)
    bh, bl = _split(b)
    return _dot_nt(ah, bh) + (_dot_nt(ah, bl) + _dot_nt(al, bh))


def _split3(a):
    h = a.astype(BF16).astype(F32)
    r = a - h
    m = r.astype(BF16).astype(F32)
    l = (r - m).astype(BF16).astype(F32)
    return h, m, l


_KV_STEP = 2 * C_BLOCK


def _moba_kernel(slope_ref, qt_ref, k_ref, vt_ref, km_ref, o_ref, sel_ref, s_ref, p_ref, *, nblk):
    hp = pl.program_id(0)
    qi = pl.program_id(1)
    tq = C_BLOCK
    pw = 2 * HEAD_DIM
    qt = qt_ref[...]
    rows = lax.broadcasted_iota(jnp.int32, (pw, tq), 0)
    blk = lax.broadcasted_iota(jnp.int32, (nblk, tq), 0)
    past = blk < qi
    aug_r = lax.broadcasted_iota(jnp.int32, (_N_AUG, tq), 0)
    tqf = lax.broadcasted_iota(jnp.int32, (_N_AUG, tq), 1).astype(F32)
    kcol = lax.broadcasted_iota(jnp.int32, (_KV_STEP, pw), 1)
    tkf = lax.broadcasted_iota(jnp.int32, (_KV_STEP, pw), 0).astype(F32)
    causal = (lax.broadcasted_iota(jnp.int32, (C_BLOCK, tq), 1)
              >= lax.broadcasted_iota(jnp.int32, (C_BLOCK, tq), 0))
    zpad = jnp.zeros((pw - _N_AUG, tq), BF16)
    k0 = pl.multiple_of(qi * C_BLOCK, C_BLOCK)
    k_own = k_ref[pl.ds(k0, C_BLOCK), :]

    qcat, kpos, c2, init = [], [], [], []
    for hh in range(2):
        c = slope_ref[2 * hp + hh] * LOG2E
        qm = jnp.where((rows >= hh * HEAD_DIM) & (rows < (hh + 1) * HEAD_DIM), qt, 0.0)
        gate = jnp.where(past, _dot3(km_ref[...], qm), NEG_INF)
        g = gate
        for r in range(C_TOPK):
            thr = jnp.max(g, axis=0, keepdims=True)
            if r < C_TOPK - 1:
                g = jnp.where(g >= thr, NEG_INF, g)
        sel_ref[hh] = jnp.where(past & (gate >= thr), 1.0, 0.0)

        qh, qmid, ql = _split3(-c * tqf)
        qpos = jnp.where(aug_r == 0, qh, jnp.where(aug_r == 1, qmid, jnp.where(aug_r == 2, ql,
                         jnp.where(aug_r < 6, 1.0, 0.0))))
        qb = (qm * (LOG2E * HEAD_DIM ** -0.5)).astype(BF16)
        qcat.append(jnp.concatenate([qb, qpos.astype(BF16), zpad], axis=0))
        kh, kmid, kl = _split3(c * tkf)
        kpos.append(jnp.where(kcol < 3, 1.0, jnp.where(kcol == 3, kh, jnp.where(kcol == 4, kmid,
                              jnp.where(kcol == 5, kl, 0.0)))).astype(BF16))
        c2.append(c)

        s = _dot(jnp.concatenate([k_own, kpos[hh][:C_BLOCK]], axis=1), qcat[hh])
        s = jnp.where(causal, s, NEG_INF)
        m0 = jnp.max(s, axis=0, keepdims=True)
        p = jnp.exp2(s - m0)
        l0 = jnp.sum(p, axis=0, keepdims=True)
        acc0 = _dot(vt_ref[pl.ds(hh * HEAD_DIM, HEAD_DIM), pl.ds(k0, C_BLOCK)], p.astype(BF16))
        init.append((m0, l0, acc0))

    nsub = _KV_STEP // C_BLOCK

    nsteps = (qi + nsub - 1) // nsub
    last_step = k_ref.shape[0] // _KV_STEP - 1

    def key_start(n):
        return pl.multiple_of(jnp.clip(n, 0, last_step) * _KV_STEP, _KV_STEP)

    def score_piece(n, slot, hh, j):
        rows = slice(j * C_BLOCK, (j + 1) * C_BLOCK)
        kb = k_ref[pl.ds(key_start(n) + j * C_BLOCK, C_BLOCK), :]
        s_ref[slot, hh, rows, :] = _dot(jnp.concatenate([kb, kpos[hh][rows]], axis=1), qcat[hh])

    def scores(n, slot):
        for hh in range(2):
            for j in range(nsub):
                score_piece(n, slot, hh, j)

    def pv(n, hh, slot):
        return _dot(vt_ref[pl.ds(hh * HEAD_DIM, HEAD_DIM), pl.ds(key_start(n), _KV_STEP)],
                    p_ref[slot, hh])

    def step(n, slot, carry):
        a_prev, state = carry
        off = (qi * C_BLOCK - n * _KV_STEP).astype(F32)
        mxu = [functools.partial(score_piece, n + 1, 1 - slot, hh, j) for hh in range(2) for j in range(nsub)]
        new_a, new_state = [], []
        for hh in range(2):
            m_i, l_i, acc = state[hh]
            shift = c2[hh] * off
            sel = [sel_ref[hh, pl.ds(nsub * n + j, 1), :] > 0.5 for j in range(nsub)]
            m_new = m_i
            for j in range(nsub):
                mxu.pop(0)()
                mj = jnp.max(s_ref[slot, hh, j * C_BLOCK:(j + 1) * C_BLOCK, :], axis=0, keepdims=True)
                m_new = jnp.maximum(m_new, jnp.where(sel[j], mj - shift, NEG_INF))
            acc = a_prev[hh] * acc + pv(n - 1, hh, 1 - slot)
            l_new = jnp.exp2(m_i - m_new) * l_i
            for j in range(nsub):
                rows = slice(j * C_BLOCK, (j + 1) * C_BLOCK)
                p = jnp.exp2(s_ref[slot, hh, rows, :] - jnp.where(sel[j], m_new + shift, BIG))
                p_ref[slot, hh, rows, :] = p.astype(BF16)
                l_new = l_new + jnp.sum(p, axis=0, keepdims=True)
            new_a.append(jnp.exp2(m_i - m_new))
            new_state.append((m_new, l_new, acc))
        return tuple(new_a), tuple(new_state)

    def body(n2, carry):
        return step(2 * n2 + 1, 1, step(2 * n2, 0, carry))

    p_ref[1] = jnp.zeros_like(p_ref[1])
    scores(0, 0)
    one_a = jnp.ones((1, tq), F32)
    npairs = (nsteps + 1) // 2
    a_prev, state = lax.fori_loop(0, npairs, body, ((one_a, one_a), tuple(init)))
    outs = []
    for hh in range(2):
        _, l, acc = state[hh]
        outs.append((a_prev[hh] * acc + pv(2 * npairs - 1, hh, 1)) / l)
    o_ref[...] = jnp.concatenate(outs, axis=0).T.astype(o_ref.dtype)


def _moba_attention(qt, k, vt, kmean):
    d, t = qt.shape
    nblk = t // C_BLOCK
    slopes = jnp.asarray([_alibi_slope(h, N_HEADS) for h in range(N_HEADS)], F32)
    pw = 2 * HEAD_DIM
    assert t % (2 * _KV_STEP) == 0
    return pl.pallas_call(
        functools.partial(_moba_kernel, nblk=nblk),
        grid=(d // pw, nblk),
        in_specs=[pl.BlockSpec(memory_space=pltpu.SMEM),
                  pl.BlockSpec((pw, C_BLOCK), lambda hp, i: (hp, i)),
                  pl.BlockSpec((t, pw), lambda hp, i: (0, hp)),
                  pl.BlockSpec((pw, t), lambda hp, i: (hp, 0)),
                  pl.BlockSpec((nblk, pw), lambda hp, i: (0, hp))],
        out_specs=pl.BlockSpec((C_BLOCK, pw), lambda hp, i: (i, hp)),
        out_shape=jax.ShapeDtypeStruct((t, d), BF16),
        scratch_shapes=[pltpu.VMEM((2, nblk, C_BLOCK), F32), pltpu.VMEM((2, 2, _KV_STEP, C_BLOCK), F32),
                        pltpu.VMEM((2, 2, _KV_STEP, C_BLOCK), BF16)],
        compiler_params=_params("parallel", "arbitrary"),
        name="moba_attention",
    )(slopes, qt, k, vt, kmean)


def _layer_norm(z, g, b):
    mu = jnp.mean(z, axis=-1, keepdims=True)
    zc = z - mu
    var = jnp.mean(zc * zc, axis=-1, keepdims=True)
    return zc * lax.rsqrt(var + LN_EPS) * g + b


def _outproj_ln_kernel(*refs, n_groups, alpha):
    o_refs = refs[:n_groups]
    lse_refs = refs[n_groups:2 * n_groups] if n_groups > 1 else ()
    pos = n_groups + len(lse_refs)
    wo_ref, x_ref, g_ref, b_ref, xn_ref, xb_ref = refs[pos:pos + 6]
    if n_groups == 1:
        o = o_refs[0][...]
    else:
        lses = [r[...] for r in lse_refs]
        m = functools.reduce(jnp.maximum, lses)
        es = [jnp.exp(l - m) for l in lses]
        den = functools.reduce(jnp.add, es)
        o = functools.reduce(jnp.add, [(e / den) * r[...].astype(F32) for e, r in zip(es, o_refs)])
        o = o.astype(BF16)
    y = _dot(o, wo_ref[...])
    xn = _layer_norm(alpha * x_ref[...] + y, g_ref[...], b_ref[...])
    xn_ref[...] = xn
    xb_ref[...] = _pack_rows(xn)


def _outproj_ln(os_, lses, wo, x, g, b, alpha, tm=512):
    t, d = x.shape
    n = len(os_)
    tile = pl.BlockSpec((tm, d), lambda i: (i, 0))
    return pl.pallas_call(
        functools.partial(_outproj_ln_kernel, n_groups=n, alpha=alpha),
        grid=(t // tm,),
        in_specs=[tile] * (n + len(lses)) + [_resident((d, d)), tile, _resident((1, d)), _resident((1, d))],
        out_specs=[tile, pl.BlockSpec((tm // 2, d), lambda i: (i, 0))],
        out_shape=[jax.ShapeDtypeStruct((t, d), F32), jax.ShapeDtypeStruct((t // 2, d), jnp.uint32)],
        compiler_params=_params("parallel"),
        name="outproj_ln",
    )(*os_, *lses, wo, x, g.reshape(1, d), b.reshape(1, d))


def _pack_rows(a):
    return pltpu.bitcast(a.astype(BF16), jnp.uint32)


def _pack_dup(a):
    bits = pltpu.bitcast(a.astype(BF16).astype(F32), jnp.uint32)
    return bits | (bits >> 16)


def _pack_weight_kernel(w_ref, o_ref):
    o_ref[...] = _pack_rows(w_ref[...])


def _pack_weight(w, layer, tr=2048):
    _, r, c = w.shape
    return pl.pallas_call(
        _pack_weight_kernel,
        grid=(r // tr,),
        in_specs=[pl.BlockSpec((None, tr, c), lambda i: (layer, i, 0))],
        out_specs=pl.BlockSpec((tr // 2, c), lambda i: (i, 0)),
        out_shape=jax.ShapeDtypeStruct((r // 2, c), jnp.uint32),
        compiler_params=_params("parallel"),
        name="pack_weight",
    )(w)


def _oddeven_merge(lo, hi, r):
    step = r * 2
    if step < hi - lo:
        yield from _oddeven_merge(lo, hi, step)
        yield from _oddeven_merge(lo + r, hi, step)
        yield from [(i, i + r) for i in range(lo + r, hi - r, step)]
    else:
        yield (lo, lo + r)


def _oddeven_sort(lo, hi):
    if hi > lo:
        mid = lo + (hi - lo) // 2
        yield from _oddeven_sort(lo, mid)
        yield from _oddeven_sort(mid + 1, hi)
        yield from _oddeven_merge(lo, hi, 1)


_SORT16 = tuple(_oddeven_sort(0, PEER_TOPK - 1))


def _top16_sorted(s):
    n = PEER_TOPK
    x = [s[_SUBLANES * j:_SUBLANES * (j + 1)] for j in range(n)]

    def cmpx(i, j):
        x[i], x[j] = jnp.maximum(x[i], x[j]), jnp.minimum(x[i], x[j])

    for i, j in _SORT16:
        cmpx(i, j)
    for shift in (4, 2, 1):
        y = [pltpu.roll(v, shift, 0) for v in x]
        x = [jnp.maximum(x[i], y[n - 1 - i]) for i in range(n)]
        stride = n // 2
        while stride:
            for i in range(n):
                if not i & stride:
                    cmpx(i, i + stride)
            stride //= 2
    top = [v[0:1] for v in x]
    nxt = jnp.max(jnp.where(s < top[n - 1], s, NEG_INF), axis=0, keepdims=True)
    return top + [nxt]


def _top_vals(s, k):
    out = []
    for r in range(k):
        m = jnp.max(s, axis=0, keepdims=True)
        out.append(m)
        if r < k - 1:
            s = jnp.where(s >= m, NEG_INF, s)
    return out


_CAND = [(i, j) for i in range(PEER_TOPK + 1) for j in range(PEER_TOPK + 1)
         if (i + 1) * (j + 1) <= PEER_TOPK + 1]
_NCAND = -(-len(_CAND) // 8) * 8


def _router_kernel(x_ref, wq_ref, sk_ref, r2_ref, e2_ref, n1_ref, w_ref, q_ref, cand_ref):
    q_ref[...] = _dot(x_ref[...].astype(BF16), wq_ref[...])
    tt = x_ref.shape[0]
    cand_ref[...] = jnp.full((_NCAND, tt), NEG_INF, F32)
    nk = PEER_NKEYS
    for h in range(PEER_HEADS):
        s1 = _dot3_nt(sk_ref[0], q_ref[:, (2 * h) * nk:(2 * h + 1) * nk])
        s2 = _dot3_nt(sk_ref[1], q_ref[:, (2 * h + 1) * nk:(2 * h + 2) * nk])
        a = _top16_sorted(s1)
        b = _top16_sorted(s2)
        for r, (i, j) in enumerate(_CAND):
            cand_ref[r:r + 1, :] = a[i] + b[j]
        v = _top_vals(cand_ref[...], PEER_TOPK + 1)
        thr = 0.5 * (v[PEER_TOPK - 1] + v[PEER_TOPK])
        z = functools.reduce(jnp.add, [jnp.exp(v[r] - v[0]) for r in range(PEER_TOPK)])
        cut = thr - s1
        rank2 = jnp.zeros_like(s2)
        count1 = jnp.zeros_like(s1)
        for r in range(PEER_TOPK):
            rank2 = jnp.where(s2 < b[r], r + 1.0, rank2)
            count1 = jnp.where(b[r] >= cut, r + 1.0, count1)
        r2_ref[h] = _pack_rows(rank2)
        e2_ref[h] = _pack_rows(jnp.exp(s2 - b[0]))
        n1_ref[h] = _pack_dup(count1)
        w_ref[h] = _pack_dup(jnp.exp(s1 - a[0]) / z)


def _router(xn, wq, subkeys, tt=256):
    t, d = xn.shape
    nq = wq.shape[1]
    shape = (PEER_HEADS, PEER_NKEYS, t)
    ospec = pl.BlockSpec((PEER_HEADS, PEER_NKEYS, tt), lambda i: (0, 0, i))
    pspec = pl.BlockSpec((PEER_HEADS, PEER_NKEYS // 2, tt), lambda i: (0, 0, i))
    packed = jax.ShapeDtypeStruct((PEER_HEADS, PEER_NKEYS // 2, t), jnp.uint32)
    return pl.pallas_call(
        _router_kernel,
        grid=(t // tt,),
        in_specs=[pl.BlockSpec((tt, d), lambda i: (i, 0)), _resident((d, nq)), _resident(subkeys.shape)],
        out_specs=[pspec, pspec, ospec, ospec],
        out_shape=[packed, packed, jax.ShapeDtypeStruct(shape, jnp.uint32),
                   jax.ShapeDtypeStruct(shape, jnp.uint32)],
        scratch_shapes=[pltpu.VMEM((tt, nq), F32), pltpu.VMEM((_NCAND, tt), F32)],
        compiler_params=_params("parallel"),
        name="peer_router",
    )(xn, wq, subkeys)


def _gelu(x):
    return 0.5 * x * (1.0 + lax.erf(x * (2.0 ** -0.5)))


_BF16_ROWS = 16
_PEER_SUB = 1024
_PEER_COLS = 256
_PEER_KQ = 512


def _peer_kernel(x_ref, u_ref, v_ref, r2_ref, e2_ref, n1_ref, w_ref, y_ref, acc_ref, h_ref, act_ref,
                 *, n_tiles, n_e):
    s = pl.program_id(0)
    _, te, tt = h_ref.shape
    nk = PEER_NKEYS
    j_c = jnp.clip(s - 2, 0, n_tiles - 1) % n_e

    @pl.when(s == 0)
    def _():
        h_ref[...] = jnp.zeros_like(h_ref)
        act_ref[...] = jnp.zeros_like(act_ref)

    @pl.when(j_c == 0)
    def _():
        acc_ref[...] = jnp.zeros_like(acc_ref)

    n_i1 = te // nk

    def first_matmul(slot, k):
        xb = pltpu.bitcast(x_ref[...], BF16)
        uk = pltpu.bitcast(u_ref[pl.ds(k * _PEER_SUB // 2, _PEER_SUB // 2), :], BF16)
        h_ref[slot, pl.ds(k * _PEER_SUB, _PEER_SUB), :] = _dot_nt(uk, xb)

    def second_matmul(slot, n, kq):
        cols = pl.ds(n * _PEER_COLS, _PEER_COLS)
        vq = pltpu.bitcast(v_ref[pl.ds(kq * _PEER_KQ // 2, _PEER_KQ // 2), :], BF16)
        acc_ref[:, cols] += lax.dot_general(vq, act_ref[slot, pl.ds(kq * _PEER_KQ, _PEER_KQ), cols],
                                            (((0,), (0,)), ((), ())), preferred_element_type=F32)

    def gate(slot, c, v):
        cols = pl.ds(c * 128, 128)
        prow = pl.ds(v * _SUBLANES, _SUBLANES)
        g = [jnp.zeros((_BF16_ROWS, 128), BF16) for _ in range(n_i1)]
        for h in range(PEER_HEADS):
            r2 = pltpu.bitcast(r2_ref[h, prow, cols], BF16)
            e2 = pltpu.bitcast(e2_ref[h, prow, cols], BF16)
            for ii in range(n_i1):
                row = pl.ds(ii, 1)
                n_row = pltpu.bitcast(jnp.broadcast_to(n1_ref[h, row, cols], (_SUBLANES, 128)), BF16)
                w_row = pltpu.bitcast(jnp.broadcast_to(w_ref[h, row, cols], (_SUBLANES, 128)), BF16)
                g[ii] = g[ii] + jnp.where(r2 < n_row, e2, 0.0) * w_row
        for ii in range(n_i1):
            rows = pl.ds(ii * nk + v * _BF16_ROWS, _BF16_ROWS)
            act_ref[slot, rows, cols] = _gelu(h_ref[slot, rows, cols]).astype(BF16) * g[ii]

    def stages(slot):
        pieces = [functools.partial(first_matmul, slot, k) for k in range(te // _PEER_SUB)]
        pieces += [functools.partial(second_matmul, slot, n, kq)
                   for n in range(tt // _PEER_COLS) for kq in range(te // _PEER_KQ)]
        blocks = [(c, v) for c in range(tt // 128) for v in range(nk // _BF16_ROWS)]
        per = -(-len(blocks) // len(pieces))
        for p, piece in enumerate(pieces):
            piece()
            for c, v in blocks[p * per:(p + 1) * per]:
                gate(1 - slot, c, v)

    @pl.when(s % 2 == 0)
    def _():
        stages(0)

    @pl.when(s % 2 == 1)
    def _():
        stages(1)

    @pl.when(j_c == n_e - 1)
    def _():
        y_ref[...] = acc_ref[...].T


def _peer_experts(xb, u, v, r2, e2, n1, w, tt=512, te=2048):
    t, d = 2 * xb.shape[0], xb.shape[1]
    ne = 2 * u.shape[0]
    n_e = ne // te
    n_tiles = (t // tt) * n_e
    pair = lambda lag: (lambda s: jnp.clip(s - lag, 0, n_tiles - 1))
    tok = lambda lag: (lambda s: pair(lag)(s) // n_e)
    exp = lambda lag: (lambda s: pair(lag)(s) % n_e)
    wspec = lambda lag: pl.BlockSpec((te // 2, d), lambda s: (exp(lag)(s), 0))
    pspec = pl.BlockSpec((PEER_HEADS, PEER_NKEYS // 2, tt), lambda s: (0, 0, tok(1)(s)))
    rspec = pl.BlockSpec((PEER_HEADS, te // PEER_NKEYS, tt), lambda s: (0, exp(1)(s), tok(1)(s)))
    return pl.pallas_call(
        functools.partial(_peer_kernel, n_tiles=n_tiles, n_e=n_e),
        grid=(n_tiles + 2,),
        in_specs=[pl.BlockSpec((tt // 2, d), lambda s: (tok(0)(s), 0)), wspec(0), wspec(2),
                  pspec, pspec, rspec, rspec],
        out_specs=pl.BlockSpec((tt, d), lambda s: (tok(2)(s), 0)),
        out_shape=jax.ShapeDtypeStruct((t, d), F32),
        scratch_shapes=[pltpu.VMEM((d, tt), F32), pltpu.VMEM((2, te, tt), F32), pltpu.VMEM((2, te, tt), BF16)],
        compiler_params=_params("arbitrary"),
        name="peer_experts",
    )(xb, u, v, r2, e2, n1, w)


def _ln_ple_kernel(x_ref, y_ref, g_ref, b_ref, p_ref, wp_ref, wg_ref, bg_ref, o_ref, *, alpha):
    xn = _layer_norm(alpha * x_ref[...] + y_ref[...], g_ref[...], b_ref[...])
    gate = jax.nn.sigmoid(_dot(xn.astype(BF16), wg_ref[...]) + bg_ref[...])
    o_ref[...] = xn + gate * _dot(p_ref[...].astype(BF16), wp_ref[...])


def _ln_ple(x, y, g, b, p, layer, wp, wg, bg, alpha, tm=512):
    t, d = x.shape
    dp = p.shape[2]
    tile = pl.BlockSpec((tm, d), lambda i: (i, 0))
    vec = _resident((1, d))
    return pl.pallas_call(
        functools.partial(_ln_ple_kernel, alpha=alpha),
        grid=(t // tm,),
        in_specs=[tile, tile, vec, vec, pl.BlockSpec((None, tm, dp), lambda i: (layer, i, 0)), _resident((dp, d)),
                  _resident((d, d)), vec],
        out_specs=tile,
        out_shape=jax.ShapeDtypeStruct((t, d), F32),
        compiler_params=_params("parallel"),
        name="ln_ple",
    )(x, y, g.reshape(1, d), b.reshape(1, d), p, wp, wg, bg.reshape(1, d))


def _mixer_a(x, w_qkv, sinks):
    d = N_HEADS * HEAD_DIM
    kvw = A_KV_HEADS * HEAD_DIM
    wq, wk, wv = w_qkv[:, :d], w_qkv[:, d:d + kvw], w_qkv[:, d + kvw:]
    qt, k, vt = _proj_t(x, wq.T.astype(BF16), wk.astype(BF16), wv.T.astype(BF16))
    o, = _band_attention(qt, k, vt, kv_group=N_HEADS // A_KV_HEADS, dist_scale=1.0,
                         max_dist=A_WINDOW - 1, sinks=sinks)
    return [o], []


def _mixer_b(x, w_qkv):
    d = N_HEADS * HEAD_DIM
    ng = len(B_PATTERNS)
    wk = w_qkv[:, ng * d:(ng + 1) * d].astype(BF16)
    wvt = w_qkv[:, (ng + 1) * d:].T.astype(BF16)
    os_, lses = [], []
    for gi, (w, r) in enumerate(B_PATTERNS):
        qt, k, vt = _proj_t(x, w_qkv[:, gi * d:(gi + 1) * d].T.astype(BF16), wk, wvt, dil=r)
        o, lse = _band_attention(qt, k, vt, kv_group=1, dist_scale=float(r), max_dist=w // r, want_lse=True)
        os_.append(o)
        lses.append(lse)
    return os_, lses


def _mixer_c(x, w_qkv):
    d = N_HEADS * HEAD_DIM
    wq, wk, wv = w_qkv[:, :d], w_qkv[:, d:2 * d], w_qkv[:, 2 * d:]
    qt, k, vt, kmean = _proj_c(x, wq.T.astype(BF16), wk.astype(BF16), wv.T.astype(BF16))
    return [_moba_attention(qt, k, vt, kmean.reshape(-1, d))], []


def kernel(x, p, a_w_qkv, a_sinks, a_w_o, b_w_qkv, b_w_o, c_w_qkv, c_w_o, ln1_g, ln1_b, ln2_g, ln2_b,
           peer_w_q, peer_subkeys, peer_u, peer_v, ple_w, ple_gate_w, ple_gate_b):
    depth = p.shape[0]
    alpha = (2 * depth) ** 0.25
    bsz, seq, d = x.shape
    assert bsz == 1 and seq % (B_PATTERNS[-1][1] * BAND) == 0
    xt = x.reshape(seq, d)
    for i in range(depth):
        kind, j = i % 3, i // 3
        if kind == 0:
            os_, lses = _mixer_a(xt, a_w_qkv[j], a_sinks[j])
            wo = a_w_o[j]
        elif kind == 1:
            os_, lses = _mixer_b(xt, b_w_qkv[j])
            wo = b_w_o[j]
        else:
            os_, lses = _mixer_c(xt, c_w_qkv[j])
            wo = c_w_o[j]
        x1, x1b = _outproj_ln(os_, lses, wo.astype(BF16), xt, ln1_g[i], ln1_b[i], alpha)
        routing = _router(x1, peer_w_q[i].astype(BF16), peer_subkeys[i])
        y = _peer_experts(x1b, _pack_weight(peer_u, i), _pack_weight(peer_v, i), *routing)
        xt = _ln_ple(x1, y, ln2_g[i], ln2_b[i], p.reshape(depth, seq, -1), i, ple_w[i].astype(BF16),
                     ple_gate_w[i].astype(BF16), ple_gate_b[i], alpha)
    return xt.reshape(bsz, seq, d)
```

```python
import functools

import jax
import jax.numpy as jnp
import ml_dtypes
import numpy as np
from jax import lax
from jax.experimental import pallas as pl
from jax.experimental.pallas import tpu as pltpu

F32 = jnp.float32
BF16 = jnp.bfloat16
NEG_INF = float("-inf")
BIG = 1e30
LOG2E = 1.4426950408889634

D_MODEL = 1024
HEAD_DIM = 64
N_HEADS = 16
BAND = 128
A_KV_HEADS = 2
A_WINDOW = 128
B_PATTERNS = ((128, 1), (512, 4), (2048, 16))
C_BLOCK = 256
C_TOPK = 3
PEER_HEADS = 8
PEER_NKEYS = 128
PEER_TOPK = 16
LN_EPS = 1e-5
VMEM_LIMIT = 56 * 1024 * 1024
_SUBLANES = 8


def _params(*sem):
    return pltpu.CompilerParams(dimension_semantics=sem, vmem_limit_bytes=VMEM_LIMIT)


def _dot(a, b):
    return jnp.dot(a, b, preferred_element_type=F32)


def _dot_nt(a, b):
    return lax.dot_general(a, b, (((1,), (1,)), ((), ())), preferred_element_type=F32)


def _split(a):
    hi = a.astype(BF16)
    lo = (a - hi.astype(F32)).astype(BF16)
    return hi, lo


def _alibi_slope(h, n):
    return 2.0 ** (-8.0 * (h + 1) / n)


def _resident(shape):
    zeros = (0,) * len(shape)
    return pl.BlockSpec(shape, lambda *_: zeros)


def _proj_t_kernel(x_ref, wqt_ref, wk_ref, wvt_ref, qt_ref, k_ref, vt_ref, *, q_scale):
    xb = x_ref[...].astype(BF16)
    qt_ref[...] = _pack_rows(_dot_nt(wqt_ref[...], xb) * q_scale)
    k_ref[...] = _pack_rows(_dot(xb, wk_ref[...]))
    vt_ref[...] = _pack_rows(_dot_nt(wvt_ref[...], xb))


def _proj_t(x, wqt, wk, wvt, dil=1, tm=512):
    t, kdim = x.shape
    d, kw = wqt.shape[0], wk.shape[1]
    l = t // dil
    tm = min(tm, l)
    u32 = jnp.uint32
    return pl.pallas_call(
        functools.partial(_proj_t_kernel, q_scale=LOG2E * HEAD_DIM ** -0.5),
        grid=(dil, l // tm),
        in_specs=[pl.BlockSpec((tm, kdim), lambda c, i: (i, c)), _resident((d, kdim)), _resident((kdim, kw)),
                  _resident((kw, kdim))],
        out_specs=[pl.BlockSpec((None, d // 2, tm), lambda c, i: (c, 0, i)),
                   pl.BlockSpec((None, tm // 2, kw), lambda c, i: (c, i, 0)),
                   pl.BlockSpec((None, kw // 2, tm), lambda c, i: (c, 0, i))],
        out_shape=[jax.ShapeDtypeStruct((dil, d // 2, l), u32), jax.ShapeDtypeStruct((dil, l // 2, kw), u32),
                   jax.ShapeDtypeStruct((dil, kw // 2, l), u32)],
        compiler_params=_params("parallel", "parallel"),
        name="proj_band",
    )(x.reshape(l, dil * kdim), wqt, wk, wvt)


def _proj_c_kernel(x_ref, wqt_ref, wk_ref, wvt_ref, qt_ref, k_ref, vt_ref, km_ref, *, nblk):
    xb = x_ref[...].astype(BF16)
    qt_ref[...] = _dot_nt(wqt_ref[...], xb)
    kf = _dot(xb, wk_ref[...])
    k_ref[...] = kf.astype(BF16)
    vt_ref[...] = _dot_nt(wvt_ref[...], xb).astype(BF16)
    for r in range(nblk):
        km_ref[r] = jnp.mean(kf[r * C_BLOCK:(r + 1) * C_BLOCK], axis=0, keepdims=True)


def _proj_c(x, wqt, wk, wvt, tm=512):
    t, k = x.shape
    d = wk.shape[1]
    nblk = tm // C_BLOCK
    return pl.pallas_call(
        functools.partial(_proj_c_kernel, nblk=nblk),
        grid=(t // tm,),
        in_specs=[pl.BlockSpec((tm, k), lambda i: (i, 0)), _resident((d, k)), _resident((k, d)),
                  _resident((d, k))],
        out_specs=[pl.BlockSpec((d, tm), lambda i: (0, i)), pl.BlockSpec((tm, d), lambda i: (i, 0)),
                   pl.BlockSpec((d, tm), lambda i: (0, i)), pl.BlockSpec((nblk, 1, d), lambda i: (i, 0, 0))],
        out_shape=[jax.ShapeDtypeStruct((d, t), F32), jax.ShapeDtypeStruct((t, d), BF16),
                   jax.ShapeDtypeStruct((d, t), BF16), jax.ShapeDtypeStruct((t // C_BLOCK, 1, d), F32)],
        compiler_params=_params("parallel"),
        name="proj_moba",
    )(x, wqt, wk, wvt)


_N_AUG = 16
_PAIR = 2 * HEAD_DIM


def _split3_np(a):
    bf = ml_dtypes.bfloat16
    h = a.astype(bf).astype(np.float32)
    m = (a - h).astype(bf).astype(np.float32)
    l = (a - h - m).astype(bf).astype(np.float32)
    return h, m, l


def _pack_rows_np(a):
    bits = a.astype(ml_dtypes.bfloat16).view(np.uint16).astype(np.uint32)
    return bits[0::2] | (bits[1::2] << 16)


def _band_tables(dist_scale):
    ki = np.arange(2 * BAND, dtype=np.float32)
    kaug = np.zeros((2 * BAND, _PAIR), np.float32)
    kaug[:, 0:3] = 1.0
    kaug[:, 3:6] = ki[:, None]
    qaug = np.zeros((N_HEADS, _N_AUG, BAND), np.float32)
    qpos = np.arange(BAND, dtype=np.float32) + BAND
    for h in range(N_HEADS):
        c = np.float32(_alibi_slope(h, N_HEADS) * dist_scale * LOG2E)
        qaug[h, 0:3] = np.stack(_split3_np(-c * qpos))
        qaug[h, 3:6] = np.stack(_split3_np(np.full((BAND,), c, np.float32)))
    return _pack_rows_np(kaug), np.stack([_pack_rows_np(qaug[h]) for h in range(N_HEADS)])


def _band_kernel(*refs, kv_group, max_dist, use_sinks, want_lse):
    q_ref, kp_ref, ko_ref, vp_ref, vo_ref, kaug_ref, qaug_ref = refs[:7]
    pos = 7
    sink_ref = None
    if use_sinks:
        sink_ref = refs[pos]
        pos += 1
    o_ref = refs[pos]
    lse_ref = refs[pos + 1] if want_lse else None

    b = pl.program_id(1)
    ki = lax.broadcasted_iota(jnp.int32, (2 * BAND, BAND), 0)
    qi = lax.broadcasted_iota(jnp.int32, (2 * BAND, BAND), 1)
    dist = qi + BAND - ki
    valid = (dist >= 0) & (dist <= max_dist) & (ki >= jnp.where(b > 0, 0, BAND))
    qt = pltpu.bitcast(q_ref[...], BF16)
    k_all = jnp.concatenate([pltpu.bitcast(kp_ref[...], BF16), pltpu.bitcast(ko_ref[...], BF16)], axis=0)
    vt_all = jnp.concatenate([pltpu.bitcast(vp_ref[...], BF16), pltpu.bitcast(vo_ref[...], BF16)], axis=1)
    kaug = pltpu.bitcast(kaug_ref[...], BF16)
    zslot = jnp.zeros((HEAD_DIM, BAND), BF16)
    zpad = jnp.zeros((_PAIR - _N_AUG, BAND), BF16)
    n_kv = N_HEADS // kv_group
    scores = [None] * N_HEADS
    for pair in range(-(-n_kv // 2)):
        heads = [h for h in range(N_HEADS) if (h // kv_group) // 2 == pair]
        qcats = []
        for h in heads:
            qh = qt[h * HEAD_DIM:(h + 1) * HEAD_DIM]
            slot = [qh, zslot] if (h // kv_group) % 2 == 0 else [zslot, qh]
            qcats.append(jnp.concatenate(slot + [pltpu.bitcast(qaug_ref[h], BF16), zpad], axis=0))
        kcat = jnp.concatenate([k_all[:, pair * _PAIR:(pair + 1) * _PAIR], kaug], axis=1)
        s_all = _dot(kcat, jnp.concatenate(qcats, axis=1))
        for n, h in enumerate(heads):
            scores[h] = s_all[:, n * BAND:(n + 1) * BAND]
    probs, dens, lses = [], [], []
    for h in range(N_HEADS):
        s = jnp.where(valid, scores[h], NEG_INF)
        m = jnp.max(s, axis=0, keepdims=True)
        if use_sinks:
            sk = sink_ref[:, h:h + 1] * LOG2E
            m = jnp.maximum(m, sk)
        p = jnp.exp2(s - m)
        den = jnp.sum(p, axis=0, keepdims=True)
        if use_sinks:
            den = den + jnp.exp2(sk - m)
        probs.append(p.astype(BF16))
        dens.append(den)
        if want_lse:
            lses.append(jnp.broadcast_to((m + jnp.log2(den)) * (1.0 / LOG2E), (HEAD_DIM, BAND)))
    outs = []
    for g in range(n_kv):
        hs = range(g * kv_group, (g + 1) * kv_group)
        pv = _dot(vt_all[g * HEAD_DIM:(g + 1) * HEAD_DIM], jnp.concatenate([probs[h] for h in hs], axis=1))
        outs += [pv[:, n * BAND:(n + 1) * BAND] / dens[h] for n, h in enumerate(hs)]
    o_ref[...] = jnp.concatenate(outs, axis=0).T.astype(o_ref.dtype)
    if want_lse:
        lse_ref[...] = jnp.concatenate(lses, axis=0).T


def _band_attention(qt, k, vt, *, kv_group, dist_scale, max_dist, sinks=None, want_lse=False):
    r, _, l = qt.shape
    d = N_HEADS * HEAD_DIM
    kw = k.shape[2]
    prev = lambda c, b: jnp.maximum(b - 1, 0)
    kaug, qaug = _band_tables(dist_scale)
    in_specs = [pl.BlockSpec((None, d // 2, BAND), lambda c, b: (c, 0, b)),
                pl.BlockSpec((None, BAND // 2, kw), lambda c, b: (c, prev(c, b), 0)),
                pl.BlockSpec((None, BAND // 2, kw), lambda c, b: (c, b, 0)),
                pl.BlockSpec((None, kw // 2, BAND), lambda c, b: (c, 0, prev(c, b))),
                pl.BlockSpec((None, kw // 2, BAND), lambda c, b: (c, 0, b)),
                _resident(kaug.shape), _resident(qaug.shape)]
    args = [qt, k, k, vt, vt, jnp.asarray(kaug), jnp.asarray(qaug)]
    if sinks is not None:
        in_specs.append(_resident((1, N_HEADS)))
        args.append(sinks.reshape(1, N_HEADS).astype(F32))
    o_spec = pl.BlockSpec((BAND, d), lambda c, b: (b, c))
    out_specs = [o_spec]
    out_shape = [jax.ShapeDtypeStruct((l, r * d), BF16)]
    if want_lse:
        out_specs.append(o_spec)
        out_shape.append(jax.ShapeDtypeStruct((l, r * d), F32))
    outs = pl.pallas_call(
        functools.partial(_band_kernel, kv_group=kv_group, max_dist=max_dist,
                          use_sinks=sinks is not None, want_lse=want_lse),
        grid=(r, l // BAND),
        in_specs=in_specs, out_specs=out_specs, out_shape=out_shape,
        compiler_params=_params("parallel", "parallel"),
        name="band_attention",
    )(*args)
    return [o.reshape(l * r, d) for o in outs]


def _dot3(a, b):
    ah, al = _split(a)
    bh, bl = _split(b)
    return _dot(ah, bh) + (_dot(ah, bl) + _dot(al, bh))


def _dot3_nt(a, b):
    ah, al = _split(a)
    bh, bl = _split(b)
    return _dot_nt(ah, bh) + (_dot_nt(ah, bl) + _dot_nt(al, bh))


def _split3(a):
    h = a.astype(BF16).astype(F32)
    r = a - h
    m = r.astype(BF16).astype(F32)
    l = (r - m).astype(BF16).astype(F32)
    return h, m, l


_KV_STEP = 2 * C_BLOCK


def _moba_kernel(slope_ref, qt_ref, k_ref, vt_ref, km_ref, o_ref, sel_ref, s_ref, p_ref, *, nblk):
    hp = pl.program_id(0)
    qi = pl.program_id(1)
    tq = C_BLOCK
    pw = 2 * HEAD_DIM
    qt = qt_ref[...]
    rows = lax.broadcasted_iota(jnp.int32, (pw, tq), 0)
    blk = lax.broadcasted_iota(jnp.int32, (nblk, tq), 0)
    past = blk < qi
    aug_r = lax.broadcasted_iota(jnp.int32, (_N_AUG, tq), 0)
    tqf = lax.broadcasted_iota(jnp.int32, (_N_AUG, tq), 1).astype(F32)
    kcol = lax.broadcasted_iota(jnp.int32, (_KV_STEP, pw), 1)
    tkf = lax.broadcasted_iota(jnp.int32, (_KV_STEP, pw), 0).astype(F32)
    causal = (lax.broadcasted_iota(jnp.int32, (C_BLOCK, tq), 1)
              >= lax.broadcasted_iota(jnp.int32, (C_BLOCK, tq), 0))
    zpad = jnp.zeros((pw - _N_AUG, tq), BF16)
    k0 = pl.multiple_of(qi * C_BLOCK, C_BLOCK)
    k_own = k_ref[pl.ds(k0, C_BLOCK), :]

    qcat, kpos, c2, init = [], [], [], []
    for hh in range(2):
        c = slope_ref[2 * hp + hh] * LOG2E
        qm = jnp.where((rows >= hh * HEAD_DIM) & (rows < (hh + 1) * HEAD_DIM), qt, 0.0)
        gate = jnp.where(past, _dot3(km_ref[...], qm), NEG_INF)
        g = gate
        for r in range(C_TOPK):
            thr = jnp.max(g, axis=0, keepdims=True)
            if r < C_TOPK - 1:
                g = jnp.where(g >= thr, NEG_INF, g)
        sel_ref[hh] = jnp.where(past & (gate >= thr), 1.0, 0.0)

        qh, qmid, ql = _split3(-c * tqf)
        qpos = jnp.where(aug_r == 0, qh, jnp.where(aug_r == 1, qmid, jnp.where(aug_r == 2, ql,
                         jnp.where(aug_r < 6, 1.0, 0.0))))
        qb = (qm * (LOG2E * HEAD_DIM ** -0.5)).astype(BF16)
        qcat.append(jnp.concatenate([qb, qpos.astype(BF16), zpad], axis=0))
        kh, kmid, kl = _split3(c * tkf)
        kpos.append(jnp.where(kcol < 3, 1.0, jnp.where(kcol == 3, kh, jnp.where(kcol == 4, kmid,
                              jnp.where(kcol == 5, kl, 0.0)))).astype(BF16))
        c2.append(c)

        s = _dot(jnp.concatenate([k_own, kpos[hh][:C_BLOCK]], axis=1), qcat[hh])
        s = jnp.where(causal, s, NEG_INF)
        m0 = jnp.max(s, axis=0, keepdims=True)
        p = jnp.exp2(s - m0)
        l0 = jnp.sum(p, axis=0, keepdims=True)
        acc0 = _dot(vt_ref[pl.ds(hh * HEAD_DIM, HEAD_DIM), pl.ds(k0, C_BLOCK)], p.astype(BF16))
        init.append((m0, l0, acc0))

    nsub = _KV_STEP // C_BLOCK

    nsteps = (qi + nsub - 1) // nsub
    last_step = k_ref.shape[0] // _KV_STEP - 1

    def key_start(n):
        return pl.multiple_of(jnp.clip(n, 0, last_step) * _KV_STEP, _KV_STEP)

    def score_piece(n, slot, hh):
        kb = k_ref[pl.ds(key_start(n), _KV_STEP), :]
        s_ref[slot, hh] = _dot(jnp.concatenate([kb, kpos[hh]], axis=1), qcat[hh])

    def scores(n, slot):
        for hh in range(2):
            score_piece(n, slot, hh)

    def pv(n, hh, slot):
        return _dot(vt_ref[pl.ds(hh * HEAD_DIM, HEAD_DIM), pl.ds(key_start(n), _KV_STEP)],
                    p_ref[slot, hh])

    def step(n, slot, carry):
        a_prev, state = carry
        off = (qi * C_BLOCK - n * _KV_STEP).astype(F32)
        new_a, new_state = [], []
        for hh in range(2):
            m_i, l_i, acc = state[hh]
            shift = c2[hh] * off
            sel = [sel_ref[hh, pl.ds(nsub * n + j, 1), :] > 0.5 for j in range(nsub)]
            m_new = m_i
            score_piece(n + 1, 1 - slot, hh)
            for j in range(nsub):
                mj = jnp.max(s_ref[slot, hh, j * C_BLOCK:(j + 1) * C_BLOCK, :], axis=0, keepdims=True)
                m_new = jnp.maximum(m_new, jnp.where(sel[j], mj - shift, NEG_INF))
            acc = a_prev[hh] * acc + pv(n - 1, hh, 1 - slot)
            l_new = jnp.exp2(m_i - m_new) * l_i
            for j in range(nsub):
                rows = slice(j * C_BLOCK, (j + 1) * C_BLOCK)
                p = jnp.exp2(s_ref[slot, hh, rows, :] - jnp.where(sel[j], m_new + shift, BIG))
                p_ref[slot, hh, rows, :] = p.astype(BF16)
                l_new = l_new + jnp.sum(p, axis=0, keepdims=True)
            new_a.append(jnp.exp2(m_i - m_new))
            new_state.append((m_new, l_new, acc))
        return tuple(new_a), tuple(new_state)

    def body(n2, carry):
        return step(2 * n2 + 1, 1, step(2 * n2, 0, carry))

    p_ref[1] = jnp.zeros_like(p_ref[1])
    scores(0, 0)
    one_a = jnp.ones((1, tq), F32)
    npairs = (nsteps + 1) // 2
    a_prev, state = lax.fori_loop(0, npairs, body, ((one_a, one_a), tuple(init)))
    outs = []
    for hh in range(2):
        _, l, acc = state[hh]
        outs.append((a_prev[hh] * acc + pv(2 * npairs - 1, hh, 1)) / l)
    o_ref[...] = jnp.concatenate(outs, axis=0).T.astype(o_ref.dtype)


def _moba_attention(qt, k, vt, kmean):
    d, t = qt.shape
    nblk = t // C_BLOCK
    slopes = jnp.asarray([_alibi_slope(h, N_HEADS) for h in range(N_HEADS)], F32)
    pw = 2 * HEAD_DIM
    assert t % (2 * _KV_STEP) == 0
    return pl.pallas_call(
        functools.partial(_moba_kernel, nblk=nblk),
        grid=(d // pw, nblk),
        in_specs=[pl.BlockSpec(memory_space=pltpu.SMEM),
                  pl.BlockSpec((pw, C_BLOCK), lambda hp, i: (hp, i)),
                  pl.BlockSpec((t, pw), lambda hp, i: (0, hp)),
                  pl.BlockSpec((pw, t), lambda hp, i: (hp, 0)),
                  pl.BlockSpec((nblk, pw), lambda hp, i: (0, hp))],
        out_specs=pl.BlockSpec((C_BLOCK, pw), lambda hp, i: (i, hp)),
        out_shape=jax.ShapeDtypeStruct((t, d), BF16),
        scratch_shapes=[pltpu.VMEM((2, nblk, C_BLOCK), F32), pltpu.VMEM((2, 2, _KV_STEP, C_BLOCK), F32),
                        pltpu.VMEM((2, 2, _KV_STEP, C_BLOCK), BF16)],
        compiler_params=_params("parallel", "arbitrary"),
        name="moba_attention",
    )(slopes, qt, k, vt, kmean)


def _layer_norm(z, g, b):
    mu = jnp.mean(z, axis=-1, keepdims=True)
    zc = z - mu
    var = jnp.mean(zc * zc, axis=-1, keepdims=True)
    return zc * lax.rsqrt(var + LN_EPS) * g + b


def _outproj_ln_kernel(*refs, n_groups, alpha):
    o_refs = refs[:n_groups]
    lse_refs = refs[n_groups:2 * n_groups] if n_groups > 1 else ()
    pos = n_groups + len(lse_refs)
    wo_ref, x_ref, g_ref, b_ref, xn_ref, xb_ref = refs[pos:pos + 6]
    if n_groups == 1:
        o = o_refs[0][...]
    else:
        lses = [r[...] for r in lse_refs]
        m = functools.reduce(jnp.maximum, lses)
        es = [jnp.exp(l - m) for l in lses]
        den = functools.reduce(jnp.add, es)
        o = functools.reduce(jnp.add, [(e / den) * r[...].astype(F32) for e, r in zip(es, o_refs)])
        o = o.astype(BF16)
    y = _dot(o, wo_ref[...])
    xn = _layer_norm(alpha * x_ref[...] + y, g_ref[...], b_ref[...])
    xn_ref[...] = xn
    xb_ref[...] = _pack_rows(xn)


def _outproj_ln(os_, lses, wo, x, g, b, alpha, tm=512):
    t, d = x.shape
    n = len(os_)
    tile = pl.BlockSpec((tm, d), lambda i: (i, 0))
    return pl.pallas_call(
        functools.partial(_outproj_ln_kernel, n_groups=n, alpha=alpha),
        grid=(t // tm,),
        in_specs=[tile] * (n + len(lses)) + [_resident((d, d)), tile, _resident((1, d)), _resident((1, d))],
        out_specs=[tile, pl.BlockSpec((tm // 2, d), lambda i: (i, 0))],
        out_shape=[jax.ShapeDtypeStruct((t, d), F32), jax.ShapeDtypeStruct((t // 2, d), jnp.uint32)],
        compiler_params=_params("parallel"),
        name="outproj_ln",
    )(*os_, *lses, wo, x, g.reshape(1, d), b.reshape(1, d))


def _pack_rows(a):
    return pltpu.bitcast(a.astype(BF16), jnp.uint32)


def _pack_dup(a):
    bits = pltpu.bitcast(a.astype(BF16).astype(F32), jnp.uint32)
    return bits | (bits >> 16)


def _pack_weight_kernel(w_ref, o_ref):
    o_ref[...] = _pack_rows(w_ref[...])


def _pack_weight(w, layer, tr=2048):
    _, r, c = w.shape
    return pl.pallas_call(
        _pack_weight_kernel,
        grid=(r // tr,),
        in_specs=[pl.BlockSpec((None, tr, c), lambda i: (layer, i, 0))],
        out_specs=pl.BlockSpec((tr // 2, c), lambda i: (i, 0)),
        out_shape=jax.ShapeDtypeStruct((r // 2, c), jnp.uint32),
        compiler_params=_params("parallel"),
        name="pack_weight",
    )(w)


def _oddeven_merge(lo, hi, r):
    step = r * 2
    if step < hi - lo:
        yield from _oddeven_merge(lo, hi, step)
        yield from _oddeven_merge(lo + r, hi, step)
        yield from [(i, i + r) for i in range(lo + r, hi - r, step)]
    else:
        yield (lo, lo + r)


def _oddeven_sort(lo, hi):
    if hi > lo:
        mid = lo + (hi - lo) // 2
        yield from _oddeven_sort(lo, mid)
        yield from _oddeven_sort(mid + 1, hi)
        yield from _oddeven_merge(lo, hi, 1)


_SORT16 = tuple(_oddeven_sort(0, PEER_TOPK - 1))


def _top16_sorted(s):
    n = PEER_TOPK
    x = [s[_SUBLANES * j:_SUBLANES * (j + 1)] for j in range(n)]

    def cmpx(i, j):
        x[i], x[j] = jnp.maximum(x[i], x[j]), jnp.minimum(x[i], x[j])

    for i, j in _SORT16:
        cmpx(i, j)
    for shift in (4, 2, 1):
        y = [pltpu.roll(v, shift, 0) for v in x]
        x = [jnp.maximum(x[i], y[n - 1 - i]) for i in range(n)]
        stride = n // 2
        while stride:
            for i in range(n):
                if not i & stride:
                    cmpx(i, i + stride)
            stride //= 2
    top = [v[0:1] for v in x]
    nxt = jnp.max(jnp.where(s < top[n - 1], s, NEG_INF), axis=0, keepdims=True)
    return top + [nxt]


def _top_vals(s, k):
    out = []
    for r in range(k):
        m = jnp.max(s, axis=0, keepdims=True)
        out.append(m)
        if r < k - 1:
            s = jnp.where(s >= m, NEG_INF, s)
    return out


_CAND = [(i, j) for i in range(PEER_TOPK + 1) for j in range(PEER_TOPK + 1)
         if (i + 1) * (j + 1) <= PEER_TOPK + 1]
_NCAND = -(-len(_CAND) // 8) * 8


def _router_kernel(x_ref, wq_ref, sk_ref, r2_ref, e2_ref, n1_ref, w_ref, q_ref, cand_ref):
    q_ref[...] = _dot(x_ref[...].astype(BF16), wq_ref[...])
    tt = x_ref.shape[0]
    cand_ref[...] = jnp.full((_NCAND, tt), NEG_INF, F32)
    nk = PEER_NKEYS
    for h in range(PEER_HEADS):
        s1 = _dot3_nt(sk_ref[0], q_ref[:, (2 * h) * nk:(2 * h + 1) * nk])
        s2 = _dot3_nt(sk_ref[1], q_ref[:, (2 * h + 1) * nk:(2 * h + 2) * nk])
        a = _top16_sorted(s1)
        b = _top16_sorted(s2)
        for r, (i, j) in enumerate(_CAND):
            cand_ref[r:r + 1, :] = a[i] + b[j]
        v = _top_vals(cand_ref[...], PEER_TOPK + 1)
        thr = 0.5 * (v[PEER_TOPK - 1] + v[PEER_TOPK])
        z = functools.reduce(jnp.add, [jnp.exp(v[r] - v[0]) for r in range(PEER_TOPK)])
        cut = thr - s1
        rank2 = jnp.zeros_like(s2)
        count1 = jnp.zeros_like(s1)
        for r in range(PEER_TOPK):
            rank2 = jnp.where(s2 < b[r], r + 1.0, rank2)
            count1 = jnp.where(b[r] >= cut, r + 1.0, count1)
        r2_ref[h] = _pack_rows(rank2)
        e2_ref[h] = _pack_rows(jnp.exp(s2 - b[0]))
        n1_ref[h] = _pack_dup(count1)
        w_ref[h] = _pack_dup(jnp.exp(s1 - a[0]) / z)


def _router(xn, wq, subkeys, tt=256):
    t, d = xn.shape
    nq = wq.shape[1]
    shape = (PEER_HEADS, PEER_NKEYS, t)
    ospec = pl.BlockSpec((PEER_HEADS, PEER_NKEYS, tt), lambda i: (0, 0, i))
    pspec = pl.BlockSpec((PEER_HEADS, PEER_NKEYS // 2, tt), lambda i: (0, 0, i))
    packed = jax.ShapeDtypeStruct((PEER_HEADS, PEER_NKEYS // 2, t), jnp.uint32)
    return pl.pallas_call(
        _router_kernel,
        grid=(t // tt,),
        in_specs=[pl.BlockSpec((tt, d), lambda i: (i, 0)), _resident((d, nq)), _resident(subkeys.shape)],
        out_specs=[pspec, pspec, ospec, ospec],
        out_shape=[packed, packed, jax.ShapeDtypeStruct(shape, jnp.uint32),
                   jax.ShapeDtypeStruct(shape, jnp.uint32)],
        scratch_shapes=[pltpu.VMEM((tt, nq), F32), pltpu.VMEM((_NCAND, tt), F32)],
        compiler_params=_params("parallel"),
        name="peer_router",
    )(xn, wq, subkeys)


def _gelu(x):
    return 0.5 * x * (1.0 + lax.erf(x * (2.0 ** -0.5)))


_BF16_ROWS = 16
_PEER_SUB = 512
_PEER_COLS = 256
_PEER_KQ = 512


def _peer_kernel(x_ref, u_ref, v_ref, r2_ref, e2_ref, n1_ref, w_ref, y_ref, acc_ref, h_ref, act_ref,
                 *, n_tiles, n_e):
    s = pl.program_id(0)
    _, te, tt = h_ref.shape
    nk = PEER_NKEYS
    j_c = jnp.clip(s - 2, 0, n_tiles - 1) % n_e

    @pl.when(s == 0)
    def _():
        h_ref[...] = jnp.zeros_like(h_ref)
        act_ref[...] = jnp.zeros_like(act_ref)

    @pl.when(j_c == 0)
    def _():
        acc_ref[...] = jnp.zeros_like(acc_ref)

    n_i1 = te // nk

    def first_matmul(slot, k):
        xb = pltpu.bitcast(x_ref[...], BF16)
        uk = pltpu.bitcast(u_ref[pl.ds(k * _PEER_SUB // 2, _PEER_SUB // 2), :], BF16)
        h_ref[slot, pl.ds(k * _PEER_SUB, _PEER_SUB), :] = _dot_nt(uk, xb)

    def second_matmul(slot, n, kq):
        cols = pl.ds(n * _PEER_COLS, _PEER_COLS)
        vq = pltpu.bitcast(v_ref[pl.ds(kq * _PEER_KQ // 2, _PEER_KQ // 2), :], BF16)
        acc_ref[:, cols] += lax.dot_general(vq, act_ref[slot, pl.ds(kq * _PEER_KQ, _PEER_KQ), cols],
                                            (((0,), (0,)), ((), ())), preferred_element_type=F32)

    def gate(slot, c, v):
        cols = pl.ds(c * 128, 128)
        prow = pl.ds(v * _SUBLANES, _SUBLANES)
        g = [jnp.zeros((_BF16_ROWS, 128), BF16) for _ in range(n_i1)]
        for h in range(PEER_HEADS):
            r2 = pltpu.bitcast(r2_ref[h, prow, cols], BF16)
            e2 = pltpu.bitcast(e2_ref[h, prow, cols], BF16)
            for ii in range(n_i1):
                row = pl.ds(ii, 1)
                n_row = pltpu.bitcast(jnp.broadcast_to(n1_ref[h, row, cols], (_SUBLANES, 128)), BF16)
                w_row = pltpu.bitcast(jnp.broadcast_to(w_ref[h, row, cols], (_SUBLANES, 128)), BF16)
                g[ii] = g[ii] + jnp.where(r2 < n_row, e2, 0.0) * w_row
        for ii in range(n_i1):
            rows = pl.ds(ii * nk + v * _BF16_ROWS, _BF16_ROWS)
            act_ref[slot, rows, cols] = _gelu(h_ref[slot, rows, cols]).astype(BF16) * g[ii]

    def stages(slot):
        pieces = [functools.partial(first_matmul, slot, k) for k in range(te // _PEER_SUB)]
        pieces += [functools.partial(second_matmul, slot, n, kq)
                   for n in range(tt // _PEER_COLS) for kq in range(te // _PEER_KQ)]
        blocks = [(c, v) for c in range(tt // 128) for v in range(nk // _BF16_ROWS)]
        per = -(-len(blocks) // len(pieces))
        for p, piece in enumerate(pieces):
            piece()
            for c, v in blocks[p * per:(p + 1) * per]:
                gate(1 - slot, c, v)

    @pl.when(s % 2 == 0)
    def _():
        stages(0)

    @pl.when(s % 2 == 1)
    def _():
        stages(1)

    @pl.when(j_c == n_e - 1)
    def _():
        y_ref[...] = acc_ref[...].T


def _peer_experts(xb, u, v, r2, e2, n1, w, tt=512, te=2048):
    t, d = 2 * xb.shape[0], xb.shape[1]
    ne = 2 * u.shape[0]
    n_e = ne // te
    n_tiles = (t // tt) * n_e
    pair = lambda lag: (lambda s: jnp.clip(s - lag, 0, n_tiles - 1))
    tok = lambda lag: (lambda s: pair(lag)(s) // n_e)
    exp = lambda lag: (lambda s: pair(lag)(s) % n_e)
    wspec = lambda lag: pl.BlockSpec((te // 2, d), lambda s: (exp(lag)(s), 0))
    pspec = pl.BlockSpec((PEER_HEADS, PEER_NKEYS // 2, tt), lambda s: (0, 0, tok(1)(s)))
    rspec = pl.BlockSpec((PEER_HEADS, te // PEER_NKEYS, tt), lambda s: (0, exp(1)(s), tok(1)(s)))
    return pl.pallas_call(
        functools.partial(_peer_kernel, n_tiles=n_tiles, n_e=n_e),
        grid=(n_tiles + 2,),
        in_specs=[pl.BlockSpec((tt // 2, d), lambda s: (tok(0)(s), 0)), wspec(0), wspec(2),
                  pspec, pspec, rspec, rspec],
        out_specs=pl.BlockSpec((tt, d), lambda s: (tok(2)(s), 0)),
        out_shape=jax.ShapeDtypeStruct((t, d), F32),
        scratch_shapes=[pltpu.VMEM((d, tt), F32), pltpu.VMEM((2, te, tt), F32), pltpu.VMEM((2, te, tt), BF16)],
        compiler_params=_params("arbitrary"),
        name="peer_experts",
    )(xb, u, v, r2, e2, n1, w)


def _ln_ple_kernel(x_ref, y_ref, g_ref, b_ref, p_ref, wp_ref, wg_ref, bg_ref, o_ref, *, alpha):
    xn = _layer_norm(alpha * x_ref[...] + y_ref[...], g_ref[...], b_ref[...])
    gate = jax.nn.sigmoid(_dot(xn.astype(BF16), wg_ref[...]) + bg_ref[...])
    o_ref[...] = xn + gate * _dot(p_ref[...].astype(BF16), wp_ref[...])


def _ln_ple(x, y, g, b, p, layer, wp, wg, bg, alpha, tm=512):
    t, d = x.shape
    dp = p.shape[2]
    tile = pl.BlockSpec((tm, d), lambda i: (i, 0))
    vec = _resident((1, d))
    return pl.pallas_call(
        functools.partial(_ln_ple_kernel, alpha=alpha),
        grid=(t // tm,),
        in_specs=[tile, tile, vec, vec, pl.BlockSpec((None, tm, dp), lambda i: (layer, i, 0)), _resident((dp, d)),
                  _resident((d, d)), vec],
        out_specs=tile,
        out_shape=jax.ShapeDtypeStruct((t, d), F32),
        compiler_params=_params("parallel"),
        name="ln_ple",
    )(x, y, g.reshape(1, d), b.reshape(1, d), p, wp, wg, bg.reshape(1, d))


def _mixer_a(x, w_qkv, sinks):
    d = N_HEADS * HEAD_DIM
    kvw = A_KV_HEADS * HEAD_DIM
    wq, wk, wv = w_qkv[:, :d], w_qkv[:, d:d + kvw], w_qkv[:, d + kvw:]
    qt, k, vt = _proj_t(x, wq.T.astype(BF16), wk.astype(BF16), wv.T.astype(BF16))
    o, = _band_attention(qt, k, vt, kv_group=N_HEADS // A_KV_HEADS, dist_scale=1.0,
                         max_dist=A_WINDOW - 1, sinks=sinks)
    return [o], []


def _mixer_b(x, w_qkv):
    d = N_HEADS * HEAD_DIM
    ng = len(B_PATTERNS)
    wk = w_qkv[:, ng * d:(ng + 1) * d].astype(BF16)
    wvt = w_qkv[:, (ng + 1) * d:].T.astype(BF16)
    os_, lses = [], []
    for gi, (w, r) in enumerate(B_PATTERNS):
        qt, k, vt = _proj_t(x, w_qkv[:, gi * d:(gi + 1) * d].T.astype(BF16), wk, wvt, dil=r)
        o, lse = _band_attention(qt, k, vt, kv_group=1, dist_scale=float(r), max_dist=w // r, want_lse=True)
        os_.append(o)
        lses.append(lse)
    return os_, lses


def _mixer_c(x, w_qkv):
    d = N_HEADS * HEAD_DIM
    wq, wk, wv = w_qkv[:, :d], w_qkv[:, d:2 * d], w_qkv[:, 2 * d:]
    qt, k, vt, kmean = _proj_c(x, wq.T.astype(BF16), wk.astype(BF16), wv.T.astype(BF16))
    return [_moba_attention(qt, k, vt, kmean.reshape(-1, d))], []


def kernel(x, p, a_w_qkv, a_sinks, a_w_o, b_w_qkv, b_w_o, c_w_qkv, c_w_o, ln1_g, ln1_b, ln2_g, ln2_b,
           peer_w_q, peer_subkeys, peer_u, peer_v, ple_w, ple_gate_w, ple_gate_b):
    depth = p.shape[0]
    alpha = (2 * depth) ** 0.25
    bsz, seq, d = x.shape
    assert bsz == 1 and seq % (B_PATTERNS[-1][1] * BAND) == 0
    xt = x.reshape(seq, d)
    for i in range(depth):
        kind, j = i % 3, i // 3
        if kind == 0:
            os_, lses = _mixer_a(xt, a_w_qkv[j], a_sinks[j])
            wo = a_w_o[j]
        elif kind == 1:
            os_, lses = _mixer_b(xt, b_w_qkv[j])
            wo = b_w_o[j]
        else:
            os_, lses = _mixer_c(xt, c_w_qkv[j])
            wo = c_w_o[j]
        x1, x1b = _outproj_ln(os_, lses, wo.astype(BF16), xt, ln1_g[i], ln1_b[i], alpha)
        routing = _router(x1, peer_w_q[i].astype(BF16), peer_subkeys[i])
        y = _peer_experts(x1b, _pack_weight(peer_u, i), _pack_weight(peer_v, i), *routing)
        xt = _ln_ple(x1, y, ln2_g[i], ln2_b[i], p.reshape(depth, seq, -1), i, ple_w[i].astype(BF16),
                     ple_gate_w[i].astype(BF16), ple_gate_b[i], alpha)
    return xt.reshape(bsz, seq, d)
```

```python
import functools

import jax
import jax.numpy as jnp
import ml_dtypes
import numpy as np
from jax import lax
from jax.experimental import pallas as pl
from jax.experimental.pallas import tpu as pltpu

F32 = jnp.float32
BF16 = jnp.bfloat16
NEG_INF = float("-inf")
BIG = 1e30
LOG2E = 1.4426950408889634

D_MODEL = 1024
HEAD_DIM = 64
N_HEADS = 16
BAND = 128
A_KV_HEADS = 2
A_WINDOW = 128
B_PATTERNS = ((128, 1), (512, 4), (2048, 16))
C_BLOCK = 256
C_TOPK = 3
PEER_HEADS = 8
PEER_NKEYS = 128
PEER_TOPK = 16
LN_EPS = 1e-5
VMEM_LIMIT = 56 * 1024 * 1024
_SUBLANES = 8


def _params(*sem):
    return pltpu.CompilerParams(dimension_semantics=sem, vmem_limit_bytes=VMEM_LIMIT)


def _dot(a, b):
    return jnp.dot(a, b, preferred_element_type=F32)


def _dot_nt(a, b):
    return lax.dot_general(a, b, (((1,), (1,)), ((), ())), preferred_element_type=F32)


def _split(a):
    hi = a.astype(BF16)
    lo = (a - hi.astype(F32)).astype(BF16)
    return hi, lo


def _alibi_slope(h, n):
    return 2.0 ** (-8.0 * (h + 1) / n)


def _resident(shape):
    zeros = (0,) * len(shape)
    return pl.BlockSpec(shape, lambda *_: zeros)


def _proj_t_kernel(x_ref, wqt_ref, wk_ref, wvt_ref, qt_ref, k_ref, vt_ref, *, q_scale):
    xb = x_ref[...].astype(BF16)
    qt_ref[...] = _pack_rows(_dot_nt(wqt_ref[...], xb) * q_scale)
    k_ref[...] = _pack_rows(_dot(xb, wk_ref[...]))
    vt_ref[...] = _pack_rows(_dot_nt(wvt_ref[...], xb))


def _proj_t(x, wqt, wk, wvt, dil=1, tm=512):
    t, kdim = x.shape
    d, kw = wqt.shape[0], wk.shape[1]
    l = t // dil
    tm = min(tm, l)
    u32 = jnp.uint32
    return pl.pallas_call(
        functools.partial(_proj_t_kernel, q_scale=LOG2E * HEAD_DIM ** -0.5),
        grid=(dil, l // tm),
        in_specs=[pl.BlockSpec((tm, kdim), lambda c, i: (i, c)), _resident((d, kdim)), _resident((kdim, kw)),
                  _resident((kw, kdim))],
        out_specs=[pl.BlockSpec((None, d // 2, tm), lambda c, i: (c, 0, i)),
                   pl.BlockSpec((None, tm // 2, kw), lambda c, i: (c, i, 0)),
                   pl.BlockSpec((None, kw // 2, tm), lambda c, i: (c, 0, i))],
        out_shape=[jax.ShapeDtypeStruct((dil, d // 2, l), u32), jax.ShapeDtypeStruct((dil, l // 2, kw), u32),
                   jax.ShapeDtypeStruct((dil, kw // 2, l), u32)],
        compiler_params=_params("parallel", "parallel"),
        name="proj_band",
    )(x.reshape(l, dil * kdim), wqt, wk, wvt)


def _proj_c_kernel(x_ref, wqt_ref, wk_ref, wvt_ref, qt_ref, k_ref, vt_ref, km_ref, *, nblk):
    xb = x_ref[...].astype(BF16)
    qt_ref[...] = _dot_nt(wqt_ref[...], xb)
    kf = _dot(xb, wk_ref[...])
    k_ref[...] = kf.astype(BF16)
    vt_ref[...] = _dot_nt(wvt_ref[...], xb).astype(BF16)
    for r in range(nblk):
        km_ref[r] = jnp.mean(kf[r * C_BLOCK:(r + 1) * C_BLOCK], axis=0, keepdims=True)


def _proj_c(x, wqt, wk, wvt, tm=512):
    t, k = x.shape
    d = wk.shape[1]
    nblk = tm // C_BLOCK
    return pl.pallas_call(
        functools.partial(_proj_c_kernel, nblk=nblk),
        grid=(t // tm,),
        in_specs=[pl.BlockSpec((tm, k), lambda i: (i, 0)), _resident((d, k)), _resident((k, d)),
                  _resident((d, k))],
        out_specs=[pl.BlockSpec((d, tm), lambda i: (0, i)), pl.BlockSpec((tm, d), lambda i: (i, 0)),
                   pl.BlockSpec((d, tm), lambda i: (0, i)), pl.BlockSpec((nblk, 1, d), lambda i: (i, 0, 0))],
        out_shape=[jax.ShapeDtypeStruct((d, t), F32), jax.ShapeDtypeStruct((t, d), BF16),
                   jax.ShapeDtypeStruct((d, t), BF16), jax.ShapeDtypeStruct((t // C_BLOCK, 1, d), F32)],
        compiler_params=_params("parallel"),
        name="proj_moba",
    )(x, wqt, wk, wvt)


_N_AUG = 16
_PAIR = 2 * HEAD_DIM


def _split3_np(a):
    bf = ml_dtypes.bfloat16
    h = a.astype(bf).astype(np.float32)
    m = (a - h).astype(bf).astype(np.float32)
    l = (a - h - m).astype(bf).astype(np.float32)
    return h, m, l


def _pack_rows_np(a):
    bits = a.astype(ml_dtypes.bfloat16).view(np.uint16).astype(np.uint32)
    return bits[0::2] | (bits[1::2] << 16)


def _band_tables(dist_scale):
    ki = np.arange(2 * BAND, dtype=np.float32)
    kaug = np.zeros((2 * BAND, _PAIR), np.float32)
    kaug[:, 0:3] = 1.0
    kaug[:, 3:6] = ki[:, None]
    qaug = np.zeros((N_HEADS, _N_AUG, BAND), np.float32)
    qpos = np.arange(BAND, dtype=np.float32) + BAND
    for h in range(N_HEADS):
        c = np.float32(_alibi_slope(h, N_HEADS) * dist_scale * LOG2E)
        qaug[h, 0:3] = np.stack(_split3_np(-c * qpos))
        qaug[h, 3:6] = np.stack(_split3_np(np.full((BAND,), c, np.float32)))
    return _pack_rows_np(kaug), np.stack([_pack_rows_np(qaug[h]) for h in range(N_HEADS)])


def _band_kernel(*refs, kv_group, max_dist, use_sinks, want_lse):
    q_ref, kp_ref, ko_ref, vp_ref, vo_ref, kaug_ref, qaug_ref = refs[:7]
    pos = 7
    sink_ref = None
    if use_sinks:
        sink_ref = refs[pos]
        pos += 1
    o_ref = refs[pos]
    lse_ref = refs[pos + 1] if want_lse else None

    b = pl.program_id(1)
    ki = lax.broadcasted_iota(jnp.int32, (2 * BAND, BAND), 0)
    qi = lax.broadcasted_iota(jnp.int32, (2 * BAND, BAND), 1)
    dist = qi + BAND - ki
    valid = (dist >= 0) & (dist <= max_dist) & (ki >= jnp.where(b > 0, 0, BAND))
    qt = pltpu.bitcast(q_ref[...], BF16)
    k_all = jnp.concatenate([pltpu.bitcast(kp_ref[...], BF16), pltpu.bitcast(ko_ref[...], BF16)], axis=0)
    vt_all = jnp.concatenate([pltpu.bitcast(vp_ref[...], BF16), pltpu.bitcast(vo_ref[...], BF16)], axis=1)
    kaug = pltpu.bitcast(kaug_ref[...], BF16)
    zslot = jnp.zeros((HEAD_DIM, BAND), BF16)
    zpad = jnp.zeros((_PAIR - _N_AUG, BAND), BF16)
    n_kv = N_HEADS // kv_group
    scores = [None] * N_HEADS
    for pair in range(-(-n_kv // 2)):
        heads = [h for h in range(N_HEADS) if (h // kv_group) // 2 == pair]
        qcats = []
        for h in heads:
            qh = qt[h * HEAD_DIM:(h + 1) * HEAD_DIM]
            slot = [qh, zslot] if (h // kv_group) % 2 == 0 else [zslot, qh]
            qcats.append(jnp.concatenate(slot + [pltpu.bitcast(qaug_ref[h], BF16), zpad], axis=0))
        kcat = jnp.concatenate([k_all[:, pair * _PAIR:(pair + 1) * _PAIR], kaug], axis=1)
        s_all = _dot(kcat, jnp.concatenate(qcats, axis=1))
        for n, h in enumerate(heads):
            scores[h] = s_all[:, n * BAND:(n + 1) * BAND]
    probs, dens, lses = [], [], []
    for h in range(N_HEADS):
        s = jnp.where(valid, scores[h], NEG_INF)
        m = jnp.max(s, axis=0, keepdims=True)
        if use_sinks:
            sk = sink_ref[:, h:h + 1] * LOG2E
            m = jnp.maximum(m, sk)
        p = jnp.exp2(s - m)
        den = jnp.sum(p, axis=0, keepdims=True)
        if use_sinks:
            den = den + jnp.exp2(sk - m)
        probs.append(p.astype(BF16))
        dens.append(den)
        if want_lse:
            lses.append(jnp.broadcast_to((m + jnp.log2(den)) * (1.0 / LOG2E), (HEAD_DIM, BAND)))
    outs = []
    for g in range(n_kv):
        hs = range(g * kv_group, (g + 1) * kv_group)
        pv = _dot(vt_all[g * HEAD_DIM:(g + 1) * HEAD_DIM], jnp.concatenate([probs[h] for h in hs], axis=1))
        outs += [pv[:, n * BAND:(n + 1) * BAND] / dens[h] for n, h in enumerate(hs)]
    o_ref[...] = jnp.concatenate(outs, axis=0).T.astype(o_ref.dtype)
    if want_lse:
        lse_ref[...] = jnp.concatenate(lses, axis=0).T


def _band_attention(qt, k, vt, *, kv_group, dist_scale, max_dist, sinks=None, want_lse=False):
    r, _, l = qt.shape
    d = N_HEADS * HEAD_DIM
    kw = k.shape[2]
    prev = lambda c, b: jnp.maximum(b - 1, 0)
    kaug, qaug = _band_tables(dist_scale)
    in_specs = [pl.BlockSpec((None, d // 2, BAND), lambda c, b: (c, 0, b)),
                pl.BlockSpec((None, BAND // 2, kw), lambda c, b: (c, prev(c, b), 0)),
                pl.BlockSpec((None, BAND // 2, kw), lambda c, b: (c, b, 0)),
                pl.BlockSpec((None, kw // 2, BAND), lambda c, b: (c, 0, prev(c, b))),
                pl.BlockSpec((None, kw // 2, BAND), lambda c, b: (c, 0, b)),
                _resident(kaug.shape), _resident(qaug.shape)]
    args = [qt, k, k, vt, vt, jnp.asarray(kaug), jnp.asarray(qaug)]
    if sinks is not None:
        in_specs.append(_resident((1, N_HEADS)))
        args.append(sinks.reshape(1, N_HEADS).astype(F32))
    o_spec = pl.BlockSpec((BAND, d), lambda c, b: (b, c))
    out_specs = [o_spec]
    out_shape = [jax.ShapeDtypeStruct((l, r * d), BF16)]
    if want_lse:
        out_specs.append(o_spec)
        out_shape.append(jax.ShapeDtypeStruct((l, r * d), F32))
    outs = pl.pallas_call(
        functools.partial(_band_kernel, kv_group=kv_group, max_dist=max_dist,
                          use_sinks=sinks is not None, want_lse=want_lse),
        grid=(r, l // BAND),
        in_specs=in_specs, out_specs=out_specs, out_shape=out_shape,
        compiler_params=_params("parallel", "parallel"),
        name="band_attention",
    )(*args)
    return [o.reshape(l * r, d) for o in outs]


def _dot3(a, b):
    ah, al = _split(a)
    bh, bl = _split(b)
    return _dot(ah, bh) + (_dot(ah, bl) + _dot(al, bh))


def _dot3_nt(a, b):
    ah, al = _split(a)
    bh, bl = _split(b)
    return _dot_nt(ah, bh) + (_dot_nt(ah, bl) + _dot_nt(al, bh))


def _split3(a):
    h = a.astype(BF16).astype(F32)
    r = a - h
    m = r.astype(BF16).astype(F32)
    l = (r - m).astype(BF16).astype(F32)
    return h, m, l


_KV_STEP = 2 * C_BLOCK


def _moba_kernel(slope_ref, qt_ref, k_ref, vt_ref, km_ref, o_ref, sel_ref, s_ref, p_ref, *, nblk):
    hp = pl.program_id(0)
    qi = pl.program_id(1)
    tq = C_BLOCK
    pw = 2 * HEAD_DIM
    qt = qt_ref[...]
    rows = lax.broadcasted_iota(jnp.int32, (pw, tq), 0)
    blk = lax.broadcasted_iota(jnp.int32, (nblk, tq), 0)
    past = blk < qi
    aug_r = lax.broadcasted_iota(jnp.int32, (_N_AUG, tq), 0)
    tqf = lax.broadcasted_iota(jnp.int32, (_N_AUG, tq), 1).astype(F32)
    kcol = lax.broadcasted_iota(jnp.int32, (_KV_STEP, pw), 1)
    tkf = lax.broadcasted_iota(jnp.int32, (_KV_STEP, pw), 0).astype(F32)
    causal = (lax.broadcasted_iota(jnp.int32, (C_BLOCK, tq), 1)
              >= lax.broadcasted_iota(jnp.int32, (C_BLOCK, tq), 0))
    zpad = jnp.zeros((pw - _N_AUG, tq), BF16)
    k0 = pl.multiple_of(qi * C_BLOCK, C_BLOCK)
    k_own = k_ref[pl.ds(k0, C_BLOCK), :]

    qcat, kpos, c2, init = [], [], [], []
    for hh in range(2):
        c = slope_ref[2 * hp + hh] * LOG2E
        qm = jnp.where((rows >= hh * HEAD_DIM) & (rows < (hh + 1) * HEAD_DIM), qt, 0.0)
        gate = jnp.where(past, _dot3(km_ref[...], qm), NEG_INF)
        g = gate
        for r in range(C_TOPK):
            thr = jnp.max(g, axis=0, keepdims=True)
            if r < C_TOPK - 1:
                g = jnp.where(g >= thr, NEG_INF, g)
        sel_ref[hh] = jnp.where(past & (gate >= thr), 1.0, 0.0)

        qh, qmid, ql = _split3(-c * tqf)
        qpos = jnp.where(aug_r == 0, qh, jnp.where(aug_r == 1, qmid, jnp.where(aug_r == 2, ql,
                         jnp.where(aug_r < 6, 1.0, 0.0))))
        qb = (qm * (LOG2E * HEAD_DIM ** -0.5)).astype(BF16)
        qcat.append(jnp.concatenate([qb, qpos.astype(BF16), zpad], axis=0))
        kh, kmid, kl = _split3(c * tkf)
        kpos.append(jnp.where(kcol < 3, 1.0, jnp.where(kcol == 3, kh, jnp.where(kcol == 4, kmid,
                              jnp.where(kcol == 5, kl, 0.0)))).astype(BF16))
        c2.append(c)

        s = _dot(jnp.concatenate([k_own, kpos[hh][:C_BLOCK]], axis=1), qcat[hh])
        s = jnp.where(causal, s, NEG_INF)
        m0 = jnp.max(s, axis=0, keepdims=True)
        p = jnp.exp2(s - m0)
        l0 = jnp.sum(p, axis=0, keepdims=True)
        acc0 = _dot(vt_ref[pl.ds(hh * HEAD_DIM, HEAD_DIM), pl.ds(k0, C_BLOCK)], p.astype(BF16))
        init.append((m0, l0, acc0))

    nsub = _KV_STEP // C_BLOCK

    nsteps = (qi + nsub - 1) // nsub
    last_step = k_ref.shape[0] // _KV_STEP - 1

    def key_start(n):
        return pl.multiple_of(jnp.clip(n, 0, last_step) * _KV_STEP, _KV_STEP)

    def score_piece(n, slot, hh, j):
        rows = slice(j * C_BLOCK, (j + 1) * C_BLOCK)
        kb = k_ref[pl.ds(key_start(n) + j * C_BLOCK, C_BLOCK), :]
        s_ref[slot, hh, rows, :] = _dot(jnp.concatenate([kb, kpos[hh][rows]], axis=1), qcat[hh])

    def scores(n, slot):
        for hh in range(2):
            for j in range(nsub):
                score_piece(n, slot, hh, j)

    def pv(n, hh, slot):
        return _dot(vt_ref[pl.ds(hh * HEAD_DIM, HEAD_DIM), pl.ds(key_start(n), _KV_STEP)],
                    p_ref[slot, hh])

    def step(n, slot, carry):
        a_prev, state = carry
        off = (qi * C_BLOCK - n * _KV_STEP).astype(F32)
        new_a, new_state = [], []
        for hh in range(2):
            m_i, l_i, acc = state[hh]
            shift = c2[hh] * off
            sel = [sel_ref[hh, pl.ds(nsub * n + j, 1), :] > 0.5 for j in range(nsub)]
            m_new = m_i
            for j in range(nsub):
                score_piece(n + 1, 1 - slot, hh, j)
                mj = jnp.max(s_ref[slot, hh, j * C_BLOCK:(j + 1) * C_BLOCK, :], axis=0, keepdims=True)
                m_new = jnp.maximum(m_new, jnp.where(sel[j], mj - shift, NEG_INF))
            acc = a_prev[hh] * acc + pv(n - 1, hh, 1 - slot)
            l_new = jnp.exp2(m_i - m_new) * l_i
            for j in range(nsub):
                rows = slice(j * C_BLOCK, (j + 1) * C_BLOCK)
                p = jnp.exp2(s_ref[slot, hh, rows, :] - jnp.where(sel[j], m_new + shift, BIG))
                p_ref[slot, hh, rows, :] = p.astype(BF16)
                l_new = l_new + jnp.sum(p, axis=0, keepdims=True)
            new_a.append(jnp.exp2(m_i - m_new))
            new_state.append((m_new, l_new, acc))
        return tuple(new_a), tuple(new_state)

    def body(n2, carry):
        return step(2 * n2 + 1, 1, step(2 * n2, 0, carry))

    p_ref[1] = jnp.zeros_like(p_ref[1])
    scores(0, 0)
    one_a = jnp.ones((1, tq), F32)
    npairs = (nsteps + 1) // 2
    a_prev, state = lax.fori_loop(0, npairs, body, ((one_a, one_a), tuple(init)))
    outs = []
    for hh in range(2):
        _, l, acc = state[hh]
        outs.append((a_prev[hh] * acc + pv(2 * npairs - 1, hh, 1)) / l)
    o_ref[...] = jnp.concatenate(outs, axis=0).T.astype(o_ref.dtype)


def _moba_attention(qt, k, vt, kmean):
    d, t = qt.shape
    nblk = t // C_BLOCK
    slopes = jnp.asarray([_alibi_slope(h, N_HEADS) for h in range(N_HEADS)], F32)
    pw = 2 * HEAD_DIM
    assert t % (2 * _KV_STEP) == 0
    return pl.pallas_call(
        functools.partial(_moba_kernel, nblk=nblk),
        grid=(d // pw, nblk),
        in_specs=[pl.BlockSpec(memory_space=pltpu.SMEM),
                  pl.BlockSpec((pw, C_BLOCK), lambda hp, i: (hp, i)),
                  pl.BlockSpec((t, pw), lambda hp, i: (0, hp)),
                  pl.BlockSpec((pw, t), lambda hp, i: (hp, 0)),
                  pl.BlockSpec((nblk, pw), lambda hp, i: (0, hp))],
        out_specs=pl.BlockSpec((C_BLOCK, pw), lambda hp, i: (i, hp)),
        out_shape=jax.ShapeDtypeStruct((t, d), BF16),
        scratch_shapes=[pltpu.VMEM((2, nblk, C_BLOCK), F32), pltpu.VMEM((2, 2, _KV_STEP, C_BLOCK), F32),
                        pltpu.VMEM((2, 2, _KV_STEP, C_BLOCK), BF16)],
        compiler_params=_params("parallel", "arbitrary"),
        name="moba_attention",
    )(slopes, qt, k, vt, kmean)


def _layer_norm(z, g, b):
    mu = jnp.mean(z, axis=-1, keepdims=True)
    zc = z - mu
    var = jnp.mean(zc * zc, axis=-1, keepdims=True)
    return zc * lax.rsqrt(var + LN_EPS) * g + b


def _outproj_ln_kernel(*refs, n_groups, alpha):
    o_refs = refs[:n_groups]
    lse_refs = refs[n_groups:2 * n_groups] if n_groups > 1 else ()
    pos = n_groups + len(lse_refs)
    wo_ref, x_ref, g_ref, b_ref, xn_ref, xb_ref = refs[pos:pos + 6]
    if n_groups == 1:
        o = o_refs[0][...]
    else:
        lses = [r[...] for r in lse_refs]
        m = functools.reduce(jnp.maximum, lses)
        es = [jnp.exp(l - m) for l in lses]
        den = functools.reduce(jnp.add, es)
        o = functools.reduce(jnp.add, [(e / den) * r[...].astype(F32) for e, r in zip(es, o_refs)])
        o = o.astype(BF16)
    y = _dot(o, wo_ref[...])
    xn = _layer_norm(alpha * x_ref[...] + y, g_ref[...], b_ref[...])
    xn_ref[...] = xn
    xb_ref[...] = _pack_rows(xn.T)


def _outproj_ln(os_, lses, wo, x, g, b, alpha, tm=512):
    t, d = x.shape
    n = len(os_)
    tile = pl.BlockSpec((tm, d), lambda i: (i, 0))
    return pl.pallas_call(
        functools.partial(_outproj_ln_kernel, n_groups=n, alpha=alpha),
        grid=(t // tm,),
        in_specs=[tile] * (n + len(lses)) + [_resident((d, d)), tile, _resident((1, d)), _resident((1, d))],
        out_specs=[tile, pl.BlockSpec((d // 2, tm), lambda i: (0, i))],
        out_shape=[jax.ShapeDtypeStruct((t, d), F32), jax.ShapeDtypeStruct((d // 2, t), jnp.uint32)],
        compiler_params=_params("parallel"),
        name="outproj_ln",
    )(*os_, *lses, wo, x, g.reshape(1, d), b.reshape(1, d))


def _pack_rows(a):
    return pltpu.bitcast(a.astype(BF16), jnp.uint32)


def _pack_dup(a):
    bits = pltpu.bitcast(a.astype(BF16).astype(F32), jnp.uint32)
    return bits | (bits >> 16)


def _pack_weight_kernel(w_ref, o_ref):
    o_ref[...] = _pack_rows(w_ref[...])


def _pack_weight(w, layer, tr=2048):
    _, r, c = w.shape
    return pl.pallas_call(
        _pack_weight_kernel,
        grid=(r // tr,),
        in_specs=[pl.BlockSpec((None, tr, c), lambda i: (layer, i, 0))],
        out_specs=pl.BlockSpec((tr // 2, c), lambda i: (i, 0)),
        out_shape=jax.ShapeDtypeStruct((r // 2, c), jnp.uint32),
        compiler_params=_params("parallel"),
        name="pack_weight",
    )(w)


def _oddeven_merge(lo, hi, r):
    step = r * 2
    if step < hi - lo:
        yield from _oddeven_merge(lo, hi, step)
        yield from _oddeven_merge(lo + r, hi, step)
        yield from [(i, i + r) for i in range(lo + r, hi - r, step)]
    else:
        yield (lo, lo + r)


def _oddeven_sort(lo, hi):
    if hi > lo:
        mid = lo + (hi - lo) // 2
        yield from _oddeven_sort(lo, mid)
        yield from _oddeven_sort(mid + 1, hi)
        yield from _oddeven_merge(lo, hi, 1)


_SORT16 = tuple(_oddeven_sort(0, PEER_TOPK - 1))


def _top16_sorted(s):
    n = PEER_TOPK
    x = [s[_SUBLANES * j:_SUBLANES * (j + 1)] for j in range(n)]

    def cmpx(i, j):
        x[i], x[j] = jnp.maximum(x[i], x[j]), jnp.minimum(x[i], x[j])

    for i, j in _SORT16:
        cmpx(i, j)
    for shift in (4, 2, 1):
        y = [pltpu.roll(v, shift, 0) for v in x]
        x = [jnp.maximum(x[i], y[n - 1 - i]) for i in range(n)]
        stride = n // 2
        while stride:
            for i in range(n):
                if not i & stride:
                    cmpx(i, i + stride)
            stride //= 2
    top = [v[0:1] for v in x]
    nxt = jnp.max(jnp.where(s < top[n - 1], s, NEG_INF), axis=0, keepdims=True)
    return top + [nxt]


def _top_vals(s, k):
    out = []
    for r in range(k):
        m = jnp.max(s, axis=0, keepdims=True)
        out.append(m)
        if r < k - 1:
            s = jnp.where(s >= m, NEG_INF, s)
    return out


_CAND = [(i, j) for i in range(PEER_TOPK + 1) for j in range(PEER_TOPK + 1)
         if (i + 1) * (j + 1) <= PEER_TOPK + 1]
_NCAND = -(-len(_CAND) // 8) * 8


def _router_kernel(x_ref, wq_ref, sk_ref, r2_ref, e2_ref, n1_ref, w_ref, q_ref, cand_ref):
    q_ref[...] = _dot(x_ref[...].astype(BF16), wq_ref[...])
    tt = x_ref.shape[0]
    cand_ref[...] = jnp.full((_NCAND, tt), NEG_INF, F32)
    nk = PEER_NKEYS
    for h in range(PEER_HEADS):
        s1 = _dot3_nt(sk_ref[0], q_ref[:, (2 * h) * nk:(2 * h + 1) * nk])
        s2 = _dot3_nt(sk_ref[1], q_ref[:, (2 * h + 1) * nk:(2 * h + 2) * nk])
        a = _top16_sorted(s1)
        b = _top16_sorted(s2)
        for r, (i, j) in enumerate(_CAND):
            cand_ref[r:r + 1, :] = a[i] + b[j]
        v = _top_vals(cand_ref[...], PEER_TOPK + 1)
        thr = 0.5 * (v[PEER_TOPK - 1] + v[PEER_TOPK])
        z = functools.reduce(jnp.add, [jnp.exp(v[r] - v[0]) for r in range(PEER_TOPK)])
        cut = thr - s1
        rank2 = jnp.zeros_like(s2)
        count1 = jnp.zeros_like(s1)
        for r in range(PEER_TOPK):
            rank2 = jnp.where(s2 < b[r], r + 1.0, rank2)
            count1 = jnp.where(b[r] >= cut, r + 1.0, count1)
        r2_ref[h] = _pack_rows(rank2)
        e2_ref[h] = _pack_rows(jnp.exp(s2 - b[0]))
        n1_ref[h] = _pack_dup(count1)
        w_ref[h] = _pack_dup(jnp.exp(s1 - a[0]) / z)


def _router(xn, wq, subkeys, tt=256):
    t, d = xn.shape
    nq = wq.shape[1]
    shape = (PEER_HEADS, PEER_NKEYS, t)
    ospec = pl.BlockSpec((PEER_HEADS, PEER_NKEYS, tt), lambda i: (0, 0, i))
    pspec = pl.BlockSpec((PEER_HEADS, PEER_NKEYS // 2, tt), lambda i: (0, 0, i))
    packed = jax.ShapeDtypeStruct((PEER_HEADS, PEER_NKEYS // 2, t), jnp.uint32)
    return pl.pallas_call(
        _router_kernel,
        grid=(t // tt,),
        in_specs=[pl.BlockSpec((tt, d), lambda i: (i, 0)), _resident((d, nq)), _resident(subkeys.shape)],
        out_specs=[pspec, pspec, ospec, ospec],
        out_shape=[packed, packed, jax.ShapeDtypeStruct(shape, jnp.uint32),
                   jax.ShapeDtypeStruct(shape, jnp.uint32)],
        scratch_shapes=[pltpu.VMEM((tt, nq), F32), pltpu.VMEM((_NCAND, tt), F32)],
        compiler_params=_params("parallel"),
        name="peer_router",
    )(xn, wq, subkeys)


def _gelu(x):
    return 0.5 * x * (1.0 + lax.erf(x * (2.0 ** -0.5)))


_BF16_ROWS = 16
_PEER_SUB = 512
_PEER_COLS = 256
_PEER_KQ = 512


def _peer_kernel(x_ref, u_ref, v_ref, r2_ref, e2_ref, n1_ref, w_ref, y_ref, acc_ref, h_ref, act_ref,
                 *, n_tiles, n_e):
    s = pl.program_id(0)
    _, te, tt = h_ref.shape
    nk = PEER_NKEYS
    j_c = jnp.clip(s - 2, 0, n_tiles - 1) % n_e

    @pl.when(s == 0)
    def _():
        h_ref[...] = jnp.zeros_like(h_ref)
        act_ref[...] = jnp.zeros_like(act_ref)

    @pl.when(j_c == 0)
    def _():
        acc_ref[...] = jnp.zeros_like(acc_ref)

    n_i1 = te // nk

    def first_matmul(slot, k):
        xt = pltpu.bitcast(x_ref[...], BF16)
        uk = pltpu.bitcast(u_ref[pl.ds(k * _PEER_SUB // 2, _PEER_SUB // 2), :], BF16)
        h_ref[slot, pl.ds(k * _PEER_SUB, _PEER_SUB), :] = _dot(uk, xt)

    def second_matmul(slot, n, kq):
        cols = pl.ds(n * _PEER_COLS, _PEER_COLS)
        vq = pltpu.bitcast(v_ref[pl.ds(kq * _PEER_KQ // 2, _PEER_KQ // 2), :], BF16)
        acc_ref[:, cols] += lax.dot_general(vq, act_ref[slot, pl.ds(kq * _PEER_KQ, _PEER_KQ), cols],
                                            (((0,), (0,)), ((), ())), preferred_element_type=F32)

    def gate(slot, c, v):
        cols = pl.ds(c * 128, 128)
        prow = pl.ds(v * _SUBLANES, _SUBLANES)
        g = [jnp.zeros((_BF16_ROWS, 128), BF16) for _ in range(n_i1)]
        for h in range(PEER_HEADS):
            r2 = pltpu.bitcast(r2_ref[h, prow, cols], BF16)
            e2 = pltpu.bitcast(e2_ref[h, prow, cols], BF16)
            for ii in range(n_i1):
                row = pl.ds(ii, 1)
                n_row = pltpu.bitcast(jnp.broadcast_to(n1_ref[h, row, cols], (_SUBLANES, 128)), BF16)
                w_row = pltpu.bitcast(jnp.broadcast_to(w_ref[h, row, cols], (_SUBLANES, 128)), BF16)
                g[ii] = g[ii] + jnp.where(r2 < n_row, e2, 0.0) * w_row
        for ii in range(n_i1):
            rows = pl.ds(ii * nk + v * _BF16_ROWS, _BF16_ROWS)
            act_ref[slot, rows, cols] = _gelu(h_ref[slot, rows, cols]).astype(BF16) * g[ii]

    def stages(slot):
        pieces = [functools.partial(first_matmul, slot, k) for k in range(te // _PEER_SUB)]
        pieces += [functools.partial(second_matmul, slot, n, kq)
                   for n in range(tt // _PEER_COLS) for kq in range(te // _PEER_KQ)]
        blocks = [(c, v) for c in range(tt // 128) for v in range(nk // _BF16_ROWS)]
        per = -(-len(blocks) // len(pieces))
        for p, piece in enumerate(pieces):
            piece()
            for c, v in blocks[p * per:(p + 1) * per]:
                gate(1 - slot, c, v)

    @pl.when(s % 2 == 0)
    def _():
        stages(0)

    @pl.when(s % 2 == 1)
    def _():
        stages(1)

    @pl.when(j_c == n_e - 1)
    def _():
        y_ref[...] = acc_ref[...].T


def _peer_experts(xbt, u, v, r2, e2, n1, w, tt=512, te=2048):
    t, d = xbt.shape[1], 2 * xbt.shape[0]
    ne = 2 * u.shape[0]
    n_e = ne // te
    n_tiles = (t // tt) * n_e
    pair = lambda lag: (lambda s: jnp.clip(s - lag, 0, n_tiles - 1))
    tok = lambda lag: (lambda s: pair(lag)(s) // n_e)
    exp = lambda lag: (lambda s: pair(lag)(s) % n_e)
    wspec = lambda lag: pl.BlockSpec((te // 2, d), lambda s: (exp(lag)(s), 0))
    pspec = pl.BlockSpec((PEER_HEADS, PEER_NKEYS // 2, tt), lambda s: (0, 0, tok(1)(s)))
    rspec = pl.BlockSpec((PEER_HEADS, te // PEER_NKEYS, tt), lambda s: (0, exp(1)(s), tok(1)(s)))
    return pl.pallas_call(
        functools.partial(_peer_kernel, n_tiles=n_tiles, n_e=n_e),
        grid=(n_tiles + 2,),
        in_specs=[pl.BlockSpec((d // 2, tt), lambda s: (0, tok(0)(s))), wspec(0), wspec(2),
                  pspec, pspec, rspec, rspec],
        out_specs=pl.BlockSpec((tt, d), lambda s: (tok(2)(s), 0)),
        out_shape=jax.ShapeDtypeStruct((t, d), F32),
        scratch_shapes=[pltpu.VMEM((d, tt), F32), pltpu.VMEM((2, te, tt), F32), pltpu.VMEM((2, te, tt), BF16)],
        compiler_params=_params("arbitrary"),
        name="peer_experts",
    )(xbt, u, v, r2, e2, n1, w)


def _ln_ple_kernel(x_ref, y_ref, g_ref, b_ref, p_ref, wp_ref, wg_ref, bg_ref, o_ref, *, alpha):
    xn = _layer_norm(alpha * x_ref[...] + y_ref[...], g_ref[...], b_ref[...])
    gate = jax.nn.sigmoid(_dot(xn.astype(BF16), wg_ref[...]) + bg_ref[...])
    o_ref[...] = xn + gate * _dot(p_ref[...].astype(BF16), wp_ref[...])


def _ln_ple(x, y, g, b, p, layer, wp, wg, bg, alpha, tm=512):
    t, d = x.shape
    dp = p.shape[2]
    tile = pl.BlockSpec((tm, d), lambda i: (i, 0))
    vec = _resident((1, d))
    return pl.pallas_call(
        functools.partial(_ln_ple_kernel, alpha=alpha),
        grid=(t // tm,),
        in_specs=[tile, tile, vec, vec, pl.BlockSpec((None, tm, dp), lambda i: (layer, i, 0)), _resident((dp, d)),
                  _resident((d, d)), vec],
        out_specs=tile,
        out_shape=jax.ShapeDtypeStruct((t, d), F32),
        compiler_params=_params("parallel"),
        name="ln_ple",
    )(x, y, g.reshape(1, d), b.reshape(1, d), p, wp, wg, bg.reshape(1, d))


def _mixer_a(x, w_qkv, sinks):
    d = N_HEADS * HEAD_DIM
    kvw = A_KV_HEADS * HEAD_DIM
    wq, wk, wv = w_qkv[:, :d], w_qkv[:, d:d + kvw], w_qkv[:, d + kvw:]
    qt, k, vt = _proj_t(x, wq.T.astype(BF16), wk.astype(BF16), wv.T.astype(BF16))
    o, = _band_attention(qt, k, vt, kv_group=N_HEADS // A_KV_HEADS, dist_scale=1.0,
                         max_dist=A_WINDOW - 1, sinks=sinks)
    return [o], []


def _mixer_b(x, w_qkv):
    d = N_HEADS * HEAD_DIM
    ng = len(B_PATTERNS)
    wk = w_qkv[:, ng * d:(ng + 1) * d].astype(BF16)
    wvt = w_qkv[:, (ng + 1) * d:].T.astype(BF16)
    os_, lses = [], []
    for gi, (w, r) in enumerate(B_PATTERNS):
        qt, k, vt = _proj_t(x, w_qkv[:, gi * d:(gi + 1) * d].T.astype(BF16), wk, wvt, dil=r)
        o, lse = _band_attention(qt, k, vt, kv_group=1, dist_scale=float(r), max_dist=w // r, want_lse=True)
        os_.append(o)
        lses.append(lse)
    return os_, lses


def _mixer_c(x, w_qkv):
    d = N_HEADS * HEAD_DIM
    wq, wk, wv = w_qkv[:, :d], w_qkv[:, d:2 * d], w_qkv[:, 2 * d:]
    qt, k, vt, kmean = _proj_c(x, wq.T.astype(BF16), wk.astype(BF16), wv.T.astype(BF16))
    return [_moba_attention(qt, k, vt, kmean.reshape(-1, d))], []


def kernel(x, p, a_w_qkv, a_sinks, a_w_o, b_w_qkv, b_w_o, c_w_qkv, c_w_o, ln1_g, ln1_b, ln2_g, ln2_b,
           peer_w_q, peer_subkeys, peer_u, peer_v, ple_w, ple_gate_w, ple_gate_b):
    depth = p.shape[0]
    alpha = (2 * depth) ** 0.25
    bsz, seq, d = x.shape
    assert bsz == 1 and seq % (B_PATTERNS[-1][1] * BAND) == 0
    xt = x.reshape(seq, d)
    for i in range(depth):
        kind, j = i % 3, i // 3
        if kind == 0:
            os_, lses = _mixer_a(xt, a_w_qkv[j], a_sinks[j])
            wo = a_w_o[j]
        elif kind == 1:
            os_, lses = _mixer_b(xt, b_w_qkv[j])
            wo = b_w_o[j]
        else:
            os_, lses = _mixer_c(xt, c_w_qkv[j])
            wo = c_w_o[j]
        x1, x1b = _outproj_ln(os_, lses, wo.astype(BF16), xt, ln1_g[i], ln1_b[i], alpha)
        routing = _router(x1, peer_w_q[i].astype(BF16), peer_subkeys[i])
        y = _peer_experts(x1b, _pack_weight(peer_u, i), _pack_weight(peer_v, i), *routing)
        xt = _ln_ple(x1, y, ln2_g[i], ln2_b[i], p.reshape(depth, seq, -1), i, ple_w[i].astype(BF16),
                     ple_gate_w[i].astype(BF16), ple_gate_b[i], alpha)
    return xt.reshape(bsz, seq, d)
```

```python
import functools

import jax
import jax.numpy as jnp
import ml_dtypes
import numpy as np
from jax import lax
from jax.experimental import pallas as pl
from jax.experimental.pallas import tpu as pltpu

F32 = jnp.float32
BF16 = jnp.bfloat16
NEG_INF = float("-inf")
BIG = 1e30
LOG2E = 1.4426950408889634

D_MODEL = 1024
HEAD_DIM = 64
N_HEADS = 16
BAND = 128
A_KV_HEADS = 2
A_WINDOW = 128
B_PATTERNS = ((128, 1), (512, 4), (2048, 16))
C_BLOCK = 256
C_TOPK = 3
PEER_HEADS = 8
PEER_NKEYS = 128
PEER_TOPK = 16
LN_EPS = 1e-5
VMEM_LIMIT = 56 * 1024 * 1024
_SUBLANES = 8


def _params(*sem):
    return pltpu.CompilerParams(dimension_semantics=sem, vmem_limit_bytes=VMEM_LIMIT)


def _dot(a, b):
    return jnp.dot(a, b, preferred_element_type=F32)


def _dot_nt(a, b):
    return lax.dot_general(a, b, (((1,), (1,)), ((), ())), preferred_element_type=F32)


def _split(a):
    hi = a.astype(BF16)
    lo = (a - hi.astype(F32)).astype(BF16)
    return hi, lo


def _alibi_slope(h, n):
    return 2.0 ** (-8.0 * (h + 1) / n)


def _resident(shape):
    zeros = (0,) * len(shape)
    return pl.BlockSpec(shape, lambda *_: zeros)


def _proj_t_kernel(x_ref, wqt_ref, wk_ref, wvt_ref, qt_ref, k_ref, vt_ref, *, q_scale):
    xb = x_ref[...].astype(BF16)
    qt_ref[...] = _pack_rows(_dot_nt(wqt_ref[...], xb) * q_scale)
    k_ref[...] = _pack_rows(_dot(xb, wk_ref[...]))
    vt_ref[...] = _pack_rows(_dot_nt(wvt_ref[...], xb))


def _proj_t(x, wqt, wk, wvt, dil=1, tm=512):
    t, kdim = x.shape
    d, kw = wqt.shape[0], wk.shape[1]
    l = t // dil
    tm = min(tm, l)
    u32 = jnp.uint32
    return pl.pallas_call(
        functools.partial(_proj_t_kernel, q_scale=LOG2E * HEAD_DIM ** -0.5),
        grid=(dil, l // tm),
        in_specs=[pl.BlockSpec((tm, kdim), lambda c, i: (i, c)), _resident((d, kdim)), _resident((kdim, kw)),
                  _resident((kw, kdim))],
        out_specs=[pl.BlockSpec((None, d // 2, tm), lambda c, i: (c, 0, i)),
                   pl.BlockSpec((None, tm // 2, kw), lambda c, i: (c, i, 0)),
                   pl.BlockSpec((None, kw // 2, tm), lambda c, i: (c, 0, i))],
        out_shape=[jax.ShapeDtypeStruct((dil, d // 2, l), u32), jax.ShapeDtypeStruct((dil, l // 2, kw), u32),
                   jax.ShapeDtypeStruct((dil, kw // 2, l), u32)],
        compiler_params=_params("parallel", "parallel"),
        name="proj_band",
    )(x.reshape(l, dil * kdim), wqt, wk, wvt)


def _proj_c_kernel(x_ref, wqt_ref, wk_ref, wvt_ref, qt_ref, k_ref, vt_ref, km_ref, *, nblk):
    xb = x_ref[...].astype(BF16)
    qt_ref[...] = _dot_nt(wqt_ref[...], xb)
    kf = _dot(xb, wk_ref[...])
    k_ref[...] = kf.astype(BF16)
    vt_ref[...] = _dot_nt(wvt_ref[...], xb).astype(BF16)
    for r in range(nblk):
        km_ref[r] = jnp.mean(kf[r * C_BLOCK:(r + 1) * C_BLOCK], axis=0, keepdims=True)


def _proj_c(x, wqt, wk, wvt, tm=512):
    t, k = x.shape
    d = wk.shape[1]
    nblk = tm // C_BLOCK
    return pl.pallas_call(
        functools.partial(_proj_c_kernel, nblk=nblk),
        grid=(t // tm,),
        in_specs=[pl.BlockSpec((tm, k), lambda i: (i, 0)), _resident((d, k)), _resident((k, d)),
                  _resident((d, k))],
        out_specs=[pl.BlockSpec((d, tm), lambda i: (0, i)), pl.BlockSpec((tm, d), lambda i: (i, 0)),
                   pl.BlockSpec((d, tm), lambda i: (0, i)), pl.BlockSpec((nblk, 1, d), lambda i: (i, 0, 0))],
        out_shape=[jax.ShapeDtypeStruct((d, t), F32), jax.ShapeDtypeStruct((t, d), BF16),
                   jax.ShapeDtypeStruct((d, t), BF16), jax.ShapeDtypeStruct((t // C_BLOCK, 1, d), F32)],
        compiler_params=_params("parallel"),
        name="proj_moba",
    )(x, wqt, wk, wvt)


_N_AUG = 16
_PAIR = 2 * HEAD_DIM


def _split3_np(a):
    bf = ml_dtypes.bfloat16
    h = a.astype(bf).astype(np.float32)
    m = (a - h).astype(bf).astype(np.float32)
    l = (a - h - m).astype(bf).astype(np.float32)
    return h, m, l


def _pack_rows_np(a):
    bits = a.astype(ml_dtypes.bfloat16).view(np.uint16).astype(np.uint32)
    return bits[0::2] | (bits[1::2] << 16)


def _band_tables(dist_scale):
    ki = np.arange(2 * BAND, dtype=np.float32)
    kaug = np.zeros((2 * BAND, _PAIR), np.float32)
    kaug[:, 0:3] = 1.0
    kaug[:, 3:6] = ki[:, None]
    qaug = np.zeros((N_HEADS, _N_AUG, BAND), np.float32)
    qpos = np.arange(BAND, dtype=np.float32) + BAND
    for h in range(N_HEADS):
        c = np.float32(_alibi_slope(h, N_HEADS) * dist_scale * LOG2E)
        qaug[h, 0:3] = np.stack(_split3_np(-c * qpos))
        qaug[h, 3:6] = np.stack(_split3_np(np.full((BAND,), c, np.float32)))
    return _pack_rows_np(kaug), np.stack([_pack_rows_np(qaug[h]) for h in range(N_HEADS)])


def _band_kernel(*refs, kv_group, max_dist, use_sinks, want_lse):
    q_ref, kp_ref, ko_ref, vp_ref, vo_ref, kaug_ref, qaug_ref = refs[:7]
    pos = 7
    sink_ref = None
    if use_sinks:
        sink_ref = refs[pos]
        pos += 1
    o_ref = refs[pos]
    lse_ref = refs[pos + 1] if want_lse else None

    b = pl.program_id(1)
    ki = lax.broadcasted_iota(jnp.int32, (2 * BAND, BAND), 0)
    qi = lax.broadcasted_iota(jnp.int32, (2 * BAND, BAND), 1)
    dist = qi + BAND - ki
    valid = (dist >= 0) & (dist <= max_dist) & (ki >= jnp.where(b > 0, 0, BAND))
    qt = pltpu.bitcast(q_ref[...], BF16)
    k_all = jnp.concatenate([pltpu.bitcast(kp_ref[...], BF16), pltpu.bitcast(ko_ref[...], BF16)], axis=0)
    vt_all = jnp.concatenate([pltpu.bitcast(vp_ref[...], BF16), pltpu.bitcast(vo_ref[...], BF16)], axis=1)
    kaug = pltpu.bitcast(kaug_ref[...], BF16)
    zslot = jnp.zeros((HEAD_DIM, BAND), BF16)
    zpad = jnp.zeros((_PAIR - _N_AUG, BAND), BF16)
    n_kv = N_HEADS // kv_group
    scores = [None] * N_HEADS
    for pair in range(-(-n_kv // 2)):
        heads = [h for h in range(N_HEADS) if (h // kv_group) // 2 == pair]
        qcats = []
        for h in heads:
            qh = qt[h * HEAD_DIM:(h + 1) * HEAD_DIM]
            slot = [qh, zslot] if (h // kv_group) % 2 == 0 else [zslot, qh]
            qcats.append(jnp.concatenate(slot + [pltpu.bitcast(qaug_ref[h], BF16), zpad], axis=0))
        kcat = jnp.concatenate([k_all[:, pair * _PAIR:(pair + 1) * _PAIR], kaug], axis=1)
        s_all = _dot(kcat, jnp.concatenate(qcats, axis=1))
        for n, h in enumerate(heads):
            scores[h] = s_all[:, n * BAND:(n + 1) * BAND]
    probs, dens, lses = [], [], []
    for h in range(N_HEADS):
        s = jnp.where(valid, scores[h], NEG_INF)
        m = jnp.max(s, axis=0, keepdims=True)
        if use_sinks:
            sk = sink_ref[:, h:h + 1] * LOG2E
            m = jnp.maximum(m, sk)
        p = jnp.exp2(s - m)
        den = jnp.sum(p, axis=0, keepdims=True)
        if use_sinks:
            den = den + jnp.exp2(sk - m)
        probs.append(p.astype(BF16))
        dens.append(den)
        if want_lse:
            lses.append((m + jnp.log2(den)) * (1.0 / LOG2E))
    outs = []
    for g in range(n_kv):
        hs = range(g * kv_group, (g + 1) * kv_group)
        pv = _dot(vt_all[g * HEAD_DIM:(g + 1) * HEAD_DIM], jnp.concatenate([probs[h] for h in hs], axis=1))
        outs += [pv[:, n * BAND:(n + 1) * BAND] / dens[h] for n, h in enumerate(hs)]
    o_ref[...] = jnp.concatenate(outs, axis=0).T.astype(o_ref.dtype)
    if want_lse:
        lse_ref[...] = jnp.concatenate(lses, axis=0)


def _band_attention(qt, k, vt, *, kv_group, dist_scale, max_dist, sinks=None, want_lse=False):
    r, _, l = qt.shape
    d = N_HEADS * HEAD_DIM
    kw = k.shape[2]
    prev = lambda c, b: jnp.maximum(b - 1, 0)
    kaug, qaug = _band_tables(dist_scale)
    in_specs = [pl.BlockSpec((None, d // 2, BAND), lambda c, b: (c, 0, b)),
                pl.BlockSpec((None, BAND // 2, kw), lambda c, b: (c, prev(c, b), 0)),
                pl.BlockSpec((None, BAND // 2, kw), lambda c, b: (c, b, 0)),
                pl.BlockSpec((None, kw // 2, BAND), lambda c, b: (c, 0, prev(c, b))),
                pl.BlockSpec((None, kw // 2, BAND), lambda c, b: (c, 0, b)),
                _resident(kaug.shape), _resident(qaug.shape)]
    args = [qt, k, k, vt, vt, jnp.asarray(kaug), jnp.asarray(qaug)]
    if sinks is not None:
        in_specs.append(_resident((1, N_HEADS)))
        args.append(sinks.reshape(1, N_HEADS).astype(F32))
    o_spec = pl.BlockSpec((BAND, d), lambda c, b: (b, c))
    out_specs = [o_spec]
    out_shape = [jax.ShapeDtypeStruct((l, r * d), BF16)]
    if want_lse:
        out_specs.append(pl.BlockSpec((None, N_HEADS, BAND), lambda c, b: (c, 0, b)))
        out_shape.append(jax.ShapeDtypeStruct((r, N_HEADS, l), F32))
    outs = pl.pallas_call(
        functools.partial(_band_kernel, kv_group=kv_group, max_dist=max_dist,
                          use_sinks=sinks is not None, want_lse=want_lse),
        grid=(r, l // BAND),
        in_specs=in_specs, out_specs=out_specs, out_shape=out_shape,
        compiler_params=_params("parallel", "parallel"),
        name="band_attention",
    )(*args)
    o = outs[0].reshape(l * r, d)
    if not want_lse:
        return [o]
    return [o, jnp.transpose(outs[1], (1, 2, 0)).reshape(N_HEADS, l * r)]


def _dot3(a, b):
    ah, al = _split(a)
    bh, bl = _split(b)
    return _dot(ah, bh) + (_dot(ah, bl) + _dot(al, bh))


def _dot3_nt(a, b):
    ah, al = _split(a)
    bh, bl = _split(b)
    return _dot_nt(ah, bh) + (_dot_nt(ah, bl) + _dot_nt(al, bh))


def _split3(a):
    h = a.astype(BF16).astype(F32)
    r = a - h
    m = r.astype(BF16).astype(F32)
    l = (r - m).astype(BF16).astype(F32)
    return h, m, l


_KV_STEP = 2 * C_BLOCK


def _moba_kernel(slope_ref, qt_ref, k_ref, vt_ref, km_ref, o_ref, sel_ref, s_ref, p_ref, *, nblk):
    hp = pl.program_id(0)
    qi = pl.program_id(1)
    tq = C_BLOCK
    pw = 2 * HEAD_DIM
    qt = qt_ref[...]
    rows = lax.broadcasted_iota(jnp.int32, (pw, tq), 0)
    blk = lax.broadcasted_iota(jnp.int32, (nblk, tq), 0)
    past = blk < qi
    aug_r = lax.broadcasted_iota(jnp.int32, (_N_AUG, tq), 0)
    tqf = lax.broadcasted_iota(jnp.int32, (_N_AUG, tq), 1).astype(F32)
    kcol = lax.broadcasted_iota(jnp.int32, (_KV_STEP, pw), 1)
    tkf = lax.broadcasted_iota(jnp.int32, (_KV_STEP, pw), 0).astype(F32)
    causal = (lax.broadcasted_iota(jnp.int32, (C_BLOCK, tq), 1)
              >= lax.broadcasted_iota(jnp.int32, (C_BLOCK, tq), 0))
    zpad = jnp.zeros((pw - _N_AUG, tq), BF16)
    k0 = pl.multiple_of(qi * C_BLOCK, C_BLOCK)
    k_own = k_ref[pl.ds(k0, C_BLOCK), :]

    qcat, kpos, c2, init = [], [], [], []
    for hh in range(2):
        c = slope_ref[2 * hp + hh] * LOG2E
        qm = jnp.where((rows >= hh * HEAD_DIM) & (rows < (hh + 1) * HEAD_DIM), qt, 0.0)
        gate = jnp.where(past, _dot3(km_ref[...], qm), NEG_INF)
        g = gate
        for r in range(C_TOPK):
            thr = jnp.max(g, axis=0, keepdims=True)
            if r < C_TOPK - 1:
                g = jnp.where(g >= thr, NEG_INF, g)
        sel_ref[hh] = jnp.where(past & (gate >= thr), 1.0, 0.0)

        qh, qmid, ql = _split3(-c * tqf)
        qpos = jnp.where(aug_r == 0, qh, jnp.where(aug_r == 1, qmid, jnp.where(aug_r == 2, ql,
                         jnp.where(aug_r < 6, 1.0, 0.0))))
        qb = (qm * (LOG2E * HEAD_DIM ** -0.5)).astype(BF16)
        qcat.append(jnp.concatenate([qb, qpos.astype(BF16), zpad], axis=0))
        kh, kmid, kl = _split3(c * tkf)
        kpos.append(jnp.where(kcol < 3, 1.0, jnp.where(kcol == 3, kh, jnp.where(kcol == 4, kmid,
                              jnp.where(kcol == 5, kl, 0.0)))).astype(BF16))
        c2.append(c)

        s = _dot(jnp.concatenate([k_own, kpos[hh][:C_BLOCK]], axis=1), qcat[hh])
        s = jnp.where(causal, s, NEG_INF)
        m0 = jnp.max(s, axis=0, keepdims=True)
        p = jnp.exp2(s - m0)
        l0 = jnp.sum(p, axis=0, keepdims=True)
        acc0 = _dot(vt_ref[pl.ds(hh * HEAD_DIM, HEAD_DIM), pl.ds(k0, C_BLOCK)], p.astype(BF16))
        init.append((m0, l0, acc0))

    nsub = _KV_STEP // C_BLOCK

    nsteps = (qi + nsub - 1) // nsub
    last_step = k_ref.shape[0] // _KV_STEP - 1

    def key_start(n):
        return pl.multiple_of(jnp.clip(n, 0, last_step) * _KV_STEP, _KV_STEP)

    def score_piece(n, slot, hh, j):
        rows = slice(j * C_BLOCK, (j + 1) * C_BLOCK)
        kb = k_ref[pl.ds(key_start(n) + j * C_BLOCK, C_BLOCK), :]
        s_ref[slot, hh, rows, :] = _dot(jnp.concatenate([kb, kpos[hh][rows]], axis=1), qcat[hh])

    def scores(n, slot):
        for hh in range(2):
            for j in range(nsub):
                score_piece(n, slot, hh, j)

    def pv(n, hh, slot):
        return _dot(vt_ref[pl.ds(hh * HEAD_DIM, HEAD_DIM), pl.ds(key_start(n), _KV_STEP)],
                    p_ref[slot, hh])

    def step(n, slot, carry):
        a_prev, state = carry
        off = (qi * C_BLOCK - n * _KV_STEP).astype(F32)
        new_a, new_state = [], []
        for hh in range(2):
            m_i, l_i, acc = state[hh]
            shift = c2[hh] * off
            sel = [sel_ref[hh, pl.ds(nsub * n + j, 1), :] > 0.5 for j in range(nsub)]
            m_new = m_i
            for j in range(nsub):
                score_piece(n + 1, 1 - slot, hh, j)
                mj = jnp.max(s_ref[slot, hh, j * C_BLOCK:(j + 1) * C_BLOCK, :], axis=0, keepdims=True)
                m_new = jnp.maximum(m_new, jnp.where(sel[j], mj - shift, NEG_INF))
            acc = a_prev[hh] * acc + pv(n - 1, hh, 1 - slot)
            l_new = jnp.exp2(m_i - m_new) * l_i
            for j in range(nsub):
                rows = slice(j * C_BLOCK, (j + 1) * C_BLOCK)
                p = jnp.exp2(s_ref[slot, hh, rows, :] - jnp.where(sel[j], m_new + shift, BIG))
                p_ref[slot, hh, rows, :] = p.astype(BF16)
                l_new = l_new + jnp.sum(p, axis=0, keepdims=True)
            new_a.append(jnp.exp2(m_i - m_new))
            new_state.append((m_new, l_new, acc))
        return tuple(new_a), tuple(new_state)

    def body(n2, carry):
        return step(2 * n2 + 1, 1, step(2 * n2, 0, carry))

    p_ref[1] = jnp.zeros_like(p_ref[1])
    scores(0, 0)
    one_a = jnp.ones((1, tq), F32)
    npairs = (nsteps + 1) // 2
    a_prev, state = lax.fori_loop(0, npairs, body, ((one_a, one_a), tuple(init)))
    outs = []
    for hh in range(2):
        _, l, acc = state[hh]
        outs.append((a_prev[hh] * acc + pv(2 * npairs - 1, hh, 1)) / l)
    o_ref[...] = jnp.concatenate(outs, axis=0).T.astype(o_ref.dtype)


def _moba_attention(qt, k, vt, kmean):
    d, t = qt.shape
    nblk = t // C_BLOCK
    slopes = jnp.asarray([_alibi_slope(h, N_HEADS) for h in range(N_HEADS)], F32)
    pw = 2 * HEAD_DIM
    assert t % (2 * _KV_STEP) == 0
    return pl.pallas_call(
        functools.partial(_moba_kernel, nblk=nblk),
        grid=(d // pw, nblk),
        in_specs=[pl.BlockSpec(memory_space=pltpu.SMEM),
                  pl.BlockSpec((pw, C_BLOCK), lambda hp, i: (hp, i)),
                  pl.BlockSpec((t, pw), lambda hp, i: (0, hp)),
                  pl.BlockSpec((pw, t), lambda hp, i: (hp, 0)),
                  pl.BlockSpec((nblk, pw), lambda hp, i: (0, hp))],
        out_specs=pl.BlockSpec((C_BLOCK, pw), lambda hp, i: (i, hp)),
        out_shape=jax.ShapeDtypeStruct((t, d), BF16),
        scratch_shapes=[pltpu.VMEM((2, nblk, C_BLOCK), F32), pltpu.VMEM((2, 2, _KV_STEP, C_BLOCK), F32),
                        pltpu.VMEM((2, 2, _KV_STEP, C_BLOCK), BF16)],
        compiler_params=_params("parallel", "arbitrary"),
        name="moba_attention",
    )(slopes, qt, k, vt, kmean)


def _layer_norm(z, g, b):
    mu = jnp.mean(z, axis=-1, keepdims=True)
    zc = z - mu
    var = jnp.mean(zc * zc, axis=-1, keepdims=True)
    return zc * lax.rsqrt(var + LN_EPS) * g + b


def _outproj_ln_kernel(*refs, n_groups, alpha):
    o_refs = refs[:n_groups]
    pos = n_groups
    if n_groups == 1:
        o = o_refs[0][...]
    else:
        lse_ref = refs[pos]
        pos += 1
        lses = [lse_ref[gi] for gi in range(n_groups)]
        m = functools.reduce(jnp.maximum, lses)
        es = [jnp.exp(l - m) for l in lses]
        den = functools.reduce(jnp.add, es)
        heads = lax.broadcasted_iota(jnp.int32, (N_HEADS, N_HEADS * HEAD_DIM), 0)
        cols = lax.broadcasted_iota(jnp.int32, (N_HEADS, N_HEADS * HEAD_DIM), 1)
        expand = jnp.where(cols // HEAD_DIM == heads, 1.0, 0.0).astype(BF16)
        tn = (((0,), (0,)), ((), ()))
        o = None
        for e, r in zip(es, o_refs):
            whi, wlo = _split(e / den)
            wexp = (lax.dot_general(whi, expand, tn, preferred_element_type=F32)
                    + lax.dot_general(wlo, expand, tn, preferred_element_type=F32))
            term = wexp * r[...].astype(F32)
            o = term if o is None else o + term
        o = o.astype(BF16)
    wo_ref, x_ref, g_ref, b_ref, xn_ref, xb_ref = refs[pos:pos + 6]
    y = _dot(o, wo_ref[...])
    xn = _layer_norm(alpha * x_ref[...] + y, g_ref[...], b_ref[...])
    xn_ref[...] = xn
    xb_ref[...] = _pack_rows(xn.T)


def _outproj_ln(os_, lses, wo, x, g, b, alpha, tm=512):
    t, d = x.shape
    n = len(os_)
    tile = pl.BlockSpec((tm, d), lambda i: (i, 0))
    lse_args = [jnp.stack(lses)] if lses else []
    lse_specs = [pl.BlockSpec((n, N_HEADS, tm), lambda i: (0, 0, i))] if lses else []
    return pl.pallas_call(
        functools.partial(_outproj_ln_kernel, n_groups=n, alpha=alpha),
        grid=(t // tm,),
        in_specs=[tile] * n + lse_specs + [_resident((d, d)), tile, _resident((1, d)), _resident((1, d))],
        out_specs=[tile, pl.BlockSpec((d // 2, tm), lambda i: (0, i))],
        out_shape=[jax.ShapeDtypeStruct((t, d), F32), jax.ShapeDtypeStruct((d // 2, t), jnp.uint32)],
        compiler_params=_params("parallel"),
        name="outproj_ln",
    )(*os_, *lse_args, wo, x, g.reshape(1, d), b.reshape(1, d))


def _pack_rows(a):
    return pltpu.bitcast(a.astype(BF16), jnp.uint32)


def _pack_dup(a):
    bits = pltpu.bitcast(a.astype(BF16).astype(F32), jnp.uint32)
    return bits | (bits >> 16)


def _pack_weight_kernel(w_ref, o_ref):
    o_ref[...] = _pack_rows(w_ref[...])


def _pack_weight(w, layer, tr=2048):
    _, r, c = w.shape
    return pl.pallas_call(
        _pack_weight_kernel,
        grid=(r // tr,),
        in_specs=[pl.BlockSpec((None, tr, c), lambda i: (layer, i, 0))],
        out_specs=pl.BlockSpec((tr // 2, c), lambda i: (i, 0)),
        out_shape=jax.ShapeDtypeStruct((r // 2, c), jnp.uint32),
        compiler_params=_params("parallel"),
        name="pack_weight",
    )(w)


def _oddeven_merge(lo, hi, r):
    step = r * 2
    if step < hi - lo:
        yield from _oddeven_merge(lo, hi, step)
        yield from _oddeven_merge(lo + r, hi, step)
        yield from [(i, i + r) for i in range(lo + r, hi - r, step)]
    else:
        yield (lo, lo + r)


def _oddeven_sort(lo, hi):
    if hi > lo:
        mid = lo + (hi - lo) // 2
        yield from _oddeven_sort(lo, mid)
        yield from _oddeven_sort(mid + 1, hi)
        yield from _oddeven_merge(lo, hi, 1)


_SORT16 = tuple(_oddeven_sort(0, PEER_TOPK - 1))


def _top16_sorted(s):
    n = PEER_TOPK
    x = [s[_SUBLANES * j:_SUBLANES * (j + 1)] for j in range(n)]

    def cmpx(i, j):
        x[i], x[j] = jnp.maximum(x[i], x[j]), jnp.minimum(x[i], x[j])

    for i, j in _SORT16:
        cmpx(i, j)
    for shift in (4, 2, 1):
        y = [pltpu.roll(v, shift, 0) for v in x]
        x = [jnp.maximum(x[i], y[n - 1 - i]) for i in range(n)]
        stride = n // 2
        while stride:
            for i in range(n):
                if not i & stride:
                    cmpx(i, i + stride)
            stride //= 2
    top = [v[0:1] for v in x]
    nxt = jnp.max(jnp.where(s < top[n - 1], s, NEG_INF), axis=0, keepdims=True)
    return top + [nxt]


def _top_vals(s, k):
    out = []
    for r in range(k):
        m = jnp.max(s, axis=0, keepdims=True)
        out.append(m)
        if r < k - 1:
            s = jnp.where(s >= m, NEG_INF, s)
    return out


_CAND = [(i, j) for i in range(PEER_TOPK + 1) for j in range(PEER_TOPK + 1)
         if (i + 1) * (j + 1) <= PEER_TOPK + 1]
_NCAND = -(-len(_CAND) // 8) * 8


def _router_kernel(x_ref, wq_ref, sk_ref, r2_ref, e2_ref, n1_ref, w_ref, q_ref, cand_ref):
    q_ref[...] = _dot(x_ref[...].astype(BF16), wq_ref[...])
    tt = x_ref.shape[0]
    cand_ref[...] = jnp.full((_NCAND, tt), NEG_INF, F32)
    nk = PEER_NKEYS
    for h in range(PEER_HEADS):
        s1 = _dot3_nt(sk_ref[0], q_ref[:, (2 * h) * nk:(2 * h + 1) * nk])
        s2 = _dot3_nt(sk_ref[1], q_ref[:, (2 * h + 1) * nk:(2 * h + 2) * nk])
        a = _top16_sorted(s1)
        b = _top16_sorted(s2)
        for r, (i, j) in enumerate(_CAND):
            cand_ref[r:r + 1, :] = a[i] + b[j]
        v = _top_vals(cand_ref[...], PEER_TOPK + 1)
        thr = 0.5 * (v[PEER_TOPK - 1] + v[PEER_TOPK])
        z = functools.reduce(jnp.add, [jnp.exp(v[r] - v[0]) for r in range(PEER_TOPK)])
        cut = thr - s1
        rank2 = jnp.zeros_like(s2)
        count1 = jnp.zeros_like(s1)
        for r in range(PEER_TOPK):
            rank2 = jnp.where(s2 < b[r], r + 1.0, rank2)
            count1 = jnp.where(b[r] >= cut, r + 1.0, count1)
        r2_ref[h] = _pack_rows(rank2)
        e2_ref[h] = _pack_rows(jnp.exp(s2 - b[0]))
        n1_ref[h] = _pack_dup(count1)
        w_ref[h] = _pack_dup(jnp.exp(s1 - a[0]) / z)


def _router(xn, wq, subkeys, tt=256):
    t, d = xn.shape
    nq = wq.shape[1]
    shape = (PEER_HEADS, PEER_NKEYS, t)
    ospec = pl.BlockSpec((PEER_HEADS, PEER_NKEYS, tt), lambda i: (0, 0, i))
    pspec = pl.BlockSpec((PEER_HEADS, PEER_NKEYS // 2, tt), lambda i: (0, 0, i))
    packed = jax.ShapeDtypeStruct((PEER_HEADS, PEER_NKEYS // 2, t), jnp.uint32)
    return pl.pallas_call(
        _router_kernel,
        grid=(t // tt,),
        in_specs=[pl.BlockSpec((tt, d), lambda i: (i, 0)), _resident((d, nq)), _resident(subkeys.shape)],
        out_specs=[pspec, pspec, ospec, ospec],
        out_shape=[packed, packed, jax.ShapeDtypeStruct(shape, jnp.uint32),
                   jax.ShapeDtypeStruct(shape, jnp.uint32)],
        scratch_shapes=[pltpu.VMEM((tt, nq), F32), pltpu.VMEM((_NCAND, tt), F32)],
        compiler_params=_params("parallel"),
        name="peer_router",
    )(xn, wq, subkeys)


def _gelu(x):
    return 0.5 * x * (1.0 + lax.erf(x * (2.0 ** -0.5)))


_BF16_ROWS = 16
_PEER_SUB = 512
_PEER_COLS = 256
_PEER_KQ = 512


def _peer_kernel(x_ref, u_ref, v_ref, r2_ref, e2_ref, n1_ref, w_ref, y_ref, acc_ref, h_ref, act_ref,
                 *, n_tiles, n_e):
    s = pl.program_id(0)
    _, te, tt = h_ref.shape
    nk = PEER_NKEYS
    j_c = jnp.clip(s - 2, 0, n_tiles - 1) % n_e

    @pl.when(s == 0)
    def _():
        h_ref[...] = jnp.zeros_like(h_ref)
        act_ref[...] = jnp.zeros_like(act_ref)

    @pl.when(j_c == 0)
    def _():
        acc_ref[...] = jnp.zeros_like(acc_ref)

    n_i1 = te // nk

    def first_matmul(slot, k):
        xt = pltpu.bitcast(x_ref[...], BF16)
        uk = pltpu.bitcast(u_ref[pl.ds(k * _PEER_SUB // 2, _PEER_SUB // 2), :], BF16)
        h_ref[slot, pl.ds(k * _PEER_SUB, _PEER_SUB), :] = _dot(uk, xt)

    def second_matmul(slot, n, kq):
        cols = pl.ds(n * _PEER_COLS, _PEER_COLS)
        vq = pltpu.bitcast(v_ref[pl.ds(kq * _PEER_KQ // 2, _PEER_KQ // 2), :], BF16)
        acc_ref[:, cols] += lax.dot_general(vq, act_ref[slot, pl.ds(kq * _PEER_KQ, _PEER_KQ), cols],
                                            (((0,), (0,)), ((), ())), preferred_element_type=F32)

    def gate(slot, c, v):
        cols = pl.ds(c * 128, 128)
        prow = pl.ds(v * _SUBLANES, _SUBLANES)
        g = [jnp.zeros((_BF16_ROWS, 128), BF16) for _ in range(n_i1)]
        for h in range(PEER_HEADS):
            r2 = pltpu.bitcast(r2_ref[h, prow, cols], BF16)
            e2 = pltpu.bitcast(e2_ref[h, prow, cols], BF16)
            for ii in range(n_i1):
                row = pl.ds(ii, 1)
                n_row = pltpu.bitcast(jnp.broadcast_to(n1_ref[h, row, cols], (_SUBLANES, 128)), BF16)
                w_row = pltpu.bitcast(jnp.broadcast_to(w_ref[h, row, cols], (_SUBLANES, 128)), BF16)
                g[ii] = g[ii] + jnp.where(r2 < n_row, e2, 0.0) * w_row
        for ii in range(n_i1):
            rows = pl.ds(ii * nk + v * _BF16_ROWS, _BF16_ROWS)
            act_ref[slot, rows, cols] = _gelu(h_ref[slot, rows, cols]).astype(BF16) * g[ii]

    def stages(slot):
        pieces = [functools.partial(first_matmul, slot, k) for k in range(te // _PEER_SUB)]
        pieces += [functools.partial(second_matmul, slot, n, kq)
                   for n in range(tt // _PEER_COLS) for kq in range(te // _PEER_KQ)]
        blocks = [(c, v) for c in range(tt // 128) for v in range(nk // _BF16_ROWS)]
        per = -(-len(blocks) // len(pieces))
        for p, piece in enumerate(pieces):
            piece()
            for c, v in blocks[p * per:(p + 1) * per]:
                gate(1 - slot, c, v)

    @pl.when(s % 2 == 0)
    def _():
        stages(0)

    @pl.when(s % 2 == 1)
    def _():
        stages(1)

    @pl.when(j_c == n_e - 1)
    def _():
        y_ref[...] = acc_ref[...].T


def _peer_experts(xbt, u, v, r2, e2, n1, w, tt=512, te=2048):
    t, d = xbt.shape[1], 2 * xbt.shape[0]
    ne = 2 * u.shape[0]
    n_e = ne // te
    n_tiles = (t // tt) * n_e
    pair = lambda lag: (lambda s: jnp.clip(s - lag, 0, n_tiles - 1))
    tok = lambda lag: (lambda s: pair(lag)(s) // n_e)
    exp = lambda lag: (lambda s: pair(lag)(s) % n_e)
    wspec = lambda lag: pl.BlockSpec((te // 2, d), lambda s: (exp(lag)(s), 0))
    pspec = pl.BlockSpec((PEER_HEADS, PEER_NKEYS // 2, tt), lambda s: (0, 0, tok(1)(s)))
    rspec = pl.BlockSpec((PEER_HEADS, te // PEER_NKEYS, tt), lambda s: (0, exp(1)(s), tok(1)(s)))
    return pl.pallas_call(
        functools.partial(_peer_kernel, n_tiles=n_tiles, n_e=n_e),
        grid=(n_tiles + 2,),
        in_specs=[pl.BlockSpec((d // 2, tt), lambda s: (0, tok(0)(s))), wspec(0), wspec(2),
                  pspec, pspec, rspec, rspec],
        out_specs=pl.BlockSpec((tt, d), lambda s: (tok(2)(s), 0)),
        out_shape=jax.ShapeDtypeStruct((t, d), F32),
        scratch_shapes=[pltpu.VMEM((d, tt), F32), pltpu.VMEM((2, te, tt), F32), pltpu.VMEM((2, te, tt), BF16)],
        compiler_params=_params("arbitrary"),
        name="peer_experts",
    )(xbt, u, v, r2, e2, n1, w)


def _ln_ple_kernel(x_ref, y_ref, g_ref, b_ref, p_ref, wp_ref, wg_ref, bg_ref, o_ref, *, alpha):
    xn = _layer_norm(alpha * x_ref[...] + y_ref[...], g_ref[...], b_ref[...])
    gate = jax.nn.sigmoid(_dot(xn.astype(BF16), wg_ref[...]) + bg_ref[...])
    o_ref[...] = xn + gate * _dot(p_ref[...].astype(BF16), wp_ref[...])


def _ln_ple(x, y, g, b, p, layer, wp, wg, bg, alpha, tm=512):
    t, d = x.shape
    dp = p.shape[2]
    tile = pl.BlockSpec((tm, d), lambda i: (i, 0))
    vec = _resident((1, d))
    return pl.pallas_call(
        functools.partial(_ln_ple_kernel, alpha=alpha),
        grid=(t // tm,),
        in_specs=[tile, tile, vec, vec, pl.BlockSpec((None, tm, dp), lambda i: (layer, i, 0)), _resident((dp, d)),
                  _resident((d, d)), vec],
        out_specs=tile,
        out_shape=jax.ShapeDtypeStruct((t, d), F32),
        compiler_params=_params("parallel"),
        name="ln_ple",
    )(x, y, g.reshape(1, d), b.reshape(1, d), p, wp, wg, bg.reshape(1, d))


def _mixer_a(x, w_qkv, sinks):
    d = N_HEADS * HEAD_DIM
    kvw = A_KV_HEADS * HEAD_DIM
    wq, wk, wv = w_qkv[:, :d], w_qkv[:, d:d + kvw], w_qkv[:, d + kvw:]
    qt, k, vt = _proj_t(x, wq.T.astype(BF16), wk.astype(BF16), wv.T.astype(BF16))
    o, = _band_attention(qt, k, vt, kv_group=N_HEADS // A_KV_HEADS, dist_scale=1.0,
                         max_dist=A_WINDOW - 1, sinks=sinks)
    return [o], []


def _mixer_b(x, w_qkv):
    d = N_HEADS * HEAD_DIM
    ng = len(B_PATTERNS)
    wk = w_qkv[:, ng * d:(ng + 1) * d].astype(BF16)
    wvt = w_qkv[:, (ng + 1) * d:].T.astype(BF16)
    os_, lses = [], []
    for gi, (w, r) in enumerate(B_PATTERNS):
        qt, k, vt = _proj_t(x, w_qkv[:, gi * d:(gi + 1) * d].T.astype(BF16), wk, wvt, dil=r)
        o, lse = _band_attention(qt, k, vt, kv_group=1, dist_scale=float(r), max_dist=w // r, want_lse=True)
        os_.append(o)
        lses.append(lse)
    return os_, lses


def _mixer_c(x, w_qkv):
    d = N_HEADS * HEAD_DIM
    wq, wk, wv = w_qkv[:, :d], w_qkv[:, d:2 * d], w_qkv[:, 2 * d:]
    qt, k, vt, kmean = _proj_c(x, wq.T.astype(BF16), wk.astype(BF16), wv.T.astype(BF16))
    return [_moba_attention(qt, k, vt, kmean.reshape(-1, d))], []


def kernel(x, p, a_w_qkv, a_sinks, a_w_o, b_w_qkv, b_w_o, c_w_qkv, c_w_o, ln1_g, ln1_b, ln2_g, ln2_b,
           peer_w_q, peer_subkeys, peer_u, peer_v, ple_w, ple_gate_w, ple_gate_b):
    depth = p.shape[0]
    alpha = (2 * depth) ** 0.25
    bsz, seq, d = x.shape
    assert bsz == 1 and seq % (B_PATTERNS[-1][1] * BAND) == 0
    xt = x.reshape(seq, d)
    for i in range(depth):
        kind, j = i % 3, i // 3
        if kind == 0:
            os_, lses = _mixer_a(xt, a_w_qkv[j], a_sinks[j])
            wo = a_w_o[j]
        elif kind == 1:
            os_, lses = _mixer_b(xt, b_w_qkv[j])
            wo = b_w_o[j]
        else:
            os_, lses = _mixer_c(xt, c_w_qkv[j])
            wo = c_w_o[j]
        x1, x1b = _outproj_ln(os_, lses, wo.astype(BF16), xt, ln1_g[i], ln1_b[i], alpha)
        routing = _router(x1, peer_w_q[i].astype(BF16), peer_subkeys[i])
        y = _peer_experts(x1b, _pack_weight(peer_u, i), _pack_weight(peer_v, i), *routing)
        xt = _ln_ple(x1, y, ln2_g[i], ln2_b[i], p.reshape(depth, seq, -1), i, ple_w[i].astype(BF16),
                     ple_gate_w[i].astype(BF16), ple_gate_b[i], alpha)
    return xt.reshape(bsz, seq, d)
```

```python
import functools

import jax
import jax.numpy as jnp
import ml_dtypes
import numpy as np
from jax import lax
from jax.experimental import pallas as pl
from jax.experimental.pallas import tpu as pltpu

F32 = jnp.float32
BF16 = jnp.bfloat16
NEG_INF = float("-inf")
BIG = 1e30
LOG2E = 1.4426950408889634

D_MODEL = 1024
HEAD_DIM = 64
N_HEADS = 16
BAND = 128
A_KV_HEADS = 2
A_WINDOW = 128
B_PATTERNS = ((128, 1), (512, 4), (2048, 16))
C_BLOCK = 256
C_TOPK = 3
PEER_HEADS = 8
PEER_NKEYS = 128
PEER_TOPK = 16
LN_EPS = 1e-5
VMEM_LIMIT = 56 * 1024 * 1024
_SUBLANES = 8


def _params(*sem):
    return pltpu.CompilerParams(dimension_semantics=sem, vmem_limit_bytes=VMEM_LIMIT)


def _dot(a, b):
    return jnp.dot(a, b, preferred_element_type=F32)


def _dot_nt(a, b):
    return lax.dot_general(a, b, (((1,), (1,)), ((), ())), preferred_element_type=F32)


def _split(a):
    hi = a.astype(BF16)
    lo = (a - hi.astype(F32)).astype(BF16)
    return hi, lo


def _alibi_slope(h, n):
    return 2.0 ** (-8.0 * (h + 1) / n)


def _resident(shape):
    zeros = (0,) * len(shape)
    return pl.BlockSpec(shape, lambda *_: zeros)


def _proj_t_kernel(x_ref, wqt_ref, wk_ref, wvt_ref, qt_ref, k_ref, vt_ref, *, q_scale):
    xb = x_ref[...].astype(BF16)
    qt_ref[...] = _pack_rows(_dot_nt(wqt_ref[...], xb) * q_scale)
    k_ref[...] = _pack_rows(_dot(xb, wk_ref[...]))
    vt_ref[...] = _pack_rows(_dot_nt(wvt_ref[...], xb))


def _proj_t(x, wqt, wk, wvt, dil=1, tm=512):
    t, kdim = x.shape
    d, kw = wqt.shape[0], wk.shape[1]
    l = t // dil
    tm = min(tm, l)
    u32 = jnp.uint32
    return pl.pallas_call(
        functools.partial(_proj_t_kernel, q_scale=LOG2E * HEAD_DIM ** -0.5),
        grid=(dil, l // tm),
        in_specs=[pl.BlockSpec((tm, kdim), lambda c, i: (i, c)), _resident((d, kdim)), _resident((kdim, kw)),
                  _resident((kw, kdim))],
        out_specs=[pl.BlockSpec((None, d // 2, tm), lambda c, i: (c, 0, i)),
                   pl.BlockSpec((None, tm // 2, kw), lambda c, i: (c, i, 0)),
                   pl.BlockSpec((None, kw // 2, tm), lambda c, i: (c, 0, i))],
        out_shape=[jax.ShapeDtypeStruct((dil, d // 2, l), u32), jax.ShapeDtypeStruct((dil, l // 2, kw), u32),
                   jax.ShapeDtypeStruct((dil, kw // 2, l), u32)],
        compiler_params=_params("parallel", "parallel"),
        name="proj_band",
    )(x.reshape(l, dil * kdim), wqt, wk, wvt)


def _proj_c_kernel(x_ref, wqt_ref, wk_ref, wvt_ref, qt_ref, k_ref, vt_ref, km_ref, *, nblk):
    xb = x_ref[...].astype(BF16)
    qt_ref[...] = _dot_nt(wqt_ref[...], xb)
    kf = _dot(xb, wk_ref[...])
    k_ref[...] = kf.astype(BF16)
    vt_ref[...] = _dot_nt(wvt_ref[...], xb).astype(BF16)
    for r in range(nblk):
        km_ref[r] = jnp.mean(kf[r * C_BLOCK:(r + 1) * C_BLOCK], axis=0, keepdims=True)


def _proj_c(x, wqt, wk, wvt, tm=512):
    t, k = x.shape
    d = wk.shape[1]
    nblk = tm // C_BLOCK
    return pl.pallas_call(
        functools.partial(_proj_c_kernel, nblk=nblk),
        grid=(t // tm,),
        in_specs=[pl.BlockSpec((tm, k), lambda i: (i, 0)), _resident((d, k)), _resident((k, d)),
                  _resident((d, k))],
        out_specs=[pl.BlockSpec((d, tm), lambda i: (0, i)), pl.BlockSpec((tm, d), lambda i: (i, 0)),
                   pl.BlockSpec((d, tm), lambda i: (0, i)), pl.BlockSpec((nblk, 1, d), lambda i: (i, 0, 0))],
        out_shape=[jax.ShapeDtypeStruct((d, t), F32), jax.ShapeDtypeStruct((t, d), BF16),
                   jax.ShapeDtypeStruct((d, t), BF16), jax.ShapeDtypeStruct((t // C_BLOCK, 1, d), F32)],
        compiler_params=_params("parallel"),
        name="proj_moba",
    )(x, wqt, wk, wvt)


_N_AUG = 16
_PAIR = 2 * HEAD_DIM


def _split3_np(a):
    bf = ml_dtypes.bfloat16
    h = a.astype(bf).astype(np.float32)
    m = (a - h).astype(bf).astype(np.float32)
    l = (a - h - m).astype(bf).astype(np.float32)
    return h, m, l


def _pack_rows_np(a):
    bits = a.astype(ml_dtypes.bfloat16).view(np.uint16).astype(np.uint32)
    return bits[0::2] | (bits[1::2] << 16)


def _band_tables(dist_scale):
    ki = np.arange(2 * BAND, dtype=np.float32)
    kaug = np.zeros((2 * BAND, _PAIR), np.float32)
    kaug[:, 0:3] = 1.0
    kaug[:, 3:6] = ki[:, None]
    qaug = np.zeros((N_HEADS, _N_AUG, BAND), np.float32)
    qpos = np.arange(BAND, dtype=np.float32) + BAND
    for h in range(N_HEADS):
        c = np.float32(_alibi_slope(h, N_HEADS) * dist_scale * LOG2E)
        qaug[h, 0:3] = np.stack(_split3_np(-c * qpos))
        qaug[h, 3:6] = np.stack(_split3_np(np.full((BAND,), c, np.float32)))
    return _pack_rows_np(kaug), np.stack([_pack_rows_np(qaug[h]) for h in range(N_HEADS)])


def _band_kernel(*refs, kv_group, max_dist, use_sinks, want_lse):
    q_ref, kp_ref, ko_ref, vp_ref, vo_ref, kaug_ref, qaug_ref = refs[:7]
    pos = 7
    sink_ref = None
    if use_sinks:
        sink_ref = refs[pos]
        pos += 1
    o_ref = refs[pos]
    lse_ref = refs[pos + 1] if want_lse else None

    b = pl.program_id(1)
    ki = lax.broadcasted_iota(jnp.int32, (2 * BAND, BAND), 0)
    qi = lax.broadcasted_iota(jnp.int32, (2 * BAND, BAND), 1)
    dist = qi + BAND - ki
    valid = (dist >= 0) & (dist <= max_dist) & (ki >= jnp.where(b > 0, 0, BAND))
    qt = pltpu.bitcast(q_ref[...], BF16)
    k_all = jnp.concatenate([pltpu.bitcast(kp_ref[...], BF16), pltpu.bitcast(ko_ref[...], BF16)], axis=0)
    vt_all = jnp.concatenate([pltpu.bitcast(vp_ref[...], BF16), pltpu.bitcast(vo_ref[...], BF16)], axis=1)
    kaug = pltpu.bitcast(kaug_ref[...], BF16)
    zslot = jnp.zeros((HEAD_DIM, BAND), BF16)
    zpad = jnp.zeros((_PAIR - _N_AUG, BAND), BF16)
    n_kv = N_HEADS // kv_group
    scores = [None] * N_HEADS
    for pair in range(-(-n_kv // 2)):
        heads = [h for h in range(N_HEADS) if (h // kv_group) // 2 == pair]
        qcats = []
        for h in heads:
            qh = qt[h * HEAD_DIM:(h + 1) * HEAD_DIM]
            slot = [qh, zslot] if (h // kv_group) % 2 == 0 else [zslot, qh]
            qcats.append(jnp.concatenate(slot + [pltpu.bitcast(qaug_ref[h], BF16), zpad], axis=0))
        kcat = jnp.concatenate([k_all[:, pair * _PAIR:(pair + 1) * _PAIR], kaug], axis=1)
        s_all = _dot(kcat, jnp.concatenate(qcats, axis=1))
        for n, h in enumerate(heads):
            scores[h] = s_all[:, n * BAND:(n + 1) * BAND]
    probs, dens, lses = [], [], []
    for h in range(N_HEADS):
        s = jnp.where(valid, scores[h], NEG_INF)
        m = jnp.max(s, axis=0, keepdims=True)
        if use_sinks:
            sk = sink_ref[:, h:h + 1] * LOG2E
            m = jnp.maximum(m, sk)
        p = jnp.exp2(s - m)
        den = jnp.sum(p, axis=0, keepdims=True)
        if use_sinks:
            den = den + jnp.exp2(sk - m)
        probs.append(p.astype(BF16))
        dens.append(den)
        if want_lse:
            lses.append((m + jnp.log2(den)) * (1.0 / LOG2E))
    outs = []
    for g in range(n_kv):
        hs = range(g * kv_group, (g + 1) * kv_group)
        pv = _dot(vt_all[g * HEAD_DIM:(g + 1) * HEAD_DIM], jnp.concatenate([probs[h] for h in hs], axis=1))
        outs += [pv[:, n * BAND:(n + 1) * BAND] / dens[h] for n, h in enumerate(hs)]
    o_ref[...] = jnp.concatenate(outs, axis=0).T.astype(o_ref.dtype)
    if want_lse:
        lse_ref[...] = jnp.concatenate(lses, axis=0)


def _band_attention(qt, k, vt, *, kv_group, dist_scale, max_dist, sinks=None, want_lse=False):
    r, _, l = qt.shape
    d = N_HEADS * HEAD_DIM
    kw = k.shape[2]
    prev = lambda c, b: jnp.maximum(b - 1, 0)
    kaug, qaug = _band_tables(dist_scale)
    in_specs = [pl.BlockSpec((None, d // 2, BAND), lambda c, b: (c, 0, b)),
                pl.BlockSpec((None, BAND // 2, kw), lambda c, b: (c, prev(c, b), 0)),
                pl.BlockSpec((None, BAND // 2, kw), lambda c, b: (c, b, 0)),
                pl.BlockSpec((None, kw // 2, BAND), lambda c, b: (c, 0, prev(c, b))),
                pl.BlockSpec((None, kw // 2, BAND), lambda c, b: (c, 0, b)),
                _resident(kaug.shape), _resident(qaug.shape)]
    args = [qt, k, k, vt, vt, jnp.asarray(kaug), jnp.asarray(qaug)]
    if sinks is not None:
        in_specs.append(_resident((1, N_HEADS)))
        args.append(sinks.reshape(1, N_HEADS).astype(F32))
    o_spec = pl.BlockSpec((BAND, d), lambda c, b: (b, c))
    out_specs = [o_spec]
    out_shape = [jax.ShapeDtypeStruct((l, r * d), BF16)]
    if want_lse:
        out_specs.append(pl.BlockSpec((None, N_HEADS, BAND), lambda c, b: (c, 0, b)))
        out_shape.append(jax.ShapeDtypeStruct((r, N_HEADS, l), F32))
    outs = pl.pallas_call(
        functools.partial(_band_kernel, kv_group=kv_group, max_dist=max_dist,
                          use_sinks=sinks is not None, want_lse=want_lse),
        grid=(r, l // BAND),
        in_specs=in_specs, out_specs=out_specs, out_shape=out_shape,
        compiler_params=_params("parallel", "parallel"),
        name="band_attention",
    )(*args)
    o = outs[0].reshape(l * r, d)
    if not want_lse:
        return [o]
    return [o, jnp.transpose(outs[1], (1, 2, 0)).reshape(N_HEADS, l * r)]


def _dot3(a, b):
    ah, al = _split(a)
    bh, bl = _split(b)
    return _dot(ah, bh) + (_dot(ah, bl) + _dot(al, bh))


def _dot3_nt(a, b):
    ah, al = _split(a)
    bh, bl = _split(b)
    return _dot_nt(ah, bh) + (_dot_nt(ah, bl) + _dot_nt(al, bh))


def _split3(a):
    h = a.astype(BF16).astype(F32)
    r = a - h
    m = r.astype(BF16).astype(F32)
    l = (r - m).astype(BF16).astype(F32)
    return h, m, l


_KV_STEP = 2 * C_BLOCK


def _moba_kernel(slope_ref, qt_ref, k_ref, vt_ref, km_ref, o_ref, sel_ref, s_ref, p_ref, *, nblk):
    hp = pl.program_id(0)
    qi = pl.program_id(1)
    tq = C_BLOCK
    pw = 2 * HEAD_DIM
    qt = qt_ref[...]
    rows = lax.broadcasted_iota(jnp.int32, (pw, tq), 0)
    blk = lax.broadcasted_iota(jnp.int32, (nblk, tq), 0)
    past = blk < qi
    aug_r = lax.broadcasted_iota(jnp.int32, (_N_AUG, tq), 0)
    tqf = lax.broadcasted_iota(jnp.int32, (_N_AUG, tq), 1).astype(F32)
    kcol = lax.broadcasted_iota(jnp.int32, (_KV_STEP, pw), 1)
    tkf = lax.broadcasted_iota(jnp.int32, (_KV_STEP, pw), 0).astype(F32)
    causal = (lax.broadcasted_iota(jnp.int32, (C_BLOCK, tq), 1)
              >= lax.broadcasted_iota(jnp.int32, (C_BLOCK, tq), 0))
    zpad = jnp.zeros((pw - _N_AUG, tq), BF16)
    k0 = pl.multiple_of(qi * C_BLOCK, C_BLOCK)
    k_own = k_ref[pl.ds(k0, C_BLOCK), :]

    qcat, kpos, c2, init = [], [], [], []
    for hh in range(2):
        c = slope_ref[2 * hp + hh] * LOG2E
        qm = jnp.where((rows >= hh * HEAD_DIM) & (rows < (hh + 1) * HEAD_DIM), qt, 0.0)
        gate = jnp.where(past, _dot3(km_ref[...], qm), NEG_INF)
        g = gate
        for r in range(C_TOPK):
            thr = jnp.max(g, axis=0, keepdims=True)
            if r < C_TOPK - 1:
                g = jnp.where(g >= thr, NEG_INF, g)
        sel_ref[hh] = jnp.where(past & (gate >= thr), 1.0, 0.0)

        qh, qmid, ql = _split3(-c * tqf)
        qpos = jnp.where(aug_r == 0, qh, jnp.where(aug_r == 1, qmid, jnp.where(aug_r == 2, ql,
                         jnp.where(aug_r < 6, 1.0, 0.0))))
        qb = (qm * (LOG2E * HEAD_DIM ** -0.5)).astype(BF16)
        qcat.append(jnp.concatenate([qb, qpos.astype(BF16), zpad], axis=0))
        kh, kmid, kl = _split3(c * tkf)
        kpos.append(jnp.where(kcol < 3, 1.0, jnp.where(kcol == 3, kh, jnp.where(kcol == 4, kmid,
                              jnp.where(kcol == 5, kl, 0.0)))).astype(BF16))
        c2.append(c)

        s = _dot(jnp.concatenate([k_own, kpos[hh][:C_BLOCK]], axis=1), qcat[hh])
        s = jnp.where(causal, s, NEG_INF)
        m0 = jnp.max(s, axis=0, keepdims=True)
        p = jnp.exp2(s - m0)
        l0 = jnp.sum(p, axis=0, keepdims=True)
        acc0 = _dot(vt_ref[pl.ds(hh * HEAD_DIM, HEAD_DIM), pl.ds(k0, C_BLOCK)], p.astype(BF16))
        init.append((m0, l0, acc0))

    nsub = _KV_STEP // C_BLOCK

    nsteps = (qi + nsub - 1) // nsub
    last_step = k_ref.shape[0] // _KV_STEP - 1

    def key_start(n):
        return pl.multiple_of(jnp.clip(n, 0, last_step) * _KV_STEP, _KV_STEP)

    def score_piece(n, slot, hh, j):
        rows = slice(j * C_BLOCK, (j + 1) * C_BLOCK)
        kb = k_ref[pl.ds(key_start(n) + j * C_BLOCK, C_BLOCK), :]
        s_ref[slot, hh, rows, :] = _dot(jnp.concatenate([kb, kpos[hh][rows]], axis=1), qcat[hh])

    def scores(n, slot):
        for hh in range(2):
            for j in range(nsub):
                score_piece(n, slot, hh, j)

    def pv(n, hh, slot):
        return _dot(vt_ref[pl.ds(hh * HEAD_DIM, HEAD_DIM), pl.ds(key_start(n), _KV_STEP)],
                    p_ref[slot, hh])

    def step(n, slot, carry):
        a_prev, state = carry
        off = (qi * C_BLOCK - n * _KV_STEP).astype(F32)
        new_a, new_state = [], []
        for hh in range(2):
            m_i, l_i, acc = state[hh]
            shift = c2[hh] * off
            sel = [sel_ref[hh, pl.ds(nsub * n + j, 1), :] > 0.5 for j in range(nsub)]
            m_new = m_i
            for j in range(nsub):
                score_piece(n + 1, 1 - slot, hh, j)
                mj = jnp.max(s_ref[slot, hh, j * C_BLOCK:(j + 1) * C_BLOCK, :], axis=0, keepdims=True)
                m_new = jnp.maximum(m_new, jnp.where(sel[j], mj - shift, NEG_INF))
            acc = a_prev[hh] * acc + pv(n - 1, hh, 1 - slot)
            l_new = jnp.exp2(m_i - m_new) * l_i
            for j in range(nsub):
                rows = slice(j * C_BLOCK, (j + 1) * C_BLOCK)
                p = jnp.exp2(s_ref[slot, hh, rows, :] - jnp.where(sel[j], m_new + shift, BIG))
                p_ref[slot, hh, rows, :] = p.astype(BF16)
                l_new = l_new + jnp.sum(p, axis=0, keepdims=True)
            new_a.append(jnp.exp2(m_i - m_new))
            new_state.append((m_new, l_new, acc))
        return tuple(new_a), tuple(new_state)

    def body(n2, carry):
        return step(2 * n2 + 1, 1, step(2 * n2, 0, carry))

    p_ref[1] = jnp.zeros_like(p_ref[1])
    scores(0, 0)
    one_a = jnp.ones((1, tq), F32)
    npairs = (nsteps + 1) // 2
    a_prev, state = lax.fori_loop(0, npairs, body, ((one_a, one_a), tuple(init)))
    outs = []
    for hh in range(2):
        _, l, acc = state[hh]
        outs.append((a_prev[hh] * acc + pv(2 * npairs - 1, hh, 1)) / l)
    o_ref[...] = jnp.concatenate(outs, axis=0).T.astype(o_ref.dtype)


def _moba_attention(qt, k, vt, kmean):
    d, t = qt.shape
    nblk = t // C_BLOCK
    slopes = jnp.asarray([_alibi_slope(h, N_HEADS) for h in range(N_HEADS)], F32)
    pw = 2 * HEAD_DIM
    assert t % (2 * _KV_STEP) == 0
    return pl.pallas_call(
        functools.partial(_moba_kernel, nblk=nblk),
        grid=(d // pw, nblk),
        in_specs=[pl.BlockSpec(memory_space=pltpu.SMEM),
                  pl.BlockSpec((pw, C_BLOCK), lambda hp, i: (hp, i)),
                  pl.BlockSpec((t, pw), lambda hp, i: (0, hp)),
                  pl.BlockSpec((pw, t), lambda hp, i: (hp, 0)),
                  pl.BlockSpec((nblk, pw), lambda hp, i: (0, hp))],
        out_specs=pl.BlockSpec((C_BLOCK, pw), lambda hp, i: (i, hp)),
        out_shape=jax.ShapeDtypeStruct((t, d), BF16),
        scratch_shapes=[pltpu.VMEM((2, nblk, C_BLOCK), F32), pltpu.VMEM((2, 2, _KV_STEP, C_BLOCK), F32),
                        pltpu.VMEM((2, 2, _KV_STEP, C_BLOCK), BF16)],
        compiler_params=_params("parallel", "arbitrary"),
        name="moba_attention",
    )(slopes, qt, k, vt, kmean)


def _layer_norm(z, g, b):
    mu = jnp.mean(z, axis=-1, keepdims=True)
    zc = z - mu
    var = jnp.mean(zc * zc, axis=-1, keepdims=True)
    return zc * lax.rsqrt(var + LN_EPS) * g + b


def _outproj_ln_kernel(*refs, n_groups, alpha):
    o_refs = refs[:n_groups]
    pos = n_groups
    if n_groups == 1:
        o = o_refs[0][...]
    else:
        lse_ref = refs[pos]
        pos += 1
        lses = [lse_ref[gi] for gi in range(n_groups)]
        m = functools.reduce(jnp.maximum, lses)
        es = [jnp.exp(l - m) for l in lses]
        den = functools.reduce(jnp.add, es)
        heads = lax.broadcasted_iota(jnp.int32, (N_HEADS, N_HEADS * HEAD_DIM), 0)
        cols = lax.broadcasted_iota(jnp.int32, (N_HEADS, N_HEADS * HEAD_DIM), 1)
        expand = jnp.where(cols // HEAD_DIM == heads, 1.0, 0.0).astype(BF16)
        tn = (((0,), (0,)), ((), ()))
        o = None
        for e, r in zip(es, o_refs):
            whi, wlo = _split(e / den)
            wexp = (lax.dot_general(whi, expand, tn, preferred_element_type=F32)
                    + lax.dot_general(wlo, expand, tn, preferred_element_type=F32))
            term = wexp * r[...].astype(F32)
            o = term if o is None else o + term
        o = o.astype(BF16)
    wo_ref, x_ref, g_ref, b_ref, xn_ref, xb_ref = refs[pos:pos + 6]
    y = _dot(o, wo_ref[...])
    xn = _layer_norm(alpha * x_ref[...] + y, g_ref[...], b_ref[...])
    xn_ref[...] = xn
    xb_ref[...] = _pack_rows(xn.T)


def _outproj_ln(os_, lses, wo, x, g, b, alpha, tm=512):
    t, d = x.shape
    n = len(os_)
    tile = pl.BlockSpec((tm, d), lambda i: (i, 0))
    lse_args = [jnp.stack(lses)] if lses else []
    lse_specs = [pl.BlockSpec((n, N_HEADS, tm), lambda i: (0, 0, i))] if lses else []
    return pl.pallas_call(
        functools.partial(_outproj_ln_kernel, n_groups=n, alpha=alpha),
        grid=(t // tm,),
        in_specs=[tile] * n + lse_specs + [_resident((d, d)), tile, _resident((1, d)), _resident((1, d))],
        out_specs=[tile, pl.BlockSpec((d // 2, tm), lambda i: (0, i))],
        out_shape=[jax.ShapeDtypeStruct((t, d), F32), jax.ShapeDtypeStruct((d // 2, t), jnp.uint32)],
        compiler_params=_params("parallel"),
        name="outproj_ln",
    )(*os_, *lse_args, wo, x, g.reshape(1, d), b.reshape(1, d))


def _pack_rows(a):
    return pltpu.bitcast(a.astype(BF16), jnp.uint32)


def _pack_dup(a):
    bits = pltpu.bitcast(a.astype(BF16).astype(F32), jnp.uint32)
    return bits | (bits >> 16)


def _pack_weight_kernel(w_ref, o_ref, *, transpose):
    w = w_ref[...]
    o_ref[...] = _pack_rows(w.T if transpose else w)


def _pack_weight(w, layer, transpose=False, tr=2048):
    _, r, c = w.shape
    if transpose:
        out_spec, out_shape = pl.BlockSpec((c // 2, tr), lambda i: (0, i)), (c // 2, r)
    else:
        out_spec, out_shape = pl.BlockSpec((tr // 2, c), lambda i: (i, 0)), (r // 2, c)
    return pl.pallas_call(
        functools.partial(_pack_weight_kernel, transpose=transpose),
        grid=(r // tr,),
        in_specs=[pl.BlockSpec((None, tr, c), lambda i: (layer, i, 0))],
        out_specs=out_spec,
        out_shape=jax.ShapeDtypeStruct(out_shape, jnp.uint32),
        compiler_params=_params("parallel"),
        name="pack_weight",
    )(w)


def _oddeven_merge(lo, hi, r):
    step = r * 2
    if step < hi - lo:
        yield from _oddeven_merge(lo, hi, step)
        yield from _oddeven_merge(lo + r, hi, step)
        yield from [(i, i + r) for i in range(lo + r, hi - r, step)]
    else:
        yield (lo, lo + r)


def _oddeven_sort(lo, hi):
    if hi > lo:
        mid = lo + (hi - lo) // 2
        yield from _oddeven_sort(lo, mid)
        yield from _oddeven_sort(mid + 1, hi)
        yield from _oddeven_merge(lo, hi, 1)


_SORT16 = tuple(_oddeven_sort(0, PEER_TOPK - 1))


def _top16_sorted(s):
    n = PEER_TOPK
    x = [s[_SUBLANES * j:_SUBLANES * (j + 1)] for j in range(n)]

    def cmpx(i, j):
        x[i], x[j] = jnp.maximum(x[i], x[j]), jnp.minimum(x[i], x[j])

    for i, j in _SORT16:
        cmpx(i, j)
    for shift in (4, 2, 1):
        y = [pltpu.roll(v, shift, 0) for v in x]
        x = [jnp.maximum(x[i], y[n - 1 - i]) for i in range(n)]
        stride = n // 2
        while stride:
            for i in range(n):
                if not i & stride:
                    cmpx(i, i + stride)
            stride //= 2
    top = [v[0:1] for v in x]
    nxt = jnp.max(jnp.where(s < top[n - 1], s, NEG_INF), axis=0, keepdims=True)
    return top + [nxt]


def _top_vals(s, k):
    out = []
    for r in range(k):
        m = jnp.max(s, axis=0, keepdims=True)
        out.append(m)
        if r < k - 1:
            s = jnp.where(s >= m, NEG_INF, s)
    return out


_CAND = [(i, j) for i in range(PEER_TOPK + 1) for j in range(PEER_TOPK + 1)
         if (i + 1) * (j + 1) <= PEER_TOPK + 1]
_NCAND = -(-len(_CAND) // 8) * 8


def _router_kernel(x_ref, wq_ref, sk_ref, r2_ref, e2_ref, n1_ref, w_ref, q_ref, cand_ref):
    q_ref[...] = _dot(x_ref[...].astype(BF16), wq_ref[...])
    tt = x_ref.shape[0]
    cand_ref[...] = jnp.full((_NCAND, tt), NEG_INF, F32)
    nk = PEER_NKEYS
    for h in range(PEER_HEADS):
        s1 = _dot3_nt(sk_ref[0], q_ref[:, (2 * h) * nk:(2 * h + 1) * nk])
        s2 = _dot3_nt(sk_ref[1], q_ref[:, (2 * h + 1) * nk:(2 * h + 2) * nk])
        a = _top16_sorted(s1)
        b = _top16_sorted(s2)
        for r, (i, j) in enumerate(_CAND):
            cand_ref[r:r + 1, :] = a[i] + b[j]
        v = _top_vals(cand_ref[...], PEER_TOPK + 1)
        thr = 0.5 * (v[PEER_TOPK - 1] + v[PEER_TOPK])
        z = functools.reduce(jnp.add, [jnp.exp(v[r] - v[0]) for r in range(PEER_TOPK)])
        cut = thr - s1
        rank2 = jnp.zeros_like(s2)
        count1 = jnp.zeros_like(s1)
        for r in range(PEER_TOPK):
            rank2 = jnp.where(s2 < b[r], r + 1.0, rank2)
            count1 = jnp.where(b[r] >= cut, r + 1.0, count1)
        r2_ref[h] = _pack_rows(rank2)
        e2_ref[h] = _pack_rows(jnp.exp(s2 - b[0]))
        n1_ref[h] = _pack_dup(count1)
        w_ref[h] = _pack_dup(jnp.exp(s1 - a[0]) / z)


def _router(xn, wq, subkeys, tt=256):
    t, d = xn.shape
    nq = wq.shape[1]
    shape = (PEER_HEADS, PEER_NKEYS, t)
    ospec = pl.BlockSpec((PEER_HEADS, PEER_NKEYS, tt), lambda i: (0, 0, i))
    pspec = pl.BlockSpec((PEER_HEADS, PEER_NKEYS // 2, tt), lambda i: (0, 0, i))
    packed = jax.ShapeDtypeStruct((PEER_HEADS, PEER_NKEYS // 2, t), jnp.uint32)
    return pl.pallas_call(
        _router_kernel,
        grid=(t // tt,),
        in_specs=[pl.BlockSpec((tt, d), lambda i: (i, 0)), _resident((d, nq)), _resident(subkeys.shape)],
        out_specs=[pspec, pspec, ospec, ospec],
        out_shape=[packed, packed, jax.ShapeDtypeStruct(shape, jnp.uint32),
                   jax.ShapeDtypeStruct(shape, jnp.uint32)],
        scratch_shapes=[pltpu.VMEM((tt, nq), F32), pltpu.VMEM((_NCAND, tt), F32)],
        compiler_params=_params("parallel"),
        name="peer_router",
    )(xn, wq, subkeys)


def _gelu(x):
    return 0.5 * x * (1.0 + lax.erf(x * (2.0 ** -0.5)))


_BF16_ROWS = 16
_PEER_SUB = 512
_PEER_COLS = 256
_PEER_KQ = 512


def _peer_kernel(x_ref, u_ref, vt_ref, r2_ref, e2_ref, n1_ref, w_ref, y_ref, acc_ref, h_ref, act_ref,
                 *, n_tiles, n_e):
    s = pl.program_id(0)
    _, te, tt = h_ref.shape
    nk = PEER_NKEYS
    j_c = jnp.clip(s - 2, 0, n_tiles - 1) % n_e

    @pl.when(s == 0)
    def _():
        h_ref[...] = jnp.zeros_like(h_ref)
        act_ref[...] = jnp.zeros_like(act_ref)

    @pl.when(j_c == 0)
    def _():
        acc_ref[...] = jnp.zeros_like(acc_ref)

    n_i1 = te // nk

    def first_matmul(slot, k):
        xt = pltpu.bitcast(x_ref[...], BF16)
        uk = pltpu.bitcast(u_ref[pl.ds(k * _PEER_SUB // 2, _PEER_SUB // 2), :], BF16)
        h_ref[slot, pl.ds(k * _PEER_SUB, _PEER_SUB), :] = _dot(uk, xt)

    def second_matmul(slot, n, kq):
        cols = pl.ds(n * _PEER_COLS, _PEER_COLS)
        krows = pl.ds(kq * _PEER_KQ, _PEER_KQ)
        acc_ref[:, cols] += _dot(pltpu.bitcast(vt_ref[:, krows], BF16), act_ref[slot, krows, cols])

    def gate(slot, c, v):
        cols = pl.ds(c * 128, 128)
        prow = pl.ds(v * _SUBLANES, _SUBLANES)
        g = [jnp.zeros((_BF16_ROWS, 128), BF16) for _ in range(n_i1)]
        for h in range(PEER_HEADS):
            r2 = pltpu.bitcast(r2_ref[h, prow, cols], BF16)
            e2 = pltpu.bitcast(e2_ref[h, prow, cols], BF16)
            for ii in range(n_i1):
                row = pl.ds(ii, 1)
                n_row = pltpu.bitcast(jnp.broadcast_to(n1_ref[h, row, cols], (_SUBLANES, 128)), BF16)
                w_row = pltpu.bitcast(jnp.broadcast_to(w_ref[h, row, cols], (_SUBLANES, 128)), BF16)
                g[ii] = g[ii] + jnp.where(r2 < n_row, e2, 0.0) * w_row
        for ii in range(n_i1):
            rows = pl.ds(ii * nk + v * _BF16_ROWS, _BF16_ROWS)
            act_ref[slot, rows, cols] = _gelu(h_ref[slot, rows, cols]).astype(BF16) * g[ii]

    def stages(slot):
        pieces = [functools.partial(first_matmul, slot, k) for k in range(te // _PEER_SUB)]
        pieces += [functools.partial(second_matmul, slot, n, kq)
                   for n in range(tt // _PEER_COLS) for kq in range(te // _PEER_KQ)]
        blocks = [(c, v) for c in range(tt // 128) for v in range(nk // _BF16_ROWS)]
        per = -(-len(blocks) // len(pieces))
        for p, piece in enumerate(pieces):
            piece()
            for c, v in blocks[p * per:(p + 1) * per]:
                gate(1 - slot, c, v)

    @pl.when(s % 2 == 0)
    def _():
        stages(0)

    @pl.when(s % 2 == 1)
    def _():
        stages(1)

    @pl.when(j_c == n_e - 1)
    def _():
        y_ref[...] = acc_ref[...].T


def _peer_experts(xbt, u, vt, r2, e2, n1, w, tt=512, te=2048):
    t, d = xbt.shape[1], 2 * xbt.shape[0]
    ne = 2 * u.shape[0]
    n_e = ne // te
    n_tiles = (t // tt) * n_e
    pair = lambda lag: (lambda s: jnp.clip(s - lag, 0, n_tiles - 1))
    tok = lambda lag: (lambda s: pair(lag)(s) // n_e)
    exp = lambda lag: (lambda s: pair(lag)(s) % n_e)
    wspec = lambda lag: pl.BlockSpec((te // 2, d), lambda s: (exp(lag)(s), 0))
    pspec = pl.BlockSpec((PEER_HEADS, PEER_NKEYS // 2, tt), lambda s: (0, 0, tok(1)(s)))
    rspec = pl.BlockSpec((PEER_HEADS, te // PEER_NKEYS, tt), lambda s: (0, exp(1)(s), tok(1)(s)))
    return pl.pallas_call(
        functools.partial(_peer_kernel, n_tiles=n_tiles, n_e=n_e),
        grid=(n_tiles + 2,),
        in_specs=[pl.BlockSpec((d // 2, tt), lambda s: (0, tok(0)(s))), wspec(0),
                  pl.BlockSpec((d // 2, te), lambda s: (0, exp(2)(s))),
                  pspec, pspec, rspec, rspec],
        out_specs=pl.BlockSpec((tt, d), lambda s: (tok(2)(s), 0)),
        out_shape=jax.ShapeDtypeStruct((t, d), F32),
        scratch_shapes=[pltpu.VMEM((d, tt), F32), pltpu.VMEM((2, te, tt), F32), pltpu.VMEM((2, te, tt), BF16)],
        compiler_params=_params("arbitrary"),
        name="peer_experts",
    )(xbt, u, vt, r2, e2, n1, w)


def _ln_ple_kernel(x_ref, y_ref, g_ref, b_ref, p_ref, wp_ref, wg_ref, bg_ref, o_ref, *, alpha):
    xn = _layer_norm(alpha * x_ref[...] + y_ref[...], g_ref[...], b_ref[...])
    gate = jax.nn.sigmoid(_dot(xn.astype(BF16), wg_ref[...]) + bg_ref[...])
    o_ref[...] = xn + gate * _dot(p_ref[...].astype(BF16), wp_ref[...])


def _ln_ple(x, y, g, b, p, layer, wp, wg, bg, alpha, tm=512):
    t, d = x.shape
    dp = p.shape[2]
    tile = pl.BlockSpec((tm, d), lambda i: (i, 0))
    vec = _resident((1, d))
    return pl.pallas_call(
        functools.partial(_ln_ple_kernel, alpha=alpha),
        grid=(t // tm,),
        in_specs=[tile, tile, vec, vec, pl.BlockSpec((None, tm, dp), lambda i: (layer, i, 0)), _resident((dp, d)),
                  _resident((d, d)), vec],
        out_specs=tile,
        out_shape=jax.ShapeDtypeStruct((t, d), F32),
        compiler_params=_params("parallel"),
        name="ln_ple",
    )(x, y, g.reshape(1, d), b.reshape(1, d), p, wp, wg, bg.reshape(1, d))


def _mixer_a(x, w_qkv, sinks):
    d = N_HEADS * HEAD_DIM
    kvw = A_KV_HEADS * HEAD_DIM
    wq, wk, wv = w_qkv[:, :d], w_qkv[:, d:d + kvw], w_qkv[:, d + kvw:]
    qt, k, vt = _proj_t(x, wq.T.astype(BF16), wk.astype(BF16), wv.T.astype(BF16))
    o, = _band_attention(qt, k, vt, kv_group=N_HEADS // A_KV_HEADS, dist_scale=1.0,
                         max_dist=A_WINDOW - 1, sinks=sinks)
    return [o], []


def _mixer_b(x, w_qkv):
    d = N_HEADS * HEAD_DIM
    ng = len(B_PATTERNS)
    wk = w_qkv[:, ng * d:(ng + 1) * d].astype(BF16)
    wvt = w_qkv[:, (ng + 1) * d:].T.astype(BF16)
    os_, lses = [], []
    for gi, (w, r) in enumerate(B_PATTERNS):
        qt, k, vt = _proj_t(x, w_qkv[:, gi * d:(gi + 1) * d].T.astype(BF16), wk, wvt, dil=r)
        o, lse = _band_attention(qt, k, vt, kv_group=1, dist_scale=float(r), max_dist=w // r, want_lse=True)
        os_.append(o)
        lses.append(lse)
    return os_, lses


def _mixer_c(x, w_qkv):
    d = N_HEADS * HEAD_DIM
    wq, wk, wv = w_qkv[:, :d], w_qkv[:, d:2 * d], w_qkv[:, 2 * d:]
    qt, k, vt, kmean = _proj_c(x, wq.T.astype(BF16), wk.astype(BF16), wv.T.astype(BF16))
    return [_moba_attention(qt, k, vt, kmean.reshape(-1, d))], []


def kernel(x, p, a_w_qkv, a_sinks, a_w_o, b_w_qkv, b_w_o, c_w_qkv, c_w_o, ln1_g, ln1_b, ln2_g, ln2_b,
           peer_w_q, peer_subkeys, peer_u, peer_v, ple_w, ple_gate_w, ple_gate_b):
    depth = p.shape[0]
    alpha = (2 * depth) ** 0.25
    bsz, seq, d = x.shape
    assert bsz == 1 and seq % (B_PATTERNS[-1][1] * BAND) == 0
    xt = x.reshape(seq, d)
    for i in range(depth):
        kind, j = i % 3, i // 3
        if kind == 0:
            os_, lses = _mixer_a(xt, a_w_qkv[j], a_sinks[j])
            wo = a_w_o[j]
        elif kind == 1:
            os_, lses = _mixer_b(xt, b_w_qkv[j])
            wo = b_w_o[j]
        else:
            os_, lses = _mixer_c(xt, c_w_qkv[j])
            wo = c_w_o[j]
        x1, x1b = _outproj_ln(os_, lses, wo.astype(BF16), xt, ln1_g[i], ln1_b[i], alpha)
        routing = _router(x1, peer_w_q[i].astype(BF16), peer_subkeys[i])
        y = _peer_experts(x1b, _pack_weight(peer_u, i), _pack_weight(peer_v, i, transpose=True), *routing)
        xt = _ln_ple(x1, y, ln2_g[i], ln2_b[i], p.reshape(depth, seq, -1), i, ple_w[i].astype(BF16),
                     ple_gate_w[i].astype(BF16), ple_gate_b[i], alpha)
    return xt.reshape(bsz, seq, d)
```

```python
import functools

import jax
import jax.numpy as jnp
import ml_dtypes
import numpy as np
from jax import lax
from jax.experimental import pallas as pl
from jax.experimental.pallas import tpu as pltpu

F32 = jnp.float32
BF16 = jnp.bfloat16
NEG_INF = float("-inf")
BIG = 1e30
LOG2E = 1.4426950408889634

D_MODEL = 1024
HEAD_DIM = 64
N_HEADS = 16
BAND = 128
A_KV_HEADS = 2
A_WINDOW = 128
B_PATTERNS = ((128, 1), (512, 4), (2048, 16))
C_BLOCK = 256
C_TOPK = 3
PEER_HEADS = 8
PEER_NKEYS = 128
PEER_TOPK = 16
LN_EPS = 1e-5
VMEM_LIMIT = 56 * 1024 * 1024
_SUBLANES = 8


def _params(*sem):
    return pltpu.CompilerParams(dimension_semantics=sem, vmem_limit_bytes=VMEM_LIMIT)


def _dot(a, b):
    return jnp.dot(a, b, preferred_element_type=F32)


def _dot_nt(a, b):
    return lax.dot_general(a, b, (((1,), (1,)), ((), ())), preferred_element_type=F32)


def _split(a):
    hi = a.astype(BF16)
    lo = (a - hi.astype(F32)).astype(BF16)
    return hi, lo


def _alibi_slope(h, n):
    return 2.0 ** (-8.0 * (h + 1) / n)


def _resident(shape):
    zeros = (0,) * len(shape)
    return pl.BlockSpec(shape, lambda *_: zeros)


def _proj_t_kernel(x_ref, wqt_ref, wk_ref, wvt_ref, qt_ref, k_ref, vt_ref, *, q_scale):
    xb = x_ref[...].astype(BF16)
    qt_ref[...] = _pack_rows(_dot_nt(wqt_ref[...], xb) * q_scale)
    k_ref[...] = _pack_rows(_dot(xb, wk_ref[...]))
    vt_ref[...] = _pack_rows(_dot_nt(wvt_ref[...], xb))


def _proj_t(x, wqt, wk, wvt, dil=1, tm=512):
    t, kdim = x.shape
    d, kw = wqt.shape[0], wk.shape[1]
    l = t // dil
    tm = min(tm, l)
    u32 = jnp.uint32
    return pl.pallas_call(
        functools.partial(_proj_t_kernel, q_scale=LOG2E * HEAD_DIM ** -0.5),
        grid=(dil, l // tm),
        in_specs=[pl.BlockSpec((tm, kdim), lambda c, i: (i, c)), _resident((d, kdim)), _resident((kdim, kw)),
                  _resident((kw, kdim))],
        out_specs=[pl.BlockSpec((None, d // 2, tm), lambda c, i: (c, 0, i)),
                   pl.BlockSpec((None, tm // 2, kw), lambda c, i: (c, i, 0)),
                   pl.BlockSpec((None, kw // 2, tm), lambda c, i: (c, 0, i))],
        out_shape=[jax.ShapeDtypeStruct((dil, d // 2, l), u32), jax.ShapeDtypeStruct((dil, l // 2, kw), u32),
                   jax.ShapeDtypeStruct((dil, kw // 2, l), u32)],
        compiler_params=_params("parallel", "parallel"),
        name="proj_band",
    )(x.reshape(l, dil * kdim), wqt, wk, wvt)


def _proj_c_kernel(x_ref, wqt_ref, wk_ref, wvt_ref, qt_ref, k_ref, vt_ref, km_ref, *, nblk):
    xb = x_ref[...].astype(BF16)
    qt_ref[...] = _dot_nt(wqt_ref[...], xb)
    kf = _dot(xb, wk_ref[...])
    k_ref[...] = kf.astype(BF16)
    vt_ref[...] = _dot_nt(wvt_ref[...], xb).astype(BF16)
    for r in range(nblk):
        km_ref[r] = jnp.mean(kf[r * C_BLOCK:(r + 1) * C_BLOCK], axis=0, keepdims=True)


def _proj_c(x, wqt, wk, wvt, tm=512):
    t, k = x.shape
    d = wk.shape[1]
    nblk = tm // C_BLOCK
    return pl.pallas_call(
        functools.partial(_proj_c_kernel, nblk=nblk),
        grid=(t // tm,),
        in_specs=[pl.BlockSpec((tm, k), lambda i: (i, 0)), _resident((d, k)), _resident((k, d)),
                  _resident((d, k))],
        out_specs=[pl.BlockSpec((d, tm), lambda i: (0, i)), pl.BlockSpec((tm, d), lambda i: (i, 0)),
                   pl.BlockSpec((d, tm), lambda i: (0, i)), pl.BlockSpec((nblk, 1, d), lambda i: (i, 0, 0))],
        out_shape=[jax.ShapeDtypeStruct((d, t), F32), jax.ShapeDtypeStruct((t, d), BF16),
                   jax.ShapeDtypeStruct((d, t), BF16), jax.ShapeDtypeStruct((t // C_BLOCK, 1, d), F32)],
        compiler_params=_params("parallel"),
        name="proj_moba",
    )(x, wqt, wk, wvt)


_N_AUG = 16
_PAIR = 2 * HEAD_DIM


def _split3_np(a):
    bf = ml_dtypes.bfloat16
    h = a.astype(bf).astype(np.float32)
    m = (a - h).astype(bf).astype(np.float32)
    l = (a - h - m).astype(bf).astype(np.float32)
    return h, m, l


def _pack_rows_np(a):
    bits = a.astype(ml_dtypes.bfloat16).view(np.uint16).astype(np.uint32)
    return bits[0::2] | (bits[1::2] << 16)


def _band_tables(dist_scale):
    ki = np.arange(2 * BAND, dtype=np.float32)
    kaug = np.zeros((2 * BAND, _PAIR), np.float32)
    kaug[:, 0:3] = 1.0
    kaug[:, 3:6] = ki[:, None]
    qaug = np.zeros((N_HEADS, _N_AUG, BAND), np.float32)
    qpos = np.arange(BAND, dtype=np.float32) + BAND
    for h in range(N_HEADS):
        c = np.float32(_alibi_slope(h, N_HEADS) * dist_scale * LOG2E)
        qaug[h, 0:3] = np.stack(_split3_np(-c * qpos))
        qaug[h, 3:6] = np.stack(_split3_np(np.full((BAND,), c, np.float32)))
    return _pack_rows_np(kaug), np.stack([_pack_rows_np(qaug[h]) for h in range(N_HEADS)])


def _band_kernel(*refs, kv_group, max_dist, use_sinks, want_lse):
    q_ref, kp_ref, ko_ref, vp_ref, vo_ref, kaug_ref, qaug_ref = refs[:7]
    pos = 7
    sink_ref = None
    if use_sinks:
        sink_ref = refs[pos]
        pos += 1
    o_ref = refs[pos]
    lse_ref = refs[pos + 1] if want_lse else None

    b = pl.program_id(1)
    ki = lax.broadcasted_iota(jnp.int32, (2 * BAND, BAND), 0)
    qi = lax.broadcasted_iota(jnp.int32, (2 * BAND, BAND), 1)
    dist = qi + BAND - ki
    valid = (dist >= 0) & (dist <= max_dist) & (ki >= jnp.where(b > 0, 0, BAND))
    qt = pltpu.bitcast(q_ref[...], BF16)
    k_all = jnp.concatenate([pltpu.bitcast(kp_ref[...], BF16), pltpu.bitcast(ko_ref[...], BF16)], axis=0)
    vt_all = jnp.concatenate([pltpu.bitcast(vp_ref[...], BF16), pltpu.bitcast(vo_ref[...], BF16)], axis=1)
    kaug = pltpu.bitcast(kaug_ref[...], BF16)
    zslot = jnp.zeros((HEAD_DIM, BAND), BF16)
    zpad = jnp.zeros((_PAIR - _N_AUG, BAND), BF16)
    n_kv = N_HEADS // kv_group
    scores = [None] * N_HEADS
    for pair in range(-(-n_kv // 2)):
        heads = [h for h in range(N_HEADS) if (h // kv_group) // 2 == pair]
        qcats = []
        for h in heads:
            qh = qt[h * HEAD_DIM:(h + 1) * HEAD_DIM]
            slot = [qh, zslot] if (h // kv_group) % 2 == 0 else [zslot, qh]
            qcats.append(jnp.concatenate(slot + [pltpu.bitcast(qaug_ref[h], BF16), zpad], axis=0))
        kcat = jnp.concatenate([k_all[:, pair * _PAIR:(pair + 1) * _PAIR], kaug], axis=1)
        s_all = _dot(kcat, jnp.concatenate(qcats, axis=1))
        for n, h in enumerate(heads):
            scores[h] = s_all[:, n * BAND:(n + 1) * BAND]
    probs, dens, lses = [], [], []
    for h in range(N_HEADS):
        s = jnp.where(valid, scores[h], NEG_INF)
        m = jnp.max(s, axis=0, keepdims=True)
        if use_sinks:
            sk = sink_ref[:, h:h + 1] * LOG2E
            m = jnp.maximum(m, sk)
        p = jnp.exp2(s - m)
        den = jnp.sum(p, axis=0, keepdims=True)
        if use_sinks:
            den = den + jnp.exp2(sk - m)
        probs.append(p.astype(BF16))
        dens.append(den)
        if want_lse:
            lses.append((m + jnp.log2(den)) * (1.0 / LOG2E))
    outs = []
    for g in range(n_kv):
        hs = range(g * kv_group, (g + 1) * kv_group)
        pv = _dot(vt_all[g * HEAD_DIM:(g + 1) * HEAD_DIM], jnp.concatenate([probs[h] for h in hs], axis=1))
        outs += [pv[:, n * BAND:(n + 1) * BAND] / dens[h] for n, h in enumerate(hs)]
    o_ref[...] = jnp.concatenate(outs, axis=0).T.astype(o_ref.dtype)
    if want_lse:
        lse_ref[...] = jnp.concatenate(lses, axis=0)


def _band_attention(qt, k, vt, *, kv_group, dist_scale, max_dist, sinks=None, want_lse=False):
    r, _, l = qt.shape
    d = N_HEADS * HEAD_DIM
    kw = k.shape[2]
    prev = lambda c, b: jnp.maximum(b - 1, 0)
    kaug, qaug = _band_tables(dist_scale)
    in_specs = [pl.BlockSpec((None, d // 2, BAND), lambda c, b: (c, 0, b)),
                pl.BlockSpec((None, BAND // 2, kw), lambda c, b: (c, prev(c, b), 0)),
                pl.BlockSpec((None, BAND // 2, kw), lambda c, b: (c, b, 0)),
                pl.BlockSpec((None, kw // 2, BAND), lambda c, b: (c, 0, prev(c, b))),
                pl.BlockSpec((None, kw // 2, BAND), lambda c, b: (c, 0, b)),
                _resident(kaug.shape), _resident(qaug.shape)]
    args = [qt, k, k, vt, vt, jnp.asarray(kaug), jnp.asarray(qaug)]
    if sinks is not None:
        in_specs.append(_resident((1, N_HEADS)))
        args.append(sinks.reshape(1, N_HEADS).astype(F32))
    o_spec = pl.BlockSpec((BAND, d), lambda c, b: (b, c))
    out_specs = [o_spec]
    out_shape = [jax.ShapeDtypeStruct((l, r * d), BF16)]
    if want_lse:
        out_specs.append(pl.BlockSpec((None, N_HEADS, BAND), lambda c, b: (c, 0, b)))
        out_shape.append(jax.ShapeDtypeStruct((r, N_HEADS, l), F32))
    outs = pl.pallas_call(
        functools.partial(_band_kernel, kv_group=kv_group, max_dist=max_dist,
                          use_sinks=sinks is not None, want_lse=want_lse),
        grid=(r, l // BAND),
        in_specs=in_specs, out_specs=out_specs, out_shape=out_shape,
        compiler_params=_params("parallel", "parallel"),
        name="band_attention",
    )(*args)
    o = outs[0].reshape(l * r, d)
    if not want_lse:
        return [o]
    return [o, jnp.transpose(outs[1], (1, 2, 0)).reshape(N_HEADS, l * r)]


def _dot3(a, b):
    ah, al = _split(a)
    bh, bl = _split(b)
    return _dot(ah, bh) + (_dot(ah, bl) + _dot(al, bh))


def _dot3_nt(a, b):
    ah, al = _split(a)
    bh, bl = _split(b)
    return _dot_nt(ah, bh) + (_dot_nt(ah, bl) + _dot_nt(al, bh))


def _split3(a):
    h = a.astype(BF16).astype(F32)
    r = a - h
    m = r.astype(BF16).astype(F32)
    l = (r - m).astype(BF16).astype(F32)
    return h, m, l


_KV_STEP = 2 * C_BLOCK


def _moba_kernel(slope_ref, qt_ref, k_ref, vt_ref, km_ref, o_ref, sel_ref, s_ref, p_ref, *, nblk):
    hp = pl.program_id(0)
    qi = pl.program_id(1)
    tq = C_BLOCK
    pw = 2 * HEAD_DIM
    qt = qt_ref[...]
    rows = lax.broadcasted_iota(jnp.int32, (pw, tq), 0)
    blk = lax.broadcasted_iota(jnp.int32, (nblk, tq), 0)
    past = blk < qi
    aug_r = lax.broadcasted_iota(jnp.int32, (_N_AUG, tq), 0)
    tqf = lax.broadcasted_iota(jnp.int32, (_N_AUG, tq), 1).astype(F32)
    kcol = lax.broadcasted_iota(jnp.int32, (_KV_STEP, pw), 1)
    tkf = lax.broadcasted_iota(jnp.int32, (_KV_STEP, pw), 0).astype(F32)
    causal = (lax.broadcasted_iota(jnp.int32, (C_BLOCK, tq), 1)
              >= lax.broadcasted_iota(jnp.int32, (C_BLOCK, tq), 0))
    zpad = jnp.zeros((pw - _N_AUG, tq), BF16)
    k0 = pl.multiple_of(qi * C_BLOCK, C_BLOCK)
    k_own = k_ref[pl.ds(k0, C_BLOCK), :]

    qcat, kpos, c2, init = [], [], [], []
    for hh in range(2):
        c = slope_ref[2 * hp + hh] * LOG2E
        qm = jnp.where((rows >= hh * HEAD_DIM) & (rows < (hh + 1) * HEAD_DIM), qt, 0.0)
        gate = jnp.where(past, _dot3(km_ref[...], qm), NEG_INF)
        g = gate
        for r in range(C_TOPK):
            thr = jnp.max(g, axis=0, keepdims=True)
            if r < C_TOPK - 1:
                g = jnp.where(g >= thr, NEG_INF, g)
        sel_ref[hh] = jnp.where(past & (gate >= thr), 1.0, 0.0)

        qh, qmid, ql = _split3(-c * tqf)
        qpos = jnp.where(aug_r == 0, qh, jnp.where(aug_r == 1, qmid, jnp.where(aug_r == 2, ql,
                         jnp.where(aug_r < 6, 1.0, 0.0))))
        qb = (qm * (LOG2E * HEAD_DIM ** -0.5)).astype(BF16)
        qcat.append(jnp.concatenate([qb, qpos.astype(BF16), zpad], axis=0))
        kh, kmid, kl = _split3(c * tkf)
        kpos.append(jnp.where(kcol < 3, 1.0, jnp.where(kcol == 3, kh, jnp.where(kcol == 4, kmid,
                              jnp.where(kcol == 5, kl, 0.0)))).astype(BF16))
        c2.append(c)

        s = _dot(jnp.concatenate([k_own, kpos[hh][:C_BLOCK]], axis=1), qcat[hh])
        s = jnp.where(causal, s, NEG_INF)
        m0 = jnp.max(s, axis=0, keepdims=True)
        p = jnp.exp2(s - m0)
        l0 = jnp.sum(p, axis=0, keepdims=True)
        acc0 = _dot(vt_ref[pl.ds(hh * HEAD_DIM, HEAD_DIM), pl.ds(k0, C_BLOCK)], p.astype(BF16))
        init.append((m0, l0, acc0))

    nsub = _KV_STEP // C_BLOCK

    nsteps = (qi + nsub - 1) // nsub
    last_step = k_ref.shape[0] // _KV_STEP - 1

    def key_start(n):
        return pl.multiple_of(jnp.clip(n, 0, last_step) * _KV_STEP, _KV_STEP)

    def score_piece(n, slot, hh, j):
        rows = slice(j * C_BLOCK, (j + 1) * C_BLOCK)
        kb = k_ref[pl.ds(key_start(n) + j * C_BLOCK, C_BLOCK), :]
        s_ref[slot, hh, rows, :] = _dot(jnp.concatenate([kb, kpos[hh][rows]], axis=1), qcat[hh])

    def scores(n, slot):
        for hh in range(2):
            for j in range(nsub):
                score_piece(n, slot, hh, j)

    def pv(n, hh, slot):
        return _dot(vt_ref[pl.ds(hh * HEAD_DIM, HEAD_DIM), pl.ds(key_start(n), _KV_STEP)],
                    p_ref[slot, hh])

    def step(n, slot, carry):
        a_prev, state = carry
        off = (qi * C_BLOCK - n * _KV_STEP).astype(F32)
        new_a, new_state = [], []
        for hh in range(2):
            m_i, l_i, acc = state[hh]
            shift = c2[hh] * off
            sel = [sel_ref[hh, pl.ds(nsub * n + j, 1), :] > 0.5 for j in range(nsub)]
            m_new = m_i
            for j in range(nsub):
                score_piece(n + 1, 1 - slot, hh, j)
                mj = jnp.max(s_ref[slot, hh, j * C_BLOCK:(j + 1) * C_BLOCK, :], axis=0, keepdims=True)
                m_new = jnp.maximum(m_new, jnp.where(sel[j], mj - shift, NEG_INF))
            acc = a_prev[hh] * acc + pv(n - 1, hh, 1 - slot)
            l_new = jnp.exp2(m_i - m_new) * l_i
            for j in range(nsub):
                rows = slice(j * C_BLOCK, (j + 1) * C_BLOCK)
                p = jnp.exp2(s_ref[slot, hh, rows, :] - jnp.where(sel[j], m_new + shift, BIG))
                p_ref[slot, hh, rows, :] = p.astype(BF16)
                l_new = l_new + jnp.sum(p, axis=0, keepdims=True)
            new_a.append(jnp.exp2(m_i - m_new))
            new_state.append((m_new, l_new, acc))
        return tuple(new_a), tuple(new_state)

    def body(n2, carry):
        return step(2 * n2 + 1, 1, step(2 * n2, 0, carry))

    p_ref[1] = jnp.zeros_like(p_ref[1])
    scores(0, 0)
    one_a = jnp.ones((1, tq), F32)
    npairs = (nsteps + 1) // 2
    a_prev, state = lax.fori_loop(0, npairs, body, ((one_a, one_a), tuple(init)))
    outs = []
    for hh in range(2):
        _, l, acc = state[hh]
        outs.append((a_prev[hh] * acc + pv(2 * npairs - 1, hh, 1)) / l)
    o_ref[...] = jnp.concatenate(outs, axis=0).T.astype(o_ref.dtype)


def _moba_attention(qt, k, vt, kmean):
    d, t = qt.shape
    nblk = t // C_BLOCK
    slopes = jnp.asarray([_alibi_slope(h, N_HEADS) for h in range(N_HEADS)], F32)
    pw = 2 * HEAD_DIM
    assert t % (2 * _KV_STEP) == 0
    return pl.pallas_call(
        functools.partial(_moba_kernel, nblk=nblk),
        grid=(d // pw, nblk),
        in_specs=[pl.BlockSpec(memory_space=pltpu.SMEM),
                  pl.BlockSpec((pw, C_BLOCK), lambda hp, i: (hp, i)),
                  pl.BlockSpec((t, pw), lambda hp, i: (0, hp)),
                  pl.BlockSpec((pw, t), lambda hp, i: (hp, 0)),
                  pl.BlockSpec((nblk, pw), lambda hp, i: (0, hp))],
        out_specs=pl.BlockSpec((C_BLOCK, pw), lambda hp, i: (i, hp)),
        out_shape=jax.ShapeDtypeStruct((t, d), BF16),
        scratch_shapes=[pltpu.VMEM((2, nblk, C_BLOCK), F32), pltpu.VMEM((2, 2, _KV_STEP, C_BLOCK), F32),
                        pltpu.VMEM((2, 2, _KV_STEP, C_BLOCK), BF16)],
        compiler_params=_params("parallel", "arbitrary"),
        name="moba_attention",
    )(slopes, qt, k, vt, kmean)


def _layer_norm(z, g, b):
    mu = jnp.mean(z, axis=-1, keepdims=True)
    zc = z - mu
    var = jnp.mean(zc * zc, axis=-1, keepdims=True)
    return zc * lax.rsqrt(var + LN_EPS) * g + b


def _outproj_ln_kernel(*refs, n_groups, alpha):
    o_refs = refs[:n_groups]
    pos = n_groups
    if n_groups == 1:
        o = o_refs[0][...]
    else:
        lse_ref = refs[pos]
        pos += 1
        lses = [lse_ref[gi] for gi in range(n_groups)]
        m = functools.reduce(jnp.maximum, lses)
        es = [jnp.exp(l - m) for l in lses]
        den = functools.reduce(jnp.add, es)
        heads = lax.broadcasted_iota(jnp.int32, (N_HEADS, N_HEADS * HEAD_DIM), 0)
        cols = lax.broadcasted_iota(jnp.int32, (N_HEADS, N_HEADS * HEAD_DIM), 1)
        expand = jnp.where(cols // HEAD_DIM == heads, 1.0, 0.0).astype(BF16)
        tn = (((0,), (0,)), ((), ()))
        o = None
        for e, r in zip(es, o_refs):
            whi, wlo = _split(e / den)
            wexp = (lax.dot_general(whi, expand, tn, preferred_element_type=F32)
                    + lax.dot_general(wlo, expand, tn, preferred_element_type=F32))
            term = wexp * r[...].astype(F32)
            o = term if o is None else o + term
        o = o.astype(BF16)
    wo_ref, x_ref, g_ref, b_ref, xn_ref, xb_ref = refs[pos:pos + 6]
    y = _dot(o, wo_ref[...])
    xn = _layer_norm(alpha * x_ref[...] + y, g_ref[...], b_ref[...])
    xn_ref[...] = xn
    xb_ref[...] = _pack_rows(xn.T)


def _outproj_ln(os_, lses, wo, x, g, b, alpha, tm=512):
    t, d = x.shape
    n = len(os_)
    tile = pl.BlockSpec((tm, d), lambda i: (i, 0))
    lse_args = [jnp.stack(lses)] if lses else []
    lse_specs = [pl.BlockSpec((n, N_HEADS, tm), lambda i: (0, 0, i))] if lses else []
    return pl.pallas_call(
        functools.partial(_outproj_ln_kernel, n_groups=n, alpha=alpha),
        grid=(t // tm,),
        in_specs=[tile] * n + lse_specs + [_resident((d, d)), tile, _resident((1, d)), _resident((1, d))],
        out_specs=[tile, pl.BlockSpec((d // 2, tm), lambda i: (0, i))],
        out_shape=[jax.ShapeDtypeStruct((t, d), F32), jax.ShapeDtypeStruct((d // 2, t), jnp.uint32)],
        compiler_params=_params("parallel"),
        name="outproj_ln",
    )(*os_, *lse_args, wo, x, g.reshape(1, d), b.reshape(1, d))


def _pack_rows(a):
    return pltpu.bitcast(a.astype(BF16), jnp.uint32)


def _pack_dup(a):
    bits = pltpu.bitcast(a.astype(BF16).astype(F32), jnp.uint32)
    return bits | (bits >> 16)


def _pack_weight_kernel(w_ref, o_ref, *, transpose):
    w = w_ref[...]
    o_ref[...] = _pack_rows(w.T if transpose else w)


def _pack_weight(w, layer, transpose=False, tr=2048):
    _, r, c = w.shape
    if transpose:
        out_spec, out_shape = pl.BlockSpec((c // 2, tr), lambda i: (0, i)), (c // 2, r)
    else:
        out_spec, out_shape = pl.BlockSpec((tr // 2, c), lambda i: (i, 0)), (r // 2, c)
    return pl.pallas_call(
        functools.partial(_pack_weight_kernel, transpose=transpose),
        grid=(r // tr,),
        in_specs=[pl.BlockSpec((None, tr, c), lambda i: (layer, i, 0))],
        out_specs=out_spec,
        out_shape=jax.ShapeDtypeStruct(out_shape, jnp.uint32),
        compiler_params=_params("parallel"),
        name="pack_weight",
    )(w)


def _oddeven_merge(lo, hi, r):
    step = r * 2
    if step < hi - lo:
        yield from _oddeven_merge(lo, hi, step)
        yield from _oddeven_merge(lo + r, hi, step)
        yield from [(i, i + r) for i in range(lo + r, hi - r, step)]
    else:
        yield (lo, lo + r)


def _oddeven_sort(lo, hi):
    if hi > lo:
        mid = lo + (hi - lo) // 2
        yield from _oddeven_sort(lo, mid)
        yield from _oddeven_sort(mid + 1, hi)
        yield from _oddeven_merge(lo, hi, 1)


_SORT16 = tuple(_oddeven_sort(0, PEER_TOPK - 1))


def _top16_sorted(s):
    n = PEER_TOPK
    x = [s[_SUBLANES * j:_SUBLANES * (j + 1)] for j in range(n)]

    def cmpx(i, j):
        x[i], x[j] = jnp.maximum(x[i], x[j]), jnp.minimum(x[i], x[j])

    for i, j in _SORT16:
        cmpx(i, j)
    for shift in (4, 2, 1):
        y = [pltpu.roll(v, shift, 0) for v in x]
        x = [jnp.maximum(x[i], y[n - 1 - i]) for i in range(n)]
        stride = n // 2
        while stride:
            for i in range(n):
                if not i & stride:
                    cmpx(i, i + stride)
            stride //= 2
    top = [v[0:1] for v in x]
    nxt = jnp.max(jnp.where(s < top[n - 1], s, NEG_INF), axis=0, keepdims=True)
    return top + [nxt]


def _top_vals(s, k):
    out = []
    for r in range(k):
        m = jnp.max(s, axis=0, keepdims=True)
        out.append(m)
        if r < k - 1:
            s = jnp.where(s >= m, NEG_INF, s)
    return out


_CAND = [(i, j) for i in range(PEER_TOPK + 1) for j in range(PEER_TOPK + 1)
         if (i + 1) * (j + 1) <= PEER_TOPK + 1]
_NCAND = -(-len(_CAND) // 8) * 8


def _router_kernel(x_ref, wq_ref, sk_ref, r2_ref, e2_ref, n1_ref, w_ref, q_ref, cand_ref):
    q_ref[...] = _dot(x_ref[...].astype(BF16), wq_ref[...])
    tt = x_ref.shape[0]
    cand_ref[...] = jnp.full((_NCAND, tt), NEG_INF, F32)
    nk = PEER_NKEYS
    for h in range(PEER_HEADS):
        s1 = _dot3_nt(sk_ref[0], q_ref[:, (2 * h) * nk:(2 * h + 1) * nk])
        s2 = _dot3_nt(sk_ref[1], q_ref[:, (2 * h + 1) * nk:(2 * h + 2) * nk])
        a = _top16_sorted(s1)
        b = _top16_sorted(s2)
        for r, (i, j) in enumerate(_CAND):
            cand_ref[r:r + 1, :] = a[i] + b[j]
        v = _top_vals(cand_ref[...], PEER_TOPK + 1)
        thr = 0.5 * (v[PEER_TOPK - 1] + v[PEER_TOPK])
        z = functools.reduce(jnp.add, [jnp.exp(v[r] - v[0]) for r in range(PEER_TOPK)])
        cut = thr - s1
        rank2 = jnp.zeros_like(s2)
        count1 = jnp.zeros_like(s1)
        for r in range(PEER_TOPK):
            rank2 = jnp.where(s2 < b[r], r + 1.0, rank2)
            count1 = jnp.where(b[r] >= cut, r + 1.0, count1)
        r2_ref[h] = _pack_rows(rank2)
        e2_ref[h] = _pack_rows(jnp.exp(s2 - b[0]))
        n1_ref[h] = _pack_dup(count1)
        w_ref[h] = _pack_dup(jnp.exp(s1 - a[0]) / z)


def _router(xn, wq, subkeys, tt=256):
    t, d = xn.shape
    nq = wq.shape[1]
    shape = (PEER_HEADS, PEER_NKEYS, t)
    ospec = pl.BlockSpec((PEER_HEADS, PEER_NKEYS, tt), lambda i: (0, 0, i))
    pspec = pl.BlockSpec((PEER_HEADS, PEER_NKEYS // 2, tt), lambda i: (0, 0, i))
    packed = jax.ShapeDtypeStruct((PEER_HEADS, PEER_NKEYS // 2, t), jnp.uint32)
    return pl.pallas_call(
        _router_kernel,
        grid=(t // tt,),
        in_specs=[pl.BlockSpec((tt, d), lambda i: (i, 0)), _resident((d, nq)), _resident(subkeys.shape)],
        out_specs=[pspec, pspec, ospec, ospec],
        out_shape=[packed, packed, jax.ShapeDtypeStruct(shape, jnp.uint32),
                   jax.ShapeDtypeStruct(shape, jnp.uint32)],
        scratch_shapes=[pltpu.VMEM((tt, nq), F32), pltpu.VMEM((_NCAND, tt), F32)],
        compiler_params=_params("parallel"),
        name="peer_router",
    )(xn, wq, subkeys)


def _gelu(x):
    return 0.5 * x * (1.0 + lax.erf(x * (2.0 ** -0.5)))


_BF16_ROWS = 16
_PEER_SUB = 512
_PEER_COLS = 256
_PEER_KQ = 256


def _peer_kernel(x_ref, u_ref, vt_ref, r2_ref, e2_ref, n1_ref, w_ref, y_ref, acc_ref, h_ref, act_ref,
                 *, n_tiles, n_e):
    s = pl.program_id(0)
    _, te, tt = h_ref.shape
    nk = PEER_NKEYS
    j_c = jnp.clip(s - 2, 0, n_tiles - 1) % n_e

    @pl.when(s == 0)
    def _():
        h_ref[...] = jnp.zeros_like(h_ref)
        act_ref[...] = jnp.zeros_like(act_ref)

    @pl.when(j_c == 0)
    def _():
        acc_ref[...] = jnp.zeros_like(acc_ref)

    n_i1 = te // nk

    def first_matmul(slot, k):
        xt = pltpu.bitcast(x_ref[...], BF16)
        uk = pltpu.bitcast(u_ref[pl.ds(k * _PEER_SUB // 2, _PEER_SUB // 2), :], BF16)
        h_ref[slot, pl.ds(k * _PEER_SUB, _PEER_SUB), :] = _dot(uk, xt)

    def second_matmul(slot, n, kq):
        cols = pl.ds(n * _PEER_COLS, _PEER_COLS)
        krows = pl.ds(kq * _PEER_KQ, _PEER_KQ)
        acc_ref[:, cols] += _dot(pltpu.bitcast(vt_ref[:, krows], BF16), act_ref[slot, krows, cols])

    def gate(slot, c, v):
        cols = pl.ds(c * 128, 128)
        prow = pl.ds(v * _SUBLANES, _SUBLANES)
        g = [jnp.zeros((_BF16_ROWS, 128), BF16) for _ in range(n_i1)]
        for h in range(PEER_HEADS):
            r2 = pltpu.bitcast(r2_ref[h, prow, cols], BF16)
            e2 = pltpu.bitcast(e2_ref[h, prow, cols], BF16)
            for ii in range(n_i1):
                row = pl.ds(ii, 1)
                n_row = pltpu.bitcast(jnp.broadcast_to(n1_ref[h, row, cols], (_SUBLANES, 128)), BF16)
                w_row = pltpu.bitcast(jnp.broadcast_to(w_ref[h, row, cols], (_SUBLANES, 128)), BF16)
                g[ii] = g[ii] + jnp.where(r2 < n_row, e2, 0.0) * w_row
        for ii in range(n_i1):
            rows = pl.ds(ii * nk + v * _BF16_ROWS, _BF16_ROWS)
            act_ref[slot, rows, cols] = _gelu(h_ref[slot, rows, cols]).astype(BF16) * g[ii]

    def stages(slot):
        pieces = [functools.partial(first_matmul, slot, k) for k in range(te // _PEER_SUB)]
        pieces += [functools.partial(second_matmul, slot, n, kq)
                   for n in range(tt // _PEER_COLS) for kq in range(te // _PEER_KQ)]
        blocks = [(c, v) for c in range(tt // 128) for v in range(nk // _BF16_ROWS)]
        per = -(-len(blocks) // len(pieces))
        for p, piece in enumerate(pieces):
            piece()
            for c, v in blocks[p * per:(p + 1) * per]:
                gate(1 - slot, c, v)

    @pl.when(s % 2 == 0)
    def _():
        stages(0)

    @pl.when(s % 2 == 1)
    def _():
        stages(1)

    @pl.when(j_c == n_e - 1)
    def _():
        y_ref[...] = acc_ref[...].T


def _peer_experts(xbt, u, vt, r2, e2, n1, w, tt=512, te=2048):
    t, d = xbt.shape[1], 2 * xbt.shape[0]
    ne = 2 * u.shape[0]
    n_e = ne // te
    n_tiles = (t // tt) * n_e
    pair = lambda lag: (lambda s: jnp.clip(s - lag, 0, n_tiles - 1))
    tok = lambda lag: (lambda s: pair(lag)(s) // n_e)
    exp = lambda lag: (lambda s: pair(lag)(s) % n_e)
    wspec = lambda lag: pl.BlockSpec((te // 2, d), lambda s: (exp(lag)(s), 0))
    pspec = pl.BlockSpec((PEER_HEADS, PEER_NKEYS // 2, tt), lambda s: (0, 0, tok(1)(s)))
    rspec = pl.BlockSpec((PEER_HEADS, te // PEER_NKEYS, tt), lambda s: (0, exp(1)(s), tok(1)(s)))
    return pl.pallas_call(
        functools.partial(_peer_kernel, n_tiles=n_tiles, n_e=n_e),
        grid=(n_tiles + 2,),
        in_specs=[pl.BlockSpec((d // 2, tt), lambda s: (0, tok(0)(s))), wspec(0),
                  pl.BlockSpec((d // 2, te), lambda s: (0, exp(2)(s))),
                  pspec, pspec, rspec, rspec],
        out_specs=pl.BlockSpec((tt, d), lambda s: (tok(2)(s), 0)),
        out_shape=jax.ShapeDtypeStruct((t, d), F32),
        scratch_shapes=[pltpu.VMEM((d, tt), F32), pltpu.VMEM((2, te, tt), F32), pltpu.VMEM((2, te, tt), BF16)],
        compiler_params=_params("arbitrary"),
        name="peer_experts",
    )(xbt, u, vt, r2, e2, n1, w)


def _ln_ple_kernel(x_ref, y_ref, g_ref, b_ref, p_ref, wp_ref, wg_ref, bg_ref, o_ref, *, alpha):
    xn = _layer_norm(alpha * x_ref[...] + y_ref[...], g_ref[...], b_ref[...])
    gate = jax.nn.sigmoid(_dot(xn.astype(BF16), wg_ref[...]) + bg_ref[...])
    o_ref[...] = xn + gate * _dot(p_ref[...].astype(BF16), wp_ref[...])


def _ln_ple(x, y, g, b, p, layer, wp, wg, bg, alpha, tm=512):
    t, d = x.shape
    dp = p.shape[2]
    tile = pl.BlockSpec((tm, d), lambda i: (i, 0))
    vec = _resident((1, d))
    return pl.pallas_call(
        functools.partial(_ln_ple_kernel, alpha=alpha),
        grid=(t // tm,),
        in_specs=[tile, tile, vec, vec, pl.BlockSpec((None, tm, dp), lambda i: (layer, i, 0)), _resident((dp, d)),
                  _resident((d, d)), vec],
        out_specs=tile,
        out_shape=jax.ShapeDtypeStruct((t, d), F32),
        compiler_params=_params("parallel"),
        name="ln_ple",
    )(x, y, g.reshape(1, d), b.reshape(1, d), p, wp, wg, bg.reshape(1, d))


def _mixer_a(x, w_qkv, sinks):
    d = N_HEADS * HEAD_DIM
    kvw = A_KV_HEADS * HEAD_DIM
    wq, wk, wv = w_qkv[:, :d], w_qkv[:, d:d + kvw], w_qkv[:, d + kvw:]
    qt, k, vt = _proj_t(x, wq.T.astype(BF16), wk.astype(BF16), wv.T.astype(BF16))
    o, = _band_attention(qt, k, vt, kv_group=N_HEADS // A_KV_HEADS, dist_scale=1.0,
                         max_dist=A_WINDOW - 1, sinks=sinks)
    return [o], []


def _mixer_b(x, w_qkv):
    d = N_HEADS * HEAD_DIM
    ng = len(B_PATTERNS)
    wk = w_qkv[:, ng * d:(ng + 1) * d].astype(BF16)
    wvt = w_qkv[:, (ng + 1) * d:].T.astype(BF16)
    os_, lses = [], []
    for gi, (w, r) in enumerate(B_PATTERNS):
        qt, k, vt = _proj_t(x, w_qkv[:, gi * d:(gi + 1) * d].T.astype(BF16), wk, wvt, dil=r)
        o, lse = _band_attention(qt, k, vt, kv_group=1, dist_scale=float(r), max_dist=w // r, want_lse=True)
        os_.append(o)
        lses.append(lse)
    return os_, lses


def _mixer_c(x, w_qkv):
    d = N_HEADS * HEAD_DIM
    wq, wk, wv = w_qkv[:, :d], w_qkv[:, d:2 * d], w_qkv[:, 2 * d:]
    qt, k, vt, kmean = _proj_c(x, wq.T.astype(BF16), wk.astype(BF16), wv.T.astype(BF16))
    return [_moba_attention(qt, k, vt, kmean.reshape(-1, d))], []


def kernel(x, p, a_w_qkv, a_sinks, a_w_o, b_w_qkv, b_w_o, c_w_qkv, c_w_o, ln1_g, ln1_b, ln2_g, ln2_b,
           peer_w_q, peer_subkeys, peer_u, peer_v, ple_w, ple_gate_w, ple_gate_b):
    depth = p.shape[0]
    alpha = (2 * depth) ** 0.25
    bsz, seq, d = x.shape
    assert bsz == 1 and seq % (B_PATTERNS[-1][1] * BAND) == 0
    xt = x.reshape(seq, d)
    for i in range(depth):
        kind, j = i % 3, i // 3
        if kind == 0:
            os_, lses = _mixer_a(xt, a_w_qkv[j], a_sinks[j])
            wo = a_w_o[j]
        elif kind == 1:
            os_, lses = _mixer_b(xt, b_w_qkv[j])
            wo = b_w_o[j]
        else:
            os_, lses = _mixer_c(xt, c_w_qkv[j])
            wo = c_w_o[j]
        x1, x1b = _outproj_ln(os_, lses, wo.astype(BF16), xt, ln1_g[i], ln1_b[i], alpha)
        routing = _router(x1, peer_w_q[i].astype(BF16), peer_subkeys[i])
        y = _peer_experts(x1b, _pack_weight(peer_u, i), _pack_weight(peer_v, i, transpose=True), *routing)
        xt = _ln_ple(x1, y, ln2_g[i], ln2_b[i], p.reshape(depth, seq, -1), i, ple_w[i].astype(BF16),
                     ple_gate_w[i].astype(BF16), ple_gate_b[i], alpha)
    return xt.reshape(bsz, seq, d)
```
